```python
import math
import jax, jax.numpy as jnp
from jax import lax
import numpy as np

D_MODEL = 2048
BATCH = 4
SEQ = 2048
DEPTH = 2

N_MIXERS = 2
ATTN_HEAD_DIM = 64
ATTN_Q_HEADS = D_MODEL // ATTN_HEAD_DIM
ATTN_KV_HEADS = 8
ATTN_GROUP = ATTN_Q_HEADS // ATTN_KV_HEADS
WINDOW = 128
ATTN_BLOCK = 128
ROPE_THETA = 10000.0
ATTN_QKV_W = (ATTN_Q_HEADS + 2 * ATTN_KV_HEADS) * ATTN_HEAD_DIM
MLSTM_HEADS = 4
MLSTM_DV = D_MODEL // MLSTM_HEADS
MLSTM_DK = MLSTM_DV // 2
MLSTM_CHUNK = 64
MLSTM_QK_W = MLSTM_HEADS * MLSTM_DK
MLSTM_V_W = MLSTM_HEADS * MLSTM_DV
MLSTM_IN_W = 2 * MLSTM_QK_W + 2 * MLSTM_V_W + 4 * MLSTM_HEADS
D_FF = 4 * D_MODEL
DEEPNORM_ALPHA = (2.0 * DEPTH) ** 0.25
DEEPNORM_BETA = (8.0 * DEPTH) ** -0.25
LN_EPS = 1e-5
HEAD_NORM_EPS = 1e-6

kernel_name = 'hybrid_swa_sink_mlstm_deepnorm_adaln'


def layer_norm(x, g, b):
    xf = x.astype(jnp.float32)
    mu = jnp.mean(xf, axis=-1, keepdims=True)
    var = jnp.mean(jnp.square(xf - mu), axis=-1, keepdims=True)
    y = (xf - mu) * lax.rsqrt(var + LN_EPS)
    return (y * g.astype(jnp.float32) + b.astype(jnp.float32)).astype(x.dtype)


def rope_cos_sin(positions):
    inv_freq = 1.0 / (ROPE_THETA ** (jnp.arange(0, ATTN_HEAD_DIM, 2, dtype=jnp.float32) / ATTN_HEAD_DIM))
    ang = positions.astype(jnp.float32)[..., None] * inv_freq
    ang = jnp.concatenate([ang, ang], axis=-1)
    return jnp.cos(ang), jnp.sin(ang)


def apply_rope(t, cos, sin):
    t1, t2 = jnp.split(t, 2, axis=-1)
    rot = jnp.concatenate([-t2, t1], axis=-1)
    return (t * cos + rot * sin).astype(t.dtype)


def windowed_gqa_sink(h, positions, w_qkv, b_qkv, sink, w_o, b_o):
    B, S, _ = h.shape
    nb = S // ATTN_BLOCK
    proj = h @ w_qkv + b_qkv
    q, k, v = jnp.split(proj, [ATTN_Q_HEADS * ATTN_HEAD_DIM,
                               (ATTN_Q_HEADS + ATTN_KV_HEADS) * ATTN_HEAD_DIM], axis=-1)
    q = q.reshape(B, S, ATTN_KV_HEADS, ATTN_GROUP, ATTN_HEAD_DIM)
    k = k.reshape(B, S, ATTN_KV_HEADS, ATTN_HEAD_DIM)
    v = v.reshape(B, S, ATTN_KV_HEADS, ATTN_HEAD_DIM)
    cos, sin = rope_cos_sin(positions)
    q = apply_rope(q, cos[:, :, None, None, :], sin[:, :, None, None, :])
    k = apply_rope(k, cos[:, :, None, :], sin[:, :, None, :])
    qb = q.reshape(B, nb, ATTN_BLOCK, ATTN_KV_HEADS, ATTN_GROUP, ATTN_HEAD_DIM)
    pad = ((0, 0), (ATTN_BLOCK, ATTN_BLOCK), (0, 0), (0, 0))
    kp = jnp.pad(k, pad).reshape(B, nb + 2, ATTN_BLOCK, ATTN_KV_HEADS, ATTN_HEAD_DIM)
    vp = jnp.pad(v, pad).reshape(B, nb + 2, ATTN_BLOCK, ATTN_KV_HEADS, ATTN_HEAD_DIM)
    kw = jnp.concatenate([kp[:, :-2], kp[:, 1:-1], kp[:, 2:]], axis=2)
    vw = jnp.concatenate([vp[:, :-2], vp[:, 1:-1], vp[:, 2:]], axis=2)
    s = jnp.einsum('bnqhgd,bnkhd->bnhgqk', qb, kw).astype(jnp.float32) * (ATTN_HEAD_DIM ** -0.5)
    blk = jnp.arange(nb)[:, None, None] * ATTN_BLOCK
    qpos = blk + jnp.arange(ATTN_BLOCK)[None, :, None]
    kpos = blk - ATTN_BLOCK + jnp.arange(3 * ATTN_BLOCK)[None, None, :]
    valid = (jnp.abs(kpos - qpos) <= WINDOW) & (kpos >= 0) & (kpos < S)
    s = jnp.where(valid[None, :, None, None], s, -jnp.inf)
    sk = sink.astype(jnp.float32).reshape(1, 1, ATTN_KV_HEADS, ATTN_GROUP, 1, 1)
    m = jnp.maximum(jnp.max(s, axis=-1, keepdims=True), sk)
    p = jnp.exp(s - m)
    p = p / (jnp.sum(p, axis=-1, keepdims=True) + jnp.exp(sk - m))
    o = jnp.einsum('bnhgqk,bnkhd->bnqhgd', p.astype(vw.dtype), vw)
    o = o.reshape(B, S, ATTN_Q_HEADS * ATTN_HEAD_DIM)
    return o @ w_o + b_o


def mlstm_chunkwise(q, k, v, log_i, log_f):
    Z, S, H, _ = q.shape
    L = MLSTM_CHUNK
    nc = S // L
    def to_chunks(t):
        t = t.reshape((Z, nc, L) + t.shape[2:])
        return jnp.moveaxis(jnp.swapaxes(t, 2, 3), 1, 0)
    qc, kc, vc = to_chunks(q), to_chunks(k), to_chunks(v)
    lic, lfc = to_chunks(log_i), to_chunks(log_f)
    tril = jnp.tril(jnp.ones((L, L), dtype=bool))

    def body(carry, inp):
        C, n, m = carry
        qq, kk, vv, li, lf = inp
        g = jnp.cumsum(lf, axis=-1)
        G = g[..., -1]
        a = g + m[..., None]
        Dm = g[..., :, None] - g[..., None, :] + li[..., None, :]
        Dm = jnp.where(tril, Dm, -jnp.inf)
        m_t = jnp.maximum(a, jnp.max(Dm, axis=-1))
        ea = jnp.exp(a - m_t)
        sc = jnp.einsum('zhtd,zhsd->zhts', qq, kk) * jnp.exp(Dm - m_t[..., None])
        num = ea[..., None] * jnp.einsum('zhtd,zhdv->zhtv', qq, C) + jnp.einsum('zhts,zhsv->zhtv', sc, vv)
        den = ea * jnp.einsum('zhtd,zhd->zht', qq, n) + jnp.sum(sc, axis=-1)
        hh = num / jnp.maximum(jnp.abs(den), jnp.exp(-m_t))[..., None]
        w_log = G[..., None] - g + li
        m_new = jnp.maximum(G + m, jnp.max(w_log, axis=-1))
        decay = jnp.exp(G + m - m_new)
        w = jnp.exp(w_log - m_new[..., None])
        C_new = decay[..., None, None] * C + jnp.einsum('zhs,zhsd,zhsv->zhdv', w, kk, vv)
        n_new = decay[..., None] * n + jnp.einsum('zhs,zhsd->zhd', w, kk)
        return (C_new, n_new, m_new), hh

    init = (jnp.zeros((Z, H, MLSTM_DK, MLSTM_DV), jnp.float32),
            jnp.zeros((Z, H, MLSTM_DK), jnp.float32),
            jnp.full((Z, H), -1e30, jnp.float32))
    _, hs = lax.scan(body, init, (qc, kc, vc, lic, lfc))
    hs = jnp.swapaxes(jnp.moveaxis(hs, 0, 1), 2, 3)
    return hs.reshape(Z, S, H, MLSTM_DV)


def bidir_mlstm(h, w_in, b_in, norm_w, w_o, b_o):
    B, S, _ = h.shape
    proj = h @ w_in + b_in
    q, k, v, o, gates = jnp.split(proj, [MLSTM_QK_W, 2 * MLSTM_QK_W, 2 * MLSTM_QK_W + MLSTM_V_W,
                                         2 * MLSTM_QK_W + 2 * MLSTM_V_W], axis=-1)
    f32 = jnp.float32
    q = q.astype(f32).reshape(B, S, MLSTM_HEADS, MLSTM_DK) * (MLSTM_DK ** -0.5)
    k = k.astype(f32).reshape(B, S, MLSTM_HEADS, MLSTM_DK)
    v = v.astype(f32).reshape(B, S, MLSTM_HEADS, MLSTM_DV)
    gates = gates.astype(f32).reshape(B, S, 4, MLSTM_HEADS)
    both = lambda fw, bw: jnp.concatenate([fw, jnp.flip(bw, axis=1)], axis=0)
    log_i = both(gates[:, :, 0], gates[:, :, 2])
    log_f = jax.nn.log_sigmoid(both(gates[:, :, 1], gates[:, :, 3]))
    hz = mlstm_chunkwise(both(q, q), both(k, k), both(v, v), log_i, log_f)
    hsum = hz[:B] + jnp.flip(hz[B:], axis=1)
    mu = jnp.mean(hsum, axis=-1, keepdims=True)
    var = jnp.mean(jnp.square(hsum - mu), axis=-1, keepdims=True)
    hn = (hsum - mu) * lax.rsqrt(var + HEAD_NORM_EPS)
    hn = hn * norm_w.astype(f32).reshape(MLSTM_HEADS, MLSTM_DV)
    y = jax.nn.sigmoid(o.astype(f32)) * hn.reshape(B, S, MLSTM_V_W)
    return y.astype(h.dtype) @ w_o + b_o


def sq_relu_mlp(h, w1, b1, w2, b2):
    u = jnp.square(jax.nn.relu(h @ w1 + b1))
    return u @ w2 + b2


def setup_inputs(seed: int = 0) -> dict:
    key = jax.random.key(seed)
    ks = jax.random.split(key, 24)
    n_attn = (DEPTH + 1) // N_MIXERS
    n_ml = DEPTH // N_MIXERS
    f32 = jnp.float32
    def w(k, shape, fan_in, gain=1.0):
        return jax.random.normal(k, shape, f32) * (gain * fan_in ** -0.5)
    def small(k, shape, s=0.02):
        return jax.random.normal(k, shape, f32) * s
    x = jax.random.normal(ks[0], (BATCH, SEQ, D_MODEL), f32)
    c = jax.random.normal(ks[1], (BATCH, D_MODEL), f32)
    positions = jnp.broadcast_to(jnp.arange(SEQ, dtype=jnp.int32)[None, :], (BATCH, SEQ))
    attn_w_qkv = w(ks[2], (n_attn, D_MODEL, ATTN_QKV_W), D_MODEL)
    attn_b_qkv = small(ks[3], (n_attn, ATTN_QKV_W))
    attn_sink = jax.random.normal(ks[4], (n_attn, ATTN_Q_HEADS), f32)
    attn_w_o = w(ks[5], (n_attn, ATTN_Q_HEADS * ATTN_HEAD_DIM, D_MODEL), ATTN_Q_HEADS * ATTN_HEAD_DIM, DEEPNORM_BETA)
    attn_b_o = small(ks[6], (n_attn, D_MODEL))
    mlstm_w_in = w(ks[7], (n_ml, D_MODEL, MLSTM_IN_W), D_MODEL)
    f_bias = jnp.linspace(3.0, 6.0, MLSTM_HEADS, dtype=f32)
    gate_bias = jnp.zeros((4, MLSTM_HEADS), f32).at[1].set(f_bias).at[3].set(f_bias)
    base_bias = jnp.concatenate([jnp.zeros((MLSTM_IN_W - 4 * MLSTM_HEADS,), f32), gate_bias.reshape(-1)])
    mlstm_b_in = small(ks[8], (n_ml, MLSTM_IN_W), 0.1) + base_bias
    mlstm_norm_w = 1.0 + small(ks[9], (n_ml, MLSTM_V_W))
    mlstm_w_o = w(ks[10], (n_ml, MLSTM_V_W, D_MODEL), MLSTM_V_W, DEEPNORM_BETA)
    mlstm_b_o = small(ks[11], (n_ml, D_MODEL))
    mod_w = w(ks[12], (DEPTH, D_MODEL, 6 * D_MODEL), D_MODEL, 0.5)
    mod_b = small(ks[13], (DEPTH, 6 * D_MODEL))
    mlp_w1 = w(ks[14], (DEPTH, D_MODEL, D_FF), D_MODEL)
    mlp_b1 = small(ks[15], (DEPTH, D_FF))
    mlp_w2 = w(ks[16], (DEPTH, D_FF, D_MODEL), D_FF, DEEPNORM_BETA)
    mlp_b2 = small(ks[17], (DEPTH, D_MODEL))
    ln_mix_g = 1.0 + small(ks[18], (DEPTH, D_MODEL))
    ln_mix_b = small(ks[19], (DEPTH, D_MODEL))
    ln_mlp_g = 1.0 + small(ks[20], (DEPTH, D_MODEL))
    ln_mlp_b = small(ks[21], (DEPTH, D_MODEL))
    return {'x': x, 'c': c, 'positions': positions,
            'attn_w_qkv': attn_w_qkv, 'attn_b_qkv': attn_b_qkv, 'attn_sink': attn_sink,
            'attn_w_o': attn_w_o, 'attn_b_o': attn_b_o,
            'mlstm_w_in': mlstm_w_in, 'mlstm_b_in': mlstm_b_in, 'mlstm_norm_w': mlstm_norm_w,
            'mlstm_w_o': mlstm_w_o, 'mlstm_b_o': mlstm_b_o,
            'mod_w': mod_w, 'mod_b': mod_b,
            'mlp_w1': mlp_w1, 'mlp_b1': mlp_b1, 'mlp_w2': mlp_w2, 'mlp_b2': mlp_b2,
            'ln_mix_g': ln_mix_g, 'ln_mix_b': ln_mix_b, 'ln_mlp_g': ln_mlp_g, 'ln_mlp_b': ln_mlp_b}


def reference(x, c, positions, attn_w_qkv, attn_b_qkv, attn_sink, attn_w_o, attn_b_o,
              mlstm_w_in, mlstm_b_in, mlstm_norm_w, mlstm_w_o, mlstm_b_o,
              mod_w, mod_b, mlp_w1, mlp_b1, mlp_w2, mlp_b2,
              ln_mix_g, ln_mix_b, ln_mlp_g, ln_mlp_b):
    c_act = jax.nn.silu(c)
    for i in range(DEPTH):
        mod = c_act @ mod_w[i] + mod_b[i]
        sh_m, sc_m, g_m, sh_f, sc_f, g_f = jnp.split(mod, 6, axis=-1)
        hmix = x * (1.0 + sc_m[:, None, :]) + sh_m[:, None, :]
        j = i // N_MIXERS
        if i % N_MIXERS == 0:
            y = windowed_gqa_sink(hmix, positions, attn_w_qkv[j], attn_b_qkv[j], attn_sink[j],
                                  attn_w_o[j], attn_b_o[j])
        else:
            y = bidir_mlstm(hmix, mlstm_w_in[j], mlstm_b_in[j], mlstm_norm_w[j],
                            mlstm_w_o[j], mlstm_b_o[j])
        x = layer_norm(DEEPNORM_ALPHA * x + (1.0 + g_m[:, None, :]) * y, ln_mix_g[i], ln_mix_b[i])
        hff = x * (1.0 + sc_f[:, None, :]) + sh_f[:, None, :]
        y = sq_relu_mlp(hff, mlp_w1[i], mlp_b1[i], mlp_w2[i], mlp_b2[i])
        x = layer_norm(DEEPNORM_ALPHA * x + (1.0 + g_f[:, None, :]) * y, ln_mlp_g[i], ln_mlp_b[i])
    return x
```

```python
import functools

import numpy as np
import jax
import jax.numpy as jnp
from jax import lax
from jax.experimental import pallas as pl
from jax.experimental.pallas import tpu as pltpu

F32 = jnp.float32
BF16 = jnp.bfloat16

DEPTH = 2
ATTN_HEAD_DIM = 64
ATTN_KV_HEADS = 8
WINDOW = 128
ATTN_BLOCK = 128
ROPE_THETA = 10000.0
MLSTM_HEADS = 4
MLSTM_GATES = 4 * MLSTM_HEADS
DEEPNORM_ALPHA = (2.0 * DEPTH) ** 0.25
LN_EPS = 1e-5
HEAD_NORM_EPS = 1e-6
MLSTM_KERNEL_CHUNK = 256

V7X_LANES = 128
V7X_VMEM_BYTES = 64 * 1024 * 1024
VMEM_LIMIT_CAP = 60000 * 1024


def _params(semantics, vmem_bytes):
    limit = min(int(vmem_bytes * 1.25) + (4 << 20), VMEM_LIMIT_CAP)
    return pltpu.CompilerParams(dimension_semantics=semantics, vmem_limit_bytes=limit)


def _mod_kernel(c_ref, w_ref, b_ref, o_ref):
    c = c_ref[...]
    c_act = (c * jax.nn.sigmoid(c)).astype(BF16)
    o_ref[0] = jnp.dot(c_act, w_ref[0].astype(BF16), preferred_element_type=F32) + b_ref[0]


def _mod_call(c_pad, mod_w, mod_b):
    depth, d, n = mod_w.shape
    rows = c_pad.shape[0]
    tn = 1024
    return pl.pallas_call(
        _mod_kernel,
        grid=(depth, n // tn),
        in_specs=[
            pl.BlockSpec((rows, d), lambda i, j: (0, 0)),
            pl.BlockSpec((1, d, tn), lambda i, j: (i, 0, j)),
            pl.BlockSpec((1, 1, tn), lambda i, j: (i, 0, j)),
        ],
        out_specs=pl.BlockSpec((1, rows, tn), lambda i, j: (i, 0, j)),
        out_shape=jax.ShapeDtypeStruct((depth, rows, n), F32),
        compiler_params=_params(("arbitrary", "arbitrary"), 2 * d * tn * 4 + d * tn * 2),
        name="adaln_mod",
    )(c_pad, mod_w, mod_b.reshape(depth, 1, n))


def _modulate_kernel(x_ref, sc_ref, sh_ref, o_ref):
    o_ref[...] = (x_ref[...] * (1.0 + sc_ref[0]) + sh_ref[0]).astype(o_ref.dtype)


def _modulate_call(x2, sc, sh, rows_per_batch):
    m, d = x2.shape
    tm = 512
    tpb = rows_per_batch // tm
    vec = pl.BlockSpec((1, 1, d), lambda i: (i // tpb, 0, 0))
    return pl.pallas_call(
        _modulate_kernel,
        grid=(m // tm,),
        in_specs=[pl.BlockSpec((tm, d), lambda i: (i, 0)), vec, vec],
        out_specs=pl.BlockSpec((tm, d), lambda i: (i, 0)),
        out_shape=jax.ShapeDtypeStruct((m, d), BF16),
        compiler_params=_params(("arbitrary",), 2 * tm * d * 6),
        name="modulate_in",
    )(x2, sc, sh)


def _mm_kernel(x_ref, w_ref, b_ref, o_ref, wb_ref, *, act):
    @pl.when(pl.program_id(1) == 0)
    def _():
        wb_ref[...] = w_ref[...].astype(BF16)

    acc = jnp.dot(x_ref[...], wb_ref[...], preferred_element_type=F32) + b_ref[...]
    if act == "relu2":
        acc = jnp.square(jnp.maximum(acc, 0.0))
    o_ref[...] = acc.astype(o_ref.dtype)


def _mm_call(x, w, b, *, n_cols, out_dtype, act=None, tm=1024, tn=1024, name):
    m, k = x.shape
    osz = jnp.dtype(out_dtype).itemsize
    vmem = 2 * k * tn * 4 + k * tn * 2 + 2 * tm * k * 2 + 2 * tm * tn * osz + tm * tn * 4
    return pl.pallas_call(
        functools.partial(_mm_kernel, act=act),
        grid=(n_cols // tn, m // tm),
        in_specs=[
            pl.BlockSpec((tm, k), lambda j, i: (i, 0)),
            pl.BlockSpec((k, tn), lambda j, i: (0, j)),
            pl.BlockSpec((1, tn), lambda j, i: (0, j)),
        ],
        out_specs=pl.BlockSpec((tm, tn), lambda j, i: (i, j)),
        out_shape=jax.ShapeDtypeStruct((m, n_cols), out_dtype),
        scratch_shapes=[pltpu.VMEM((k, tn), BF16)],
        compiler_params=_params(("arbitrary", "arbitrary"), vmem),
        name=name,
    )(x, w, b.reshape(1, -1))


def _head_norm_gate(hs, og, nw, heads):
    dv = hs.shape[-1] // heads
    outs = []
    for h in range(heads):
        t = hs[:, h * dv:(h + 1) * dv]
        mu = jnp.mean(t, axis=-1, keepdims=True)
        var = jnp.mean(jnp.square(t - mu), axis=-1, keepdims=True)
        outs.append((t - mu) * lax.rsqrt(var + HEAD_NORM_EPS))
    hn = jnp.concatenate(outs, axis=-1) * nw
    return jax.nn.sigmoid(og) * hn


def _mm_ln_kernel(*refs, nk, mlstm_prologue, emit_h):
    refs = list(refs)
    a_ref = refs.pop(0)
    if mlstm_prologue:
        og_ref = refs.pop(0)
        nw_ref = refs.pop(0)
    w_ref, b_ref, res_ref, gate_ref, lng_ref, lnb_ref = refs[:6]
    refs = refs[6:]
    if emit_h:
        scn_ref, shn_ref = refs[:2]
        refs = refs[2:]
    xo_ref = refs.pop(0)
    if emit_h:
        ho_ref = refs.pop(0)
    acc_ref = refs.pop(0) if nk > 1 else None

    if mlstm_prologue:
        a = _head_norm_gate(a_ref[...], og_ref[...], nw_ref[...], MLSTM_HEADS).astype(BF16)
    else:
        a = a_ref[...]
    part = jnp.dot(a, w_ref[...].astype(BF16), preferred_element_type=F32)

    def epilogue(y):
        z = DEEPNORM_ALPHA * res_ref[...] + (1.0 + gate_ref[0]) * (y + b_ref[...])
        mu = jnp.mean(z, axis=-1, keepdims=True)
        var = jnp.mean(jnp.square(z - mu), axis=-1, keepdims=True)
        xn = (z - mu) * lax.rsqrt(var + LN_EPS) * lng_ref[...] + lnb_ref[...]
        xo_ref[...] = xn
        if emit_h:
            ho_ref[...] = (xn * (1.0 + scn_ref[0]) + shn_ref[0]).astype(ho_ref.dtype)

    if nk == 1:
        epilogue(part)
    else:
        kk = pl.program_id(1)

        @pl.when(kk == 0)
        def _():
            acc_ref[...] = part

        @pl.when(jnp.logical_and(kk > 0, kk < nk - 1))
        def _():
            acc_ref[...] += part

        @pl.when(kk == nk - 1)
        def _():
            epilogue(acc_ref[...] + part)


def _mm_ln_call(a, w, b, res, gate, lng, lnb, nxt, *, rows_per_batch, tm, tk, name,
                a_col_block=0, ogate=None, normw=None):
    m = a.shape[0]
    k, d = w.shape
    nk = k // tk
    tpb = rows_per_batch // tm
    mlstm_prologue = ogate is not None
    emit_h = nxt is not None
    vec_b = pl.BlockSpec((1, 1, d), lambda i, kk: (i // tpb, 0, 0))
    vec = pl.BlockSpec((1, d), lambda i, kk: (0, 0))
    row_tile = pl.BlockSpec((tm, d), lambda i, kk: (i, 0))

    in_specs = [pl.BlockSpec((tm, tk), lambda i, kk: (i, kk))]
    args = [a]
    asz = a.dtype.itemsize
    if mlstm_prologue:
        ocb = ogate[1]
        in_specs += [pl.BlockSpec((tm, tk), lambda i, kk: (i, ocb)), pl.BlockSpec((1, tk), lambda i, kk: (0, 0))]
        args += [ogate[0], normw.reshape(1, -1)]
    in_specs += [pl.BlockSpec((tk, d), lambda i, kk: (kk, 0)), vec, row_tile, vec_b, vec, vec]
    args += [w, b.reshape(1, d), res, gate, lng.reshape(1, d), lnb.reshape(1, d)]
    out_specs = [row_tile]
    out_shape = [jax.ShapeDtypeStruct((m, d), F32)]
    if emit_h:
        in_specs += [vec_b, vec_b]
        args += [nxt[0], nxt[1]]
        out_specs.append(row_tile)
        out_shape.append(jax.ShapeDtypeStruct((m, d), BF16))
    scratch = [pltpu.VMEM((tm, d), F32)] if nk > 1 else []
    vmem = (2 * tm * tk * asz * (2 if mlstm_prologue else 1) + 2 * tk * d * 4 + tk * d * 2
            + 2 * tm * d * 4 + 2 * tm * d * 4 + 2 * tm * d * 2 + 3 * tm * d * 4)
    outs = pl.pallas_call(
        functools.partial(_mm_ln_kernel, nk=nk, mlstm_prologue=mlstm_prologue, emit_h=emit_h),
        grid=(m // tm, nk),
        in_specs=in_specs,
        out_specs=out_specs,
        out_shape=out_shape,
        scratch_shapes=scratch,
        compiler_params=_params(("arbitrary", "arbitrary"), vmem),
        name=name,
    )(*args)
    return (outs[0], outs[1]) if emit_h else (outs[0], None)


def _rope_kernel(pos_ref, freq_ref, cos_ref, sin_ref):
    ang = pos_ref[0].astype(F32) * freq_ref[...]
    lane = lax.broadcasted_iota(jnp.int32, ang.shape, 1)
    first_half = (lane % ATTN_HEAD_DIM) < ATTN_HEAD_DIM // 2
    cos_ref[0] = jnp.cos(ang)
    s = jnp.sin(ang)
    sin_ref[0] = jnp.where(first_half, -s, s)


def _rope_call(positions):
    bsz, s = positions.shape
    half = ATTN_HEAD_DIM // 2
    inv_freq = 1.0 / (ROPE_THETA ** (np.arange(0, ATTN_HEAD_DIM, 2, dtype=np.float32) / ATTN_HEAD_DIM))
    freq = jnp.asarray(np.tile(inv_freq.astype(np.float32), V7X_LANES // half).reshape(1, V7X_LANES))
    tbl = jax.ShapeDtypeStruct((bsz, s, V7X_LANES), F32)
    return pl.pallas_call(
        _rope_kernel,
        grid=(bsz,),
        in_specs=[pl.BlockSpec((1, s, 1), lambda b: (b, 0, 0)), pl.BlockSpec((1, V7X_LANES), lambda b: (0, 0))],
        out_specs=[pl.BlockSpec((1, s, V7X_LANES), lambda b: (b, 0, 0))] * 2,
        out_shape=[tbl, tbl],
        compiler_params=_params(("arbitrary",), 6 * s * V7X_LANES * 4),
        name="rope_tables",
    )(positions.reshape(bsz, s, 1), freq)


def _rope(t, cos, sin_signed, first_half):
    half = ATTN_HEAD_DIM // 2
    rot = jnp.where(first_half, pltpu.roll(t, V7X_LANES - half, 1), pltpu.roll(t, half, 1))
    return t * cos + rot * sin_signed


def _attn_kernel(sink_ref, q_ref, k_ref, v_ref, cos_ref, sin_ref, o_ref, kp_ref, vp_ref, *, seq):
    blk = ATTN_BLOCK
    hd = ATTN_HEAD_DIM
    hp = pl.program_id(1)
    lane = lax.broadcasted_iota(jnp.int32, (seq, V7X_LANES), 1)
    low = lane < hd
    first_half = (lane % hd) < hd // 2

    kr = _rope(k_ref[0], cos_ref[0], sin_ref[0], first_half)
    v = v_ref[0]
    zeros_pad = jnp.zeros((blk, V7X_LANES), BF16)
    for src, dst in ((kr, kp_ref), (v, vp_ref)):
        lo = jnp.where(low, src, 0.0)
        hi = jnp.where(low, 0.0, src)
        variants = (lo, pltpu.roll(lo, hd, 1), pltpu.roll(hi, hd, 1), hi)
        for idx, val in enumerate(variants):
            dst[idx, 0:blk, :] = zeros_pad
            dst[idx, blk:blk + seq, :] = val.astype(BF16)
            dst[idx, blk + seq:, :] = zeros_pad

    qi = lax.broadcasted_iota(jnp.int32, (blk, 3 * blk), 0)
    ci = lax.broadcasted_iota(jnp.int32, (blk, 3 * blk), 1)
    in_window = jnp.abs(ci - blk - qi) <= WINDOW
    lane_q = lax.broadcasted_iota(jnp.int32, (blk, V7X_LANES), 1)
    first_half_q = (lane_q % hd) < hd // 2
    scale = hd ** -0.5

    def body(i, carry):
        r0 = pl.multiple_of(i * blk, blk)
        kpos = r0 - blk + ci
        valid = in_window & (kpos >= 0) & (kpos < seq)
        cosq = cos_ref[0, pl.ds(r0, blk), :]
        sinq = sin_ref[0, pl.ds(r0, blk), :]
        for t in range(4):
            qt = q_ref[0, pl.ds(r0, blk), t * V7X_LANES:(t + 1) * V7X_LANES]
            qb = (_rope(qt, cosq, sinq, first_half_q) * scale).astype(BF16)
            acc = None
            for g in range(2):
                slot = 2 * (t // 2) + g
                kk = kp_ref[slot, pl.ds(r0, 3 * blk), :]
                s = lax.dot_general(qb, kk, (((1,), (1,)), ((), ())), preferred_element_type=F32)
                s = jnp.where(valid, s, -jnp.inf)
                snk = sink_ref[hp * 8 + 2 * t + g]
                m = jnp.maximum(jnp.max(s, axis=-1, keepdims=True), snk)
                p = jnp.exp(s - m)
                den = jnp.sum(p, axis=-1, keepdims=True) + jnp.exp(snk - m)
                vv = vp_ref[slot, pl.ds(r0, 3 * blk), :]
                o = jnp.dot(p.astype(BF16), vv, preferred_element_type=F32) * (1.0 / den)
                acc = o if acc is None else acc + o
            o_ref[0, pl.ds(r0, blk), t * V7X_LANES:(t + 1) * V7X_LANES] = acc.astype(o_ref.dtype)
        return carry

    lax.fori_loop(0, seq // blk, body, 0)


def _attn_call(proj, cos, sin, sink, bsz, seq):
    n_pairs = ATTN_KV_HEADS // 2
    qw = 4 * V7X_LANES
    k_off = (proj.shape[-1] - 2 * ATTN_KV_HEADS * ATTN_HEAD_DIM) // V7X_LANES
    v_off = k_off + n_pairs
    tbl = pl.BlockSpec((1, seq, V7X_LANES), lambda b, p: (b, 0, 0))
    vmem = 2 * seq * (qw * 4 + 4 * V7X_LANES * 4 + qw * 2) + 8 * (seq + 2 * ATTN_BLOCK) * V7X_LANES * 2
    return pl.pallas_call(
        functools.partial(_attn_kernel, seq=seq),
        grid=(bsz, n_pairs),
        in_specs=[
            pl.BlockSpec(memory_space=pltpu.SMEM),
            pl.BlockSpec((1, seq, qw), lambda b, p: (b, 0, p)),
            pl.BlockSpec((1, seq, V7X_LANES), lambda b, p: (b, 0, k_off + p)),
            pl.BlockSpec((1, seq, V7X_LANES), lambda b, p: (b, 0, v_off + p)),
            tbl, tbl,
        ],
        out_specs=pl.BlockSpec((1, seq, qw), lambda b, p: (b, 0, p)),
        out_shape=jax.ShapeDtypeStruct((bsz, seq, n_pairs * qw), BF16),
        scratch_shapes=[pltpu.VMEM((4, seq + 2 * ATTN_BLOCK, V7X_LANES), BF16)] * 2,
        compiler_params=_params(("arbitrary", "arbitrary"), vmem),
        name="swa_sink_attention",
    )(sink, proj, proj, proj, cos, sin)


def _gates_kernel(h_ref, w_ref, b_ref, o_ref):
    g = jnp.dot(h_ref[...], w_ref[...].astype(BF16), preferred_element_type=F32) + b_ref[...]
    o_ref[0] = jnp.transpose(g)


def _gates_call(h, w_pad, b_pad, bsz, seq):
    m, k = h.shape
    tm = 1024
    tpb = seq // tm
    return pl.pallas_call(
        _gates_kernel,
        grid=(m // tm,),
        in_specs=[
            pl.BlockSpec((tm, k), lambda i: (i, 0)),
            pl.BlockSpec((k, V7X_LANES), lambda i: (0, 0)),
            pl.BlockSpec((1, V7X_LANES), lambda i: (0, 0)),
        ],
        out_specs=pl.BlockSpec((1, V7X_LANES, tm), lambda i: (i // tpb, 0, i % tpb)),
        out_shape=jax.ShapeDtypeStruct((bsz, V7X_LANES, seq), F32),
        compiler_params=_params(("arbitrary",), 2 * tm * k * 2 + 3 * k * V7X_LANES * 4 + 4 * tm * V7X_LANES * 4),
        name="mlstm_gates",
    )(h, w_pad, b_pad)


def _log_sigmoid(x):
    return jnp.minimum(x, 0.0) - jnp.log1p(jnp.exp(-jnp.abs(x)))


def _mlstm_kernel(q_ref, k_ref, v_ref, g_ref, o_ref, qb_ref, kt_ref, vx_ref, c_ref, *, seq, chunk, dk, dv):
    nh = MLSTM_HEADS
    head = pl.program_id(1)
    nc = seq // chunk
    qb_ref[...] = (q_ref[0] * (dk ** -0.5)).astype(BF16)
    kt_ref[...] = jnp.transpose(k_ref[0])
    vx_ref[:, :dv] = v_ref[0].astype(BF16)
    ones_lane = lax.broadcasted_iota(jnp.int32, (seq, V7X_LANES), 1) == 0
    vx_ref[:, dv:] = jnp.where(ones_lane, 1.0, 0.0).astype(BF16)

    ti = lax.broadcasted_iota(jnp.int32, (chunk, chunk), 0)
    ui = lax.broadcasted_iota(jnp.int32, (chunk, chunk), 1)
    eye = ti == ui
    for direction in range(2):
        seen = (ui <= ti) if direction == 0 else (ui >= ti)
        c_ref[...] = jnp.zeros_like(c_ref)
        m_prev = jnp.full((1, 1), -1e30, F32)
        order = range(nc) if direction == 0 else range(nc - 1, -1, -1)
        for c in order:
            rows = slice(c * chunk, (c + 1) * chunk)
            gate_row = 2 * direction * nh + head
            li = g_ref[0, pl.ds(gate_row, 1), rows]
            lf = _log_sigmoid(g_ref[0, pl.ds(gate_row + nh, 1), rows])
            g_col = jnp.sum(jnp.where(seen, lf, 0.0), axis=1, keepdims=True)
            g_row = jnp.sum(jnp.where(eye, g_col, 0.0), axis=0, keepdims=True)
            g_tot = jnp.sum(lf, axis=1, keepdims=True)
            dm = jnp.where(seen, g_col - g_row + li, -jnp.inf)
            a = g_col + m_prev
            m_t = jnp.maximum(a, jnp.max(dm, axis=1, keepdims=True))
            p = jnp.exp(dm - m_t)
            ea = jnp.exp(a - m_t)

            qc = qb_ref[rows, :]
            ktc = kt_ref[:, rows]
            vxc = vx_ref[rows, :]
            sqk = jnp.dot(qc, ktc.astype(BF16), preferred_element_type=F32)
            sc = (sqk * p).astype(BF16)
            tot = (ea * jnp.dot(qc, c_ref[...].astype(BF16), preferred_element_type=F32)
                   + jnp.dot(sc, vxc, preferred_element_type=F32))
            den = tot[:, dv:dv + 1]
            hh = tot[:, :dv] * (1.0 / jnp.maximum(jnp.abs(den), jnp.exp(-m_t)))
            if direction == 0:
                o_ref[0, rows, :] = hh
            else:
                o_ref[0, rows, :] += hh

            w_log = g_tot - g_row + li
            m_new = jnp.maximum(g_tot + m_prev, jnp.max(w_log, axis=1, keepdims=True))
            decay = jnp.exp(g_tot + m_prev - m_new)
            w = jnp.exp(w_log - m_new)
            c_ref[...] = decay * c_ref[...] + jnp.dot((ktc * w).astype(BF16), vxc, preferred_element_type=F32)
            m_prev = m_new


def _mlstm_call(proj, gates_t, bsz, seq, dk, dv):
    nh = MLSTM_HEADS
    chunk = MLSTM_KERNEL_CHUNK
    k_blk = nh
    v_blk = (2 * nh * dk) // dv
    dvx = dv + V7X_LANES
    vmem = (2 * seq * (2 * dk + 2 * dv) * 4 + 2 * MLSTM_GATES * seq * 4
            + seq * dk * 2 + dk * seq * 4 + seq * dvx * 2 + dk * dvx * 4 + 8 * chunk * dvx * 4)
    return pl.pallas_call(
        functools.partial(_mlstm_kernel, seq=seq, chunk=chunk, dk=dk, dv=dv),
        grid=(bsz, nh),
        in_specs=[
            pl.BlockSpec((1, seq, dk), lambda b, h: (b, 0, h)),
            pl.BlockSpec((1, seq, dk), lambda b, h: (b, 0, k_blk + h)),
            pl.BlockSpec((1, seq, dv), lambda b, h: (b, 0, v_blk + h)),
            pl.BlockSpec((1, MLSTM_GATES, seq), lambda b, h: (b, 0, 0)),
        ],
        out_specs=pl.BlockSpec((1, seq, dv), lambda b, h: (b, 0, h)),
        out_shape=jax.ShapeDtypeStruct((bsz, seq, nh * dv), F32),
        scratch_shapes=[
            pltpu.VMEM((seq, dk), BF16),
            pltpu.VMEM((dk, seq), F32),
            pltpu.VMEM((seq, dvx), BF16),
            pltpu.VMEM((dk, dvx), F32),
        ],
        compiler_params=_params(("arbitrary", "arbitrary"), vmem),
        name="bidir_mlstm",
    )(proj, proj, proj, gates_t)


def kernel(x, c, positions, attn_w_qkv, attn_b_qkv, attn_sink, attn_w_o, attn_b_o, mlstm_w_in, mlstm_b_in,
           mlstm_norm_w, mlstm_w_o, mlstm_b_o, mod_w, mod_b, mlp_w1, mlp_b1, mlp_w2, mlp_b2,
           ln_mix_g, ln_mix_b, ln_mlp_g, ln_mlp_b):
    bsz, seq, d = x.shape
    depth = mod_w.shape[0]
    assert depth == DEPTH
    m = bsz * seq
    d_ff = mlp_w1.shape[-1]
    ml_main = mlstm_w_in.shape[-1] - MLSTM_GATES
    ml_dv = mlstm_w_o.shape[1] // MLSTM_HEADS
    ml_dk = (ml_main - 2 * MLSTM_HEADS * ml_dv) // (2 * MLSTM_HEADS)

    c_pad = jnp.pad(c, ((0, 8 - bsz), (0, 0)))
    mod = _mod_call(c_pad, mod_w, mod_b)[:, :bsz]
    mod = mod.reshape(depth, bsz, 6, 1, d)
    sh_m, sc_m, g_m, sh_f, sc_f, g_f = (mod[:, :, j] for j in range(6))

    cos, sin = _rope_call(positions)
    x2 = x.reshape(m, d)
    h = _modulate_call(x2, sc_m[0], sh_m[0], seq)

    for i in range(depth):
        j = i // 2
        if i % 2 == 0:
            proj = _mm_call(h, attn_w_qkv[j], attn_b_qkv[j], n_cols=attn_w_qkv.shape[-1], out_dtype=F32,
                            name="attn_qkv_proj")
            a = _attn_call(proj.reshape(bsz, seq, -1), cos, sin, attn_sink[j], bsz, seq).reshape(m, d)
            x2, h = _mm_ln_call(a, attn_w_o[j], attn_b_o[j], x2, g_m[i], ln_mix_g[i], ln_mix_b[i],
                                (sc_f[i], sh_f[i]), rows_per_batch=seq, tm=512, tk=d, name="attn_out_ln")
        else:
            proj = _mm_call(h, mlstm_w_in[j], mlstm_b_in[j], n_cols=ml_main, out_dtype=F32, name="mlstm_in_proj")
            w_g = jnp.pad(mlstm_w_in[j][:, ml_main:], ((0, 0), (0, V7X_LANES - MLSTM_GATES)))
            b_g = jnp.pad(mlstm_b_in[j][ml_main:], (0, V7X_LANES - MLSTM_GATES)).reshape(1, V7X_LANES)
            gates_t = _gates_call(h, w_g, b_g, bsz, seq)
            hs = _mlstm_call(proj.reshape(bsz, seq, ml_main), gates_t, bsz, seq, ml_dk, ml_dv).reshape(m, d)
            x2, h = _mm_ln_call(hs, mlstm_w_o[j], mlstm_b_o[j], x2, g_m[i], ln_mix_g[i], ln_mix_b[i],
                                (sc_f[i], sh_f[i]), rows_per_batch=seq, tm=512, tk=d, name="mlstm_out_ln",
                                ogate=(proj, (ml_main - d) // d), normw=mlstm_norm_w[j])
        u = _mm_call(h, mlp_w1[i], mlp_b1[i], n_cols=d_ff, out_dtype=BF16, act="relu2", name="mlp_up")
        nxt = (sc_m[i + 1], sh_m[i + 1]) if i + 1 < depth else None
        x2, h = _mm_ln_call(u, mlp_w2[i], mlp_b2[i], x2, g_f[i], ln_mlp_g[i], ln_mlp_b[i], nxt,
                            rows_per_batch=seq, tm=512, tk=512, name="mlp_down_ln")
    return x2.reshape(bsz, seq, d)
```

```python
import functools

import numpy as np
import jax
import jax.numpy as jnp
from jax import lax
from jax.experimental import pallas as pl
from jax.experimental.pallas import tpu as pltpu

F32 = jnp.float32
BF16 = jnp.bfloat16

DEPTH = 2
ATTN_HEAD_DIM = 64
ATTN_KV_HEADS = 8
WINDOW = 128
ATTN_BLOCK = 128
ROPE_THETA = 10000.0
MLSTM_HEADS = 4
MLSTM_GATES = 4 * MLSTM_HEADS
DEEPNORM_ALPHA = (2.0 * DEPTH) ** 0.25
LN_EPS = 1e-5
HEAD_NORM_EPS = 1e-6
MLSTM_KERNEL_CHUNK = 256

V7X_LANES = 128
V7X_VMEM_BYTES = 64 * 1024 * 1024
VMEM_LIMIT_CAP = 60000 * 1024


def _params(semantics, vmem_bytes):
    limit = min(int(vmem_bytes * 1.25) + (4 << 20), VMEM_LIMIT_CAP)
    return pltpu.CompilerParams(dimension_semantics=semantics, vmem_limit_bytes=limit)


def _mod_kernel(c_ref, w_ref, b_ref, o_ref):
    c = c_ref[...]
    c_act = (c * jax.nn.sigmoid(c)).astype(BF16)
    o_ref[0] = jnp.dot(c_act, w_ref[0].astype(BF16), preferred_element_type=F32) + b_ref[0]


def _mod_call(c_pad, mod_w, mod_b):
    depth, d, n = mod_w.shape
    rows = c_pad.shape[0]
    tn = 1024
    return pl.pallas_call(
        _mod_kernel,
        grid=(depth, n // tn),
        in_specs=[
            pl.BlockSpec((rows, d), lambda i, j: (0, 0)),
            pl.BlockSpec((1, d, tn), lambda i, j: (i, 0, j)),
            pl.BlockSpec((1, 1, tn), lambda i, j: (i, 0, j)),
        ],
        out_specs=pl.BlockSpec((1, rows, tn), lambda i, j: (i, 0, j)),
        out_shape=jax.ShapeDtypeStruct((depth, rows, n), F32),
        compiler_params=_params(("arbitrary", "arbitrary"), 2 * d * tn * 4 + d * tn * 2),
        name="adaln_mod",
    )(c_pad, mod_w, mod_b.reshape(depth, 1, n))


def _modulate_kernel(x_ref, sc_ref, sh_ref, o_ref):
    o_ref[...] = (x_ref[...] * (1.0 + sc_ref[0]) + sh_ref[0]).astype(o_ref.dtype)


def _modulate_call(x2, sc, sh, rows_per_batch):
    m, d = x2.shape
    tm = 512
    tpb = rows_per_batch // tm
    vec = pl.BlockSpec((1, 1, d), lambda i: (i // tpb, 0, 0))
    return pl.pallas_call(
        _modulate_kernel,
        grid=(m // tm,),
        in_specs=[pl.BlockSpec((tm, d), lambda i: (i, 0)), vec, vec],
        out_specs=pl.BlockSpec((tm, d), lambda i: (i, 0)),
        out_shape=jax.ShapeDtypeStruct((m, d), BF16),
        compiler_params=_params(("arbitrary",), 2 * tm * d * 6),
        name="modulate_in",
    )(x2, sc, sh)


def _mm_kernel(x_ref, w_ref, b_ref, o_ref, wb_ref, *, act):
    @pl.when(pl.program_id(1) == 0)
    def _():
        wb_ref[...] = w_ref[...].astype(BF16)

    acc = jnp.dot(x_ref[...], wb_ref[...], preferred_element_type=F32) + b_ref[...]
    if act == "relu2":
        acc = jnp.square(jnp.maximum(acc, 0.0))
    o_ref[...] = acc.astype(o_ref.dtype)


def _mm_call(x, w, layer, b, *, n_cols, out_dtype, act=None, tm=1024, tn=1024, name):
    m, k = x.shape
    osz = jnp.dtype(out_dtype).itemsize
    vmem = 2 * k * tn * 4 + k * tn * 2 + 2 * tm * k * 2 + 2 * tm * tn * osz + tm * tn * 4
    return pl.pallas_call(
        functools.partial(_mm_kernel, act=act),
        grid=(n_cols // tn, m // tm),
        in_specs=[
            pl.BlockSpec((tm, k), lambda j, i: (i, 0)),
            pl.BlockSpec((None, k, tn), lambda j, i: (layer, 0, j)),
            pl.BlockSpec((1, tn), lambda j, i: (0, j)),
        ],
        out_specs=pl.BlockSpec((tm, tn), lambda j, i: (i, j)),
        out_shape=jax.ShapeDtypeStruct((m, n_cols), out_dtype),
        scratch_shapes=[pltpu.VMEM((k, tn), BF16)],
        compiler_params=_params(("arbitrary", "arbitrary"), vmem),
        name=name,
    )(x, w, b.reshape(1, -1))


def _head_norm_gate(hs, og, nw, heads):
    dv = hs.shape[-1] // heads
    outs = []
    for h in range(heads):
        t = hs[:, h * dv:(h + 1) * dv]
        mu = jnp.mean(t, axis=-1, keepdims=True)
        var = jnp.mean(jnp.square(t - mu), axis=-1, keepdims=True)
        outs.append((t - mu) * lax.rsqrt(var + HEAD_NORM_EPS))
    hn = jnp.concatenate(outs, axis=-1) * nw
    return jax.nn.sigmoid(og) * hn


def _cast_kernel(w_ref, o_ref):
    o_ref[...] = w_ref[...].astype(o_ref.dtype)


def _cast_call(w, name):
    nl, k, n = w.shape
    tk = 512
    spec = pl.BlockSpec((None, tk, n), lambda l, i: (l, i, 0))
    return pl.pallas_call(
        _cast_kernel,
        grid=(nl, k // tk),
        in_specs=[spec],
        out_specs=spec,
        out_shape=jax.ShapeDtypeStruct(w.shape, BF16),
        compiler_params=_params(("arbitrary", "arbitrary"), 2 * tk * n * 6),
        name=name,
    )(w)


def _mm_ln_kernel(*refs, mlstm_prologue, emit_h):
    refs = list(refs)
    a_ref = refs.pop(0)
    if mlstm_prologue:
        og_ref = refs.pop(0)
        nw_ref = refs.pop(0)
    w_ref, b_ref, res_ref, gate_ref, lng_ref, lnb_ref = refs[:6]
    refs = refs[6:]
    if emit_h:
        scn_ref, shn_ref = refs[:2]
        refs = refs[2:]
    xo_ref = refs.pop(0)
    if emit_h:
        ho_ref = refs.pop(0)

    if mlstm_prologue:
        a = _head_norm_gate(a_ref[...], og_ref[...], nw_ref[...], MLSTM_HEADS).astype(BF16)
    else:
        a = a_ref[...]
    y = jnp.dot(a, w_ref[...], preferred_element_type=F32) + b_ref[...]
    z = DEEPNORM_ALPHA * res_ref[...] + (1.0 + gate_ref[0]) * y
    mu = jnp.mean(z, axis=-1, keepdims=True)
    var = jnp.mean(jnp.square(z - mu), axis=-1, keepdims=True)
    xn = (z - mu) * lax.rsqrt(var + LN_EPS) * lng_ref[...] + lnb_ref[...]
    xo_ref[...] = xn
    if emit_h:
        ho_ref[...] = (xn * (1.0 + scn_ref[0]) + shn_ref[0]).astype(ho_ref.dtype)


def _mm_ln_call(a, w_bf, layer, b, res, gate, lng, lnb, nxt, *, rows_per_batch, tm, name, ogate=None, normw=None):
    m = a.shape[0]
    _, k, d = w_bf.shape
    tpb = rows_per_batch // tm
    mlstm_prologue = ogate is not None
    emit_h = nxt is not None
    vec_b = pl.BlockSpec((1, 1, d), lambda i: (i // tpb, 0, 0))
    vec = pl.BlockSpec((1, d), lambda i: (0, 0))
    row_tile = pl.BlockSpec((tm, d), lambda i: (i, 0))

    in_specs = [pl.BlockSpec((tm, k), lambda i: (i, 0))]
    args = [a]
    asz = a.dtype.itemsize
    if mlstm_prologue:
        ocb = ogate[1]
        in_specs += [pl.BlockSpec((tm, k), lambda i: (i, ocb)), pl.BlockSpec((1, k), lambda i: (0, 0))]
        args += [ogate[0], normw.reshape(1, -1)]
    in_specs += [pl.BlockSpec((None, k, d), lambda i: (layer, 0, 0), pipeline_mode=pl.Buffered(1)),
                 vec, row_tile, vec_b, vec, vec]
    args += [w_bf, b.reshape(1, d), res, gate, lng.reshape(1, d), lnb.reshape(1, d)]
    out_specs = [row_tile]
    out_shape = [jax.ShapeDtypeStruct((m, d), F32)]
    if emit_h:
        in_specs += [vec_b, vec_b]
        args += [nxt[0], nxt[1]]
        out_specs.append(row_tile)
        out_shape.append(jax.ShapeDtypeStruct((m, d), BF16))
    vmem = (k * d * 2 + 2 * tm * k * asz * (2 if mlstm_prologue else 1)
            + 2 * tm * d * (4 + 4 + 2) + 2 * tm * d * 4)
    outs = pl.pallas_call(
        functools.partial(_mm_ln_kernel, mlstm_prologue=mlstm_prologue, emit_h=emit_h),
        grid=(m // tm,),
        in_specs=in_specs,
        out_specs=out_specs,
        out_shape=out_shape,
        compiler_params=_params(("arbitrary",), vmem),
        name=name,
    )(*args)
    return (outs[0], outs[1]) if emit_h else (outs[0], None)


def _rope_kernel(pos_ref, freq_ref, cos_ref, sin_ref):
    ang = pos_ref[0].astype(F32) * freq_ref[...]
    lane = lax.broadcasted_iota(jnp.int32, ang.shape, 1)
    first_half = (lane % ATTN_HEAD_DIM) < ATTN_HEAD_DIM // 2
    cos_ref[0] = jnp.cos(ang)
    s = jnp.sin(ang)
    sin_ref[0] = jnp.where(first_half, -s, s)


def _rope_call(positions):
    bsz, s = positions.shape
    half = ATTN_HEAD_DIM // 2
    inv_freq = 1.0 / (ROPE_THETA ** (np.arange(0, ATTN_HEAD_DIM, 2, dtype=np.float32) / ATTN_HEAD_DIM))
    freq = jnp.asarray(np.tile(inv_freq.astype(np.float32), V7X_LANES // half).reshape(1, V7X_LANES))
    tbl = jax.ShapeDtypeStruct((bsz, s, V7X_LANES), F32)
    return pl.pallas_call(
        _rope_kernel,
        grid=(bsz,),
        in_specs=[pl.BlockSpec((1, s, 1), lambda b: (b, 0, 0)), pl.BlockSpec((1, V7X_LANES), lambda b: (0, 0))],
        out_specs=[pl.BlockSpec((1, s, V7X_LANES), lambda b: (b, 0, 0))] * 2,
        out_shape=[tbl, tbl],
        compiler_params=_params(("arbitrary",), 6 * s * V7X_LANES * 4),
        name="rope_tables",
    )(positions.reshape(bsz, s, 1), freq)


def _rope(t, cos, sin_signed, first_half):
    half = ATTN_HEAD_DIM // 2
    rot = jnp.where(first_half, pltpu.roll(t, V7X_LANES - half, 1), pltpu.roll(t, half, 1))
    return t * cos + rot * sin_signed


def _attn_kernel(sink_ref, q_ref, k_ref, v_ref, cos_ref, sin_ref, o_ref, kp_ref, vp_ref, *, seq):
    blk = ATTN_BLOCK
    hd = ATTN_HEAD_DIM
    hp = pl.program_id(1)
    lane = lax.broadcasted_iota(jnp.int32, (seq, V7X_LANES), 1)
    low = lane < hd
    first_half = (lane % hd) < hd // 2

    kr = _rope(k_ref[0], cos_ref[0], sin_ref[0], first_half)
    v = v_ref[0]
    zeros_pad = jnp.zeros((blk, V7X_LANES), BF16)
    for src, dst in ((kr, kp_ref), (v, vp_ref)):
        lo = jnp.where(low, src, 0.0)
        hi = jnp.where(low, 0.0, src)
        variants = (lo, pltpu.roll(lo, hd, 1), pltpu.roll(hi, hd, 1), hi)
        for idx, val in enumerate(variants):
            dst[idx, 0:blk, :] = zeros_pad
            dst[idx, blk:blk + seq, :] = val.astype(BF16)
            dst[idx, blk + seq:, :] = zeros_pad

    qi = lax.broadcasted_iota(jnp.int32, (blk, 3 * blk), 0)
    ci = lax.broadcasted_iota(jnp.int32, (blk, 3 * blk), 1)
    in_window = jnp.abs(ci - blk - qi) <= WINDOW
    lane_q = lax.broadcasted_iota(jnp.int32, (blk, V7X_LANES), 1)
    first_half_q = (lane_q % hd) < hd // 2
    scale = hd ** -0.5

    def body(i, carry):
        r0 = pl.multiple_of(i * blk, blk)
        kpos = r0 - blk + ci
        valid = in_window & (kpos >= 0) & (kpos < seq)
        cosq = cos_ref[0, pl.ds(r0, blk), :]
        sinq = sin_ref[0, pl.ds(r0, blk), :]
        for t in range(4):
            qt = q_ref[0, pl.ds(r0, blk), t * V7X_LANES:(t + 1) * V7X_LANES]
            qb = (_rope(qt, cosq, sinq, first_half_q) * scale).astype(BF16)
            acc = None
            for g in range(2):
                slot = 2 * (t // 2) + g
                kk = kp_ref[slot, pl.ds(r0, 3 * blk), :]
                s = lax.dot_general(qb, kk, (((1,), (1,)), ((), ())), preferred_element_type=F32)
                s = jnp.where(valid, s, -jnp.inf)
                snk = sink_ref[hp * 8 + 2 * t + g]
                m = jnp.maximum(jnp.max(s, axis=-1, keepdims=True), snk)
                p = jnp.exp(s - m)
                den = jnp.sum(p, axis=-1, keepdims=True) + jnp.exp(snk - m)
                vv = vp_ref[slot, pl.ds(r0, 3 * blk), :]
                o = jnp.dot(p.astype(BF16), vv, preferred_element_type=F32) * (1.0 / den)
                acc = o if acc is None else acc + o
            o_ref[0, pl.ds(r0, blk), t * V7X_LANES:(t + 1) * V7X_LANES] = acc.astype(o_ref.dtype)
        return carry

    lax.fori_loop(0, seq // blk, body, 0)


def _attn_call(proj, cos, sin, sink, bsz, seq):
    n_pairs = ATTN_KV_HEADS // 2
    qw = 4 * V7X_LANES
    k_off = (proj.shape[-1] - 2 * ATTN_KV_HEADS * ATTN_HEAD_DIM) // V7X_LANES
    v_off = k_off + n_pairs
    tbl = pl.BlockSpec((1, seq, V7X_LANES), lambda b, p: (b, 0, 0))
    vmem = 2 * seq * (qw * 4 + 4 * V7X_LANES * 4 + qw * 2) + 8 * (seq + 2 * ATTN_BLOCK) * V7X_LANES * 2
    return pl.pallas_call(
        functools.partial(_attn_kernel, seq=seq),
        grid=(bsz, n_pairs),
        in_specs=[
            pl.BlockSpec(memory_space=pltpu.SMEM),
            pl.BlockSpec((1, seq, qw), lambda b, p: (b, 0, p)),
            pl.BlockSpec((1, seq, V7X_LANES), lambda b, p: (b, 0, k_off + p)),
            pl.BlockSpec((1, seq, V7X_LANES), lambda b, p: (b, 0, v_off + p)),
            tbl, tbl,
        ],
        out_specs=pl.BlockSpec((1, seq, qw), lambda b, p: (b, 0, p)),
        out_shape=jax.ShapeDtypeStruct((bsz, seq, n_pairs * qw), BF16),
        scratch_shapes=[pltpu.VMEM((4, seq + 2 * ATTN_BLOCK, V7X_LANES), BF16)] * 2,
        compiler_params=_params(("arbitrary", "arbitrary"), vmem),
        name="swa_sink_attention",
    )(sink, proj, proj, proj, cos, sin)


def _gates_kernel(h_ref, w_ref, b_ref, o_ref):
    g = jnp.dot(h_ref[...], w_ref[...].astype(BF16), preferred_element_type=F32) + b_ref[...]
    o_ref[0] = jnp.transpose(g)


def _gates_call(h, w_pad, b_pad, bsz, seq):
    m, k = h.shape
    tm = 1024
    tpb = seq // tm
    return pl.pallas_call(
        _gates_kernel,
        grid=(m // tm,),
        in_specs=[
            pl.BlockSpec((tm, k), lambda i: (i, 0)),
            pl.BlockSpec((k, V7X_LANES), lambda i: (0, 0)),
            pl.BlockSpec((1, V7X_LANES), lambda i: (0, 0)),
        ],
        out_specs=pl.BlockSpec((1, V7X_LANES, tm), lambda i: (i // tpb, 0, i % tpb)),
        out_shape=jax.ShapeDtypeStruct((bsz, V7X_LANES, seq), F32),
        compiler_params=_params(("arbitrary",), 2 * tm * k * 2 + 3 * k * V7X_LANES * 4 + 4 * tm * V7X_LANES * 4),
        name="mlstm_gates",
    )(h, w_pad, b_pad)


def _log_sigmoid(x):
    return jnp.minimum(x, 0.0) - jnp.log1p(jnp.exp(-jnp.abs(x)))


def _mlstm_kernel(q_ref, k_ref, v_ref, g_ref, o_ref, qb_ref, kt_ref, vx_ref, c_ref, *, seq, chunk, dk, dv):
    nh = MLSTM_HEADS
    head = pl.program_id(1)
    nc = seq // chunk
    qb_ref[...] = (q_ref[0] * (dk ** -0.5)).astype(BF16)
    kt_ref[...] = jnp.transpose(k_ref[0])
    vx_ref[:, :dv] = v_ref[0].astype(BF16)
    ones_lane = lax.broadcasted_iota(jnp.int32, (seq, V7X_LANES), 1) == 0
    vx_ref[:, dv:] = jnp.where(ones_lane, 1.0, 0.0).astype(BF16)

    ti = lax.broadcasted_iota(jnp.int32, (chunk, chunk), 0)
    ui = lax.broadcasted_iota(jnp.int32, (chunk, chunk), 1)
    eye = ti == ui
    for direction in range(2):
        seen = (ui <= ti) if direction == 0 else (ui >= ti)
        c_ref[...] = jnp.zeros_like(c_ref)
        m_prev = jnp.full((1, 1), -1e30, F32)
        order = range(nc) if direction == 0 else range(nc - 1, -1, -1)
        for c in order:
            rows = slice(c * chunk, (c + 1) * chunk)
            gate_row = 2 * direction * nh + head
            li = g_ref[0, pl.ds(gate_row, 1), rows]
            lf = _log_sigmoid(g_ref[0, pl.ds(gate_row + nh, 1), rows])
            g_col = jnp.sum(jnp.where(seen, lf, 0.0), axis=1, keepdims=True)
            g_row = jnp.sum(jnp.where(eye, g_col, 0.0), axis=0, keepdims=True)
            g_tot = jnp.sum(lf, axis=1, keepdims=True)
            dm = jnp.where(seen, g_col - g_row + li, -jnp.inf)
            a = g_col + m_prev
            m_t = jnp.maximum(a, jnp.max(dm, axis=1, keepdims=True))
            p = jnp.exp(dm - m_t)
            ea = jnp.exp(a - m_t)

            qc = qb_ref[rows, :]
            ktc = kt_ref[:, rows]
            vxc = vx_ref[rows, :]
            sqk = jnp.dot(qc, ktc.astype(BF16), preferred_element_type=F32)
            sc = (sqk * p).astype(BF16)
            tot = (ea * jnp.dot(qc, c_ref[...].astype(BF16), preferred_element_type=F32)
                   + jnp.dot(sc, vxc, preferred_element_type=F32))
            den = tot[:, dv:dv + 1]
            hh = tot[:, :dv] * (1.0 / jnp.maximum(jnp.abs(den), jnp.exp(-m_t)))
            if direction == 0:
                o_ref[0, rows, :] = hh
            else:
                o_ref[0, rows, :] += hh

            w_log = g_tot - g_row + li
            m_new = jnp.maximum(g_tot + m_prev, jnp.max(w_log, axis=1, keepdims=True))
            decay = jnp.exp(g_tot + m_prev - m_new)
            w = jnp.exp(w_log - m_new)
            c_ref[...] = decay * c_ref[...] + jnp.dot((ktc * w).astype(BF16), vxc, preferred_element_type=F32)
            m_prev = m_new


def _mlstm_call(proj, gates_t, bsz, seq, dk, dv):
    nh = MLSTM_HEADS
    chunk = MLSTM_KERNEL_CHUNK
    k_blk = nh
    v_blk = (2 * nh * dk) // dv
    dvx = dv + V7X_LANES
    vmem = (2 * seq * (2 * dk + 2 * dv) * 4 + 2 * MLSTM_GATES * seq * 4
            + seq * dk * 2 + dk * seq * 4 + seq * dvx * 2 + dk * dvx * 4 + 8 * chunk * dvx * 4)
    return pl.pallas_call(
        functools.partial(_mlstm_kernel, seq=seq, chunk=chunk, dk=dk, dv=dv),
        grid=(bsz, nh),
        in_specs=[
            pl.BlockSpec((1, seq, dk), lambda b, h: (b, 0, h)),
            pl.BlockSpec((1, seq, dk), lambda b, h: (b, 0, k_blk + h)),
            pl.BlockSpec((1, seq, dv), lambda b, h: (b, 0, v_blk + h)),
            pl.BlockSpec((1, MLSTM_GATES, seq), lambda b, h: (b, 0, 0)),
        ],
        out_specs=pl.BlockSpec((1, seq, dv), lambda b, h: (b, 0, h)),
        out_shape=jax.ShapeDtypeStruct((bsz, seq, nh * dv), F32),
        scratch_shapes=[
            pltpu.VMEM((seq, dk), BF16),
            pltpu.VMEM((dk, seq), F32),
            pltpu.VMEM((seq, dvx), BF16),
            pltpu.VMEM((dk, dvx), F32),
        ],
        compiler_params=_params(("arbitrary", "arbitrary"), vmem),
        name="bidir_mlstm",
    )(proj, proj, proj, gates_t)


def kernel(x, c, positions, attn_w_qkv, attn_b_qkv, attn_sink, attn_w_o, attn_b_o, mlstm_w_in, mlstm_b_in,
           mlstm_norm_w, mlstm_w_o, mlstm_b_o, mod_w, mod_b, mlp_w1, mlp_b1, mlp_w2, mlp_b2,
           ln_mix_g, ln_mix_b, ln_mlp_g, ln_mlp_b):
    bsz, seq, d = x.shape
    depth = mod_w.shape[0]
    assert depth == DEPTH
    m = bsz * seq
    d_ff = mlp_w1.shape[-1]
    ml_main = mlstm_w_in.shape[-1] - MLSTM_GATES
    ml_dv = mlstm_w_o.shape[1] // MLSTM_HEADS
    ml_dk = (ml_main - 2 * MLSTM_HEADS * ml_dv) // (2 * MLSTM_HEADS)

    c_pad = jnp.pad(c, ((0, 8 - bsz), (0, 0)))
    mod = _mod_call(c_pad, mod_w, mod_b)[:, :bsz]
    mod = mod.reshape(depth, bsz, 6, 1, d)
    sh_m, sc_m, g_m, sh_f, sc_f, g_f = (mod[:, :, j] for j in range(6))

    cos, sin = _rope_call(positions)
    x2 = x.reshape(m, d)
    h = _modulate_call(x2, sc_m[0], sh_m[0], seq)

    attn_w_o_bf = _cast_call(attn_w_o, "cast_attn_w_o")
    mlstm_w_o_bf = _cast_call(mlstm_w_o, "cast_mlstm_w_o")
    mlp_w2_bf = _cast_call(mlp_w2, "cast_mlp_w2")

    for i in range(depth):
        j = i // 2
        if i % 2 == 0:
            proj = _mm_call(h, attn_w_qkv, j, attn_b_qkv[j], n_cols=attn_w_qkv.shape[-1], out_dtype=F32,
                            name="attn_qkv_proj")
            a = _attn_call(proj.reshape(bsz, seq, -1), cos, sin, attn_sink[j], bsz, seq).reshape(m, d)
            x2, h = _mm_ln_call(a, attn_w_o_bf, j, attn_b_o[j], x2, g_m[i], ln_mix_g[i], ln_mix_b[i],
                                (sc_f[i], sh_f[i]), rows_per_batch=seq, tm=512, name="attn_out_ln")
        else:
            proj = _mm_call(h, mlstm_w_in, j, mlstm_b_in[j], n_cols=ml_main, out_dtype=F32, name="mlstm_in_proj")
            w_g = jnp.pad(mlstm_w_in[j][:, ml_main:], ((0, 0), (0, V7X_LANES - MLSTM_GATES)))
            b_g = jnp.pad(mlstm_b_in[j][ml_main:], (0, V7X_LANES - MLSTM_GATES)).reshape(1, V7X_LANES)
            gates_t = _gates_call(h, w_g, b_g, bsz, seq)
            hs = _mlstm_call(proj.reshape(bsz, seq, ml_main), gates_t, bsz, seq, ml_dk, ml_dv).reshape(m, d)
            x2, h = _mm_ln_call(hs, mlstm_w_o_bf, j, mlstm_b_o[j], x2, g_m[i], ln_mix_g[i], ln_mix_b[i],
                                (sc_f[i], sh_f[i]), rows_per_batch=seq, tm=512, name="mlstm_out_ln",
                                ogate=(proj, (ml_main - d) // d), normw=mlstm_norm_w[j])
        u = _mm_call(h, mlp_w1, i, mlp_b1[i], n_cols=d_ff, out_dtype=BF16, act="relu2", name="mlp_up")
        nxt = (sc_m[i + 1], sh_m[i + 1]) if i + 1 < depth else None
        x2, h = _mm_ln_call(u, mlp_w2_bf, i, mlp_b2[i], x2, g_f[i], ln_mlp_g[i], ln_mlp_b[i], nxt,
                            rows_per_batch=seq, tm=256, name="mlp_down_ln")
    return x2.reshape(bsz, seq, d)
```

```python
import functools

import numpy as np
import jax
import jax.numpy as jnp
from jax import lax
from jax.experimental import pallas as pl
from jax.experimental.pallas import tpu as pltpu

F32 = jnp.float32
BF16 = jnp.bfloat16

DEPTH = 2
ATTN_HEAD_DIM = 64
ATTN_KV_HEADS = 8
WINDOW = 128
ATTN_BLOCK = 128
ROPE_THETA = 10000.0
MLSTM_HEADS = 4
MLSTM_GATES = 4 * MLSTM_HEADS
DEEPNORM_ALPHA = (2.0 * DEPTH) ** 0.25
LN_EPS = 1e-5
HEAD_NORM_EPS = 1e-6
MLSTM_KERNEL_CHUNK = 256

V7X_LANES = 128
V7X_VMEM_BYTES = 64 * 1024 * 1024
VMEM_LIMIT_CAP = 60000 * 1024


def _params(semantics, vmem_bytes):
    limit = min(int(vmem_bytes * 1.25) + (4 << 20), VMEM_LIMIT_CAP)
    return pltpu.CompilerParams(dimension_semantics=semantics, vmem_limit_bytes=limit)


def _mod_kernel(c_ref, w_ref, b_ref, o_ref):
    c = c_ref[...]
    c_act = (c * jax.nn.sigmoid(c)).astype(BF16)
    o_ref[0] = jnp.dot(c_act, w_ref[0].astype(BF16), preferred_element_type=F32) + b_ref[0]


def _mod_call(c_pad, mod_w, mod_b):
    depth, d, n = mod_w.shape
    rows = c_pad.shape[0]
    tn = 1024
    return pl.pallas_call(
        _mod_kernel,
        grid=(depth, n // tn),
        in_specs=[
            pl.BlockSpec((rows, d), lambda i, j: (0, 0)),
            pl.BlockSpec((1, d, tn), lambda i, j: (i, 0, j)),
            pl.BlockSpec((1, 1, tn), lambda i, j: (i, 0, j)),
        ],
        out_specs=pl.BlockSpec((1, rows, tn), lambda i, j: (i, 0, j)),
        out_shape=jax.ShapeDtypeStruct((depth, rows, n), F32),
        compiler_params=_params(("arbitrary", "arbitrary"), 2 * d * tn * 4 + d * tn * 2),
        name="adaln_mod",
    )(c_pad, mod_w, mod_b.reshape(depth, 1, n))


def _modulate_kernel(x_ref, sc_ref, sh_ref, o_ref):
    o_ref[...] = (x_ref[...] * (1.0 + sc_ref[0]) + sh_ref[0]).astype(o_ref.dtype)


def _modulate_call(x2, sc, sh, rows_per_batch):
    m, d = x2.shape
    tm = 512
    tpb = rows_per_batch // tm
    vec = pl.BlockSpec((1, 1, d), lambda i: (i // tpb, 0, 0))
    return pl.pallas_call(
        _modulate_kernel,
        grid=(m // tm,),
        in_specs=[pl.BlockSpec((tm, d), lambda i: (i, 0)), vec, vec],
        out_specs=pl.BlockSpec((tm, d), lambda i: (i, 0)),
        out_shape=jax.ShapeDtypeStruct((m, d), BF16),
        compiler_params=_params(("arbitrary",), 2 * tm * d * 6),
        name="modulate_in",
    )(x2, sc, sh)


_CONTRACT_LAST = (((1,), (1,)), ((), ()))


def _mm_kernel(x_ref, w_ref, b_ref, o_ref, wb_ref, *, act, w_transposed):
    @pl.when(pl.program_id(1) == 0)
    def _():
        wb_ref[...] = w_ref[...].astype(BF16)

    if w_transposed:
        acc = lax.dot_general(x_ref[...], wb_ref[...], _CONTRACT_LAST, preferred_element_type=F32)
    else:
        acc = jnp.dot(x_ref[...], wb_ref[...], preferred_element_type=F32)
    acc = acc + b_ref[...]
    if act == "relu2":
        acc = jnp.square(jnp.maximum(acc, 0.0))
    o_ref[...] = acc.astype(o_ref.dtype)


def _mm_call(x, w, layer, b, *, n_cols, out_dtype, act=None, w_transposed=False, tm=1024, tn=1024, name):
    m, k = x.shape
    osz = jnp.dtype(out_dtype).itemsize
    vmem = 2 * k * tn * 4 + k * tn * 2 + 2 * tm * k * 2 + 2 * tm * tn * osz + tm * tn * 4
    if w_transposed:
        w_spec = pl.BlockSpec((None, tn, k), lambda j, i: (layer, j, 0))
        w_tile = (tn, k)
    else:
        w_spec = pl.BlockSpec((None, k, tn), lambda j, i: (layer, 0, j))
        w_tile = (k, tn)
    return pl.pallas_call(
        functools.partial(_mm_kernel, act=act, w_transposed=w_transposed),
        grid=(n_cols // tn, m // tm),
        in_specs=[
            pl.BlockSpec((tm, k), lambda j, i: (i, 0)),
            w_spec,
            pl.BlockSpec((1, tn), lambda j, i: (0, j)),
        ],
        out_specs=pl.BlockSpec((tm, tn), lambda j, i: (i, j)),
        out_shape=jax.ShapeDtypeStruct((m, n_cols), out_dtype),
        scratch_shapes=[pltpu.VMEM(w_tile, BF16)],
        compiler_params=_params(("arbitrary", "arbitrary"), vmem),
        name=name,
    )(x, w, b.reshape(1, -1))


def _head_norm_gate(hs, og, nw, heads):
    dv = hs.shape[-1] // heads
    outs = []
    for h in range(heads):
        t = hs[:, h * dv:(h + 1) * dv]
        mu = jnp.mean(t, axis=-1, keepdims=True)
        var = jnp.mean(jnp.square(t - mu), axis=-1, keepdims=True)
        outs.append((t - mu) * lax.rsqrt(var + HEAD_NORM_EPS))
    hn = jnp.concatenate(outs, axis=-1) * nw
    return jax.nn.sigmoid(og) * hn


def _cast_kernel(w_ref, o_ref):
    o_ref[...] = w_ref[...].astype(o_ref.dtype)


def _cast_call(w, name):
    nl, k, n = w.shape
    tk = 512
    spec = pl.BlockSpec((None, tk, n), lambda l, i: (l, i, 0))
    return pl.pallas_call(
        _cast_kernel,
        grid=(nl, k // tk),
        in_specs=[spec],
        out_specs=spec,
        out_shape=jax.ShapeDtypeStruct(w.shape, BF16),
        compiler_params=_params(("arbitrary", "arbitrary"), 2 * tk * n * 6),
        name=name,
    )(w)


def _mm_ln_kernel(*refs, mlstm_prologue, emit_h):
    refs = list(refs)
    a_ref = refs.pop(0)
    if mlstm_prologue:
        og_ref = refs.pop(0)
        nw_ref = refs.pop(0)
    w_ref, b_ref, res_ref, gate_ref, lng_ref, lnb_ref = refs[:6]
    refs = refs[6:]
    if emit_h:
        scn_ref, shn_ref = refs[:2]
        refs = refs[2:]
    xo_ref = refs.pop(0)
    if emit_h:
        ho_ref = refs.pop(0)

    if mlstm_prologue:
        a = _head_norm_gate(a_ref[...], og_ref[...], nw_ref[...], MLSTM_HEADS).astype(BF16)
    else:
        a = a_ref[...]
    y = jnp.dot(a, w_ref[...], preferred_element_type=F32) + b_ref[...]
    z = DEEPNORM_ALPHA * res_ref[...] + (1.0 + gate_ref[0]) * y
    mu = jnp.mean(z, axis=-1, keepdims=True)
    var = jnp.mean(jnp.square(z - mu), axis=-1, keepdims=True)
    xn = (z - mu) * lax.rsqrt(var + LN_EPS) * lng_ref[...] + lnb_ref[...]
    xo_ref[...] = xn
    if emit_h:
        ho_ref[...] = (xn * (1.0 + scn_ref[0]) + shn_ref[0]).astype(ho_ref.dtype)


def _mm_ln_call(a, w_bf, layer, b, res, gate, lng, lnb, nxt, *, rows_per_batch, tm, name, ogate=None, normw=None):
    m = a.shape[0]
    _, k, d = w_bf.shape
    tpb = rows_per_batch // tm
    mlstm_prologue = ogate is not None
    emit_h = nxt is not None
    vec_b = pl.BlockSpec((1, 1, d), lambda i: (i // tpb, 0, 0))
    vec = pl.BlockSpec((1, d), lambda i: (0, 0))
    row_tile = pl.BlockSpec((tm, d), lambda i: (i, 0))

    in_specs = [pl.BlockSpec((tm, k), lambda i: (i, 0))]
    args = [a]
    asz = a.dtype.itemsize
    if mlstm_prologue:
        ocb = ogate[1]
        in_specs += [pl.BlockSpec((tm, k), lambda i: (i, ocb)), pl.BlockSpec((1, k), lambda i: (0, 0))]
        args += [ogate[0], normw.reshape(1, -1)]
    in_specs += [pl.BlockSpec((None, k, d), lambda i: (layer, 0, 0), pipeline_mode=pl.Buffered(1)),
                 vec, row_tile, vec_b, vec, vec]
    args += [w_bf, b.reshape(1, d), res, gate, lng.reshape(1, d), lnb.reshape(1, d)]
    out_specs = [row_tile]
    out_shape = [jax.ShapeDtypeStruct((m, d), F32)]
    if emit_h:
        in_specs += [vec_b, vec_b]
        args += [nxt[0], nxt[1]]
        out_specs.append(row_tile)
        out_shape.append(jax.ShapeDtypeStruct((m, d), BF16))
    vmem = (k * d * 2 + 2 * tm * k * asz * (2 if mlstm_prologue else 1)
            + 2 * tm * d * (4 + 4 + 2) + 2 * tm * d * 4)
    outs = pl.pallas_call(
        functools.partial(_mm_ln_kernel, mlstm_prologue=mlstm_prologue, emit_h=emit_h),
        grid=(m // tm,),
        in_specs=in_specs,
        out_specs=out_specs,
        out_shape=out_shape,
        compiler_params=_params(("arbitrary",), vmem),
        name=name,
    )(*args)
    return (outs[0], outs[1]) if emit_h else (outs[0], None)


def _rope_kernel(pos_ref, freq_ref, cos_ref, sin_ref):
    ang = pos_ref[0].astype(F32) * freq_ref[...]
    lane = lax.broadcasted_iota(jnp.int32, ang.shape, 1)
    first_half = (lane % ATTN_HEAD_DIM) < ATTN_HEAD_DIM // 2
    cos_ref[0] = jnp.cos(ang)
    s = jnp.sin(ang)
    sin_ref[0] = jnp.where(first_half, -s, s)


def _rope_call(positions):
    bsz, s = positions.shape
    half = ATTN_HEAD_DIM // 2
    inv_freq = 1.0 / (ROPE_THETA ** (np.arange(0, ATTN_HEAD_DIM, 2, dtype=np.float32) / ATTN_HEAD_DIM))
    freq = jnp.asarray(np.tile(inv_freq.astype(np.float32), V7X_LANES // half).reshape(1, V7X_LANES))
    tbl = jax.ShapeDtypeStruct((bsz, s, V7X_LANES), F32)
    return pl.pallas_call(
        _rope_kernel,
        grid=(bsz,),
        in_specs=[pl.BlockSpec((1, s, 1), lambda b: (b, 0, 0)), pl.BlockSpec((1, V7X_LANES), lambda b: (0, 0))],
        out_specs=[pl.BlockSpec((1, s, V7X_LANES), lambda b: (b, 0, 0))] * 2,
        out_shape=[tbl, tbl],
        compiler_params=_params(("arbitrary",), 6 * s * V7X_LANES * 4),
        name="rope_tables",
    )(positions.reshape(bsz, s, 1), freq)


def _rope(t, cos, sin_signed, first_half):
    half = ATTN_HEAD_DIM // 2
    rot = jnp.where(first_half, pltpu.roll(t, V7X_LANES - half, 1), pltpu.roll(t, half, 1))
    return t * cos + rot * sin_signed


ATTN_VT_ROWS = ATTN_HEAD_DIM + 16


def _attn_kernel(sink_ref, q_ref, k_ref, v_ref, cos_ref, sin_ref, o_ref, qs_ref, kp_ref, vt_ref, bias_ref, *, seq):
    blk = ATTN_BLOCK
    hd = ATTN_HEAD_DIM
    nb = seq // blk
    hp = pl.program_id(1)
    lane = lax.broadcasted_iota(jnp.int32, (seq, V7X_LANES), 1)
    low = lane < hd
    first_half = (lane % hd) < hd // 2
    cos = cos_ref[0]
    sin = sin_ref[0]

    scale = hd ** -0.5
    for t in range(4):
        qt = q_ref[0, :, t * V7X_LANES:(t + 1) * V7X_LANES]
        qs_ref[t] = (_rope(qt, cos, sin, first_half) * scale).astype(BF16)

    kr = _rope(k_ref[0], cos, sin, first_half)
    lo = jnp.where(low, kr, 0.0)
    hi = jnp.where(low, 0.0, kr)
    zeros_pad = jnp.zeros((blk, V7X_LANES), BF16)
    for idx, val in enumerate((lo, pltpu.roll(lo, hd, 1), pltpu.roll(hi, hd, 1), hi)):
        kp_ref[idx, 0:blk, :] = zeros_pad
        kp_ref[idx, blk:blk + seq, :] = val.astype(BF16)
        kp_ref[idx, blk + seq:, :] = zeros_pad

    vt = jnp.transpose(v_ref[0])
    ones_row = lax.broadcasted_iota(jnp.int32, (ATTN_VT_ROWS - hd, seq), 0) == 0
    zeros_vt = jnp.zeros((ATTN_VT_ROWS, blk), BF16)
    for j in range(2):
        vt_ref[j, :, 0:blk] = zeros_vt
        vt_ref[j, 0:hd, blk:blk + seq] = vt[j * hd:(j + 1) * hd].astype(BF16)
        vt_ref[j, hd:, blk:blk + seq] = jnp.where(ones_row, 1.0, 0.0).astype(BF16)
        vt_ref[j, :, blk + seq:] = zeros_vt

    ci = lax.broadcasted_iota(jnp.int32, (3 * blk, 2 * blk), 0)
    ri = lax.broadcasted_iota(jnp.int32, (3 * blk, 2 * blk), 1) % blk
    in_window = jnp.abs(ci - blk - ri) <= WINDOW
    bias_ref[0] = jnp.where(in_window & (ci >= blk), 0.0, -jnp.inf)
    bias_ref[1] = jnp.where(in_window, 0.0, -jnp.inf)
    bias_ref[2] = jnp.where(in_window & (ci < 2 * blk), 0.0, -jnp.inf)
    lane2 = lax.broadcasted_iota(jnp.int32, (1, 2 * blk), 1)

    def body(i, carry):
        r0 = pl.multiple_of(i * blk, blk)
        bias = bias_ref[jnp.where(i == 0, 0, jnp.where(i == nb - 1, 2, 1))]
        chains = [(j, half) for j in range(2) for half in range(2)]
        scores = []
        for j, half in chains:
            qcat = jnp.concatenate([qs_ref[2 * j, pl.ds(r0, blk), :], qs_ref[2 * j + 1, pl.ds(r0, blk), :]], axis=0)
            kk = kp_ref[2 * j + half, pl.ds(r0, 3 * blk), :]
            scores.append(lax.dot_general(kk, qcat, (((1,), (1,)), ((), ())), preferred_element_type=F32) + bias)
        probs = []
        for (j, half), st in zip(chains, scores):
            head = hp * 8 + 4 * j + half
            snk = jnp.where(lane2 < blk, sink_ref[head], sink_ref[head + 2])
            m = jnp.maximum(jnp.max(st, axis=0, keepdims=True), snk)
            probs.append((jnp.exp(st - m).astype(BF16), jnp.exp(snk - m)))
        outs = []
        for (j, half), (p, sink_p) in zip(chains, probs):
            vtw = vt_ref[j, :, pl.ds(r0, 3 * blk)]
            ot = jnp.dot(vtw, p, preferred_element_type=F32)
            outs.append(ot[0:hd] * (1.0 / (ot[hd:hd + 1] + sink_p)))
        for t in range(4):
            j, tt = divmod(t, 2)
            cols = slice(tt * blk, (tt + 1) * blk)
            tile_t = jnp.concatenate([outs[2 * j][:, cols], outs[2 * j + 1][:, cols]], axis=0)
            o_ref[0, pl.ds(r0, blk), t * V7X_LANES:(t + 1) * V7X_LANES] = jnp.transpose(tile_t).astype(o_ref.dtype)
        return carry

    lax.fori_loop(0, nb, body, 0, unroll=2)


def _attn_call(proj, cos, sin, sink, bsz, seq):
    n_pairs = ATTN_KV_HEADS // 2
    qw = 4 * V7X_LANES
    k_off = (proj.shape[-1] - 2 * ATTN_KV_HEADS * ATTN_HEAD_DIM) // V7X_LANES
    v_off = k_off + n_pairs
    assert seq // ATTN_BLOCK >= 2 and ATTN_BLOCK == V7X_LANES and 2 * ATTN_HEAD_DIM == V7X_LANES
    pad_seq = seq + 2 * ATTN_BLOCK
    tbl = pl.BlockSpec((1, seq, V7X_LANES), lambda b, p: (b, 0, 0))
    vmem = (2 * seq * (qw * 4 + 4 * V7X_LANES * 4 + qw * 2) + seq * qw * 2 + 4 * pad_seq * V7X_LANES * 2
            + 2 * ATTN_VT_ROWS * pad_seq * 2 + 3 * 3 * ATTN_BLOCK * 2 * ATTN_BLOCK * 4 + 6 * seq * V7X_LANES * 4)
    return pl.pallas_call(
        functools.partial(_attn_kernel, seq=seq),
        grid=(bsz, n_pairs),
        in_specs=[
            pl.BlockSpec(memory_space=pltpu.SMEM),
            pl.BlockSpec((1, seq, qw), lambda b, p: (b, 0, p)),
            pl.BlockSpec((1, seq, V7X_LANES), lambda b, p: (b, 0, k_off + p)),
            pl.BlockSpec((1, seq, V7X_LANES), lambda b, p: (b, 0, v_off + p)),
            tbl, tbl,
        ],
        out_specs=pl.BlockSpec((1, seq, qw), lambda b, p: (b, 0, p)),
        out_shape=jax.ShapeDtypeStruct((bsz, seq, n_pairs * qw), BF16),
        scratch_shapes=[
            pltpu.VMEM((4, seq, V7X_LANES), BF16),
            pltpu.VMEM((4, pad_seq, V7X_LANES), BF16),
            pltpu.VMEM((2, ATTN_VT_ROWS, pad_seq), BF16),
            pltpu.VMEM((3, 3 * ATTN_BLOCK, 2 * ATTN_BLOCK), F32),
        ],
        compiler_params=_params(("arbitrary", "arbitrary"), vmem),
        name="swa_sink_attention",
    )(sink, proj, proj, proj, cos, sin)


def _gates_kernel(h_ref, w_ref, b_ref, o_ref):
    g_t = lax.dot_general(w_ref[...].astype(BF16), h_ref[...], _CONTRACT_LAST, preferred_element_type=F32)
    o_ref[0] = g_t + b_ref[...]


def _gates_call(h, w_t, layer, b_col, bsz, seq):
    m, k = h.shape
    tm = 1024
    tpb = seq // tm
    gate_blk = (w_t.shape[1] - MLSTM_GATES) // MLSTM_GATES
    return pl.pallas_call(
        _gates_kernel,
        grid=(m // tm,),
        in_specs=[
            pl.BlockSpec((tm, k), lambda i: (i, 0)),
            pl.BlockSpec((None, MLSTM_GATES, k), lambda i: (layer, gate_blk, 0)),
            pl.BlockSpec((MLSTM_GATES, 1), lambda i: (0, 0)),
        ],
        out_specs=pl.BlockSpec((1, MLSTM_GATES, tm), lambda i: (i // tpb, 0, i % tpb)),
        out_shape=jax.ShapeDtypeStruct((bsz, MLSTM_GATES, seq), F32),
        compiler_params=_params(("arbitrary",), 2 * tm * k * 2 + 4 * MLSTM_GATES * (k + tm) * 4),
        name="mlstm_gates",
    )(h, w_t, b_col)


def _log_sigmoid(x):
    return jnp.minimum(x, 0.0) - jnp.log1p(jnp.exp(-jnp.abs(x)))


def _mlstm_kernel(q_ref, k_ref, v_ref, g_ref, o_ref, qb_ref, kt_ref, vx_ref, c_ref, *, seq, chunk, dk, dv):
    nh = MLSTM_HEADS
    head = pl.program_id(1)
    nc = seq // chunk
    qb_ref[...] = (q_ref[0] * (dk ** -0.5)).astype(BF16)
    kt_ref[...] = jnp.transpose(k_ref[0])
    vx_ref[:, :dv] = v_ref[0].astype(BF16)
    ones_lane = lax.broadcasted_iota(jnp.int32, (seq, V7X_LANES), 1) == 0
    vx_ref[:, dv:] = jnp.where(ones_lane, 1.0, 0.0).astype(BF16)

    ti = lax.broadcasted_iota(jnp.int32, (chunk, chunk), 0)
    ui = lax.broadcasted_iota(jnp.int32, (chunk, chunk), 1)
    eye = ti == ui
    for direction in range(2):
        seen = (ui <= ti) if direction == 0 else (ui >= ti)
        c_ref[...] = jnp.zeros_like(c_ref)
        m_prev = jnp.full((1, 1), -1e30, F32)
        order = range(nc) if direction == 0 else range(nc - 1, -1, -1)
        for c in order:
            rows = slice(c * chunk, (c + 1) * chunk)
            gate_row = 2 * direction * nh + head
            li = g_ref[0, pl.ds(gate_row, 1), rows]
            lf = _log_sigmoid(g_ref[0, pl.ds(gate_row + nh, 1), rows])
            g_col = jnp.sum(jnp.where(seen, lf, 0.0), axis=1, keepdims=True)
            g_row = jnp.sum(jnp.where(eye, g_col, 0.0), axis=0, keepdims=True)
            g_tot = jnp.sum(lf, axis=1, keepdims=True)
            dm = jnp.where(seen, g_col - g_row + li, -jnp.inf)
            a = g_col + m_prev
            m_t = jnp.maximum(a, jnp.max(dm, axis=1, keepdims=True))
            p = jnp.exp(dm - m_t)
            ea = jnp.exp(a - m_t)

            qc = qb_ref[rows, :]
            ktc = kt_ref[:, rows]
            vxc = vx_ref[rows, :]
            sqk = jnp.dot(qc, ktc.astype(BF16), preferred_element_type=F32)
            sc = (sqk * p).astype(BF16)
            tot = (ea * jnp.dot(qc, c_ref[...].astype(BF16), preferred_element_type=F32)
                   + jnp.dot(sc, vxc, preferred_element_type=F32))
            den = tot[:, dv:dv + 1]
            hh = tot[:, :dv] * (1.0 / jnp.maximum(jnp.abs(den), jnp.exp(-m_t)))
            if direction == 0:
                o_ref[0, rows, :] = hh
            else:
                o_ref[0, rows, :] += hh

            w_log = g_tot - g_row + li
            m_new = jnp.maximum(g_tot + m_prev, jnp.max(w_log, axis=1, keepdims=True))
            decay = jnp.exp(g_tot + m_prev - m_new)
            w = jnp.exp(w_log - m_new)
            c_ref[...] = decay * c_ref[...] + jnp.dot((ktc * w).astype(BF16), vxc, preferred_element_type=F32)
            m_prev = m_new


def _mlstm_call(proj, gates_t, bsz, seq, dk, dv):
    nh = MLSTM_HEADS
    chunk = MLSTM_KERNEL_CHUNK
    k_blk = nh
    v_blk = (2 * nh * dk) // dv
    dvx = dv + V7X_LANES
    vmem = (2 * seq * (2 * dk + 2 * dv) * 4 + 2 * MLSTM_GATES * seq * 4
            + seq * dk * 2 + dk * seq * 4 + seq * dvx * 2 + dk * dvx * 4 + 8 * chunk * dvx * 4)
    return pl.pallas_call(
        functools.partial(_mlstm_kernel, seq=seq, chunk=chunk, dk=dk, dv=dv),
        grid=(bsz, nh),
        in_specs=[
            pl.BlockSpec((1, seq, dk), lambda b, h: (b, 0, h)),
            pl.BlockSpec((1, seq, dk), lambda b, h: (b, 0, k_blk + h)),
            pl.BlockSpec((1, seq, dv), lambda b, h: (b, 0, v_blk + h)),
            pl.BlockSpec((1, MLSTM_GATES, seq), lambda b, h: (b, 0, 0)),
        ],
        out_specs=pl.BlockSpec((1, seq, dv), lambda b, h: (b, 0, h)),
        out_shape=jax.ShapeDtypeStruct((bsz, seq, nh * dv), F32),
        scratch_shapes=[
            pltpu.VMEM((seq, dk), BF16),
            pltpu.VMEM((dk, seq), F32),
            pltpu.VMEM((seq, dvx), BF16),
            pltpu.VMEM((dk, dvx), F32),
        ],
        compiler_params=_params(("arbitrary", "arbitrary"), vmem),
        name="bidir_mlstm",
    )(proj, proj, proj, gates_t)


def kernel(x, c, positions, attn_w_qkv, attn_b_qkv, attn_sink, attn_w_o, attn_b_o, mlstm_w_in, mlstm_b_in,
           mlstm_norm_w, mlstm_w_o, mlstm_b_o, mod_w, mod_b, mlp_w1, mlp_b1, mlp_w2, mlp_b2,
           ln_mix_g, ln_mix_b, ln_mlp_g, ln_mlp_b):
    bsz, seq, d = x.shape
    depth = mod_w.shape[0]
    assert depth == DEPTH
    m = bsz * seq
    d_ff = mlp_w1.shape[-1]
    ml_main = mlstm_w_in.shape[-1] - MLSTM_GATES
    ml_dv = mlstm_w_o.shape[1] // MLSTM_HEADS
    ml_dk = (ml_main - 2 * MLSTM_HEADS * ml_dv) // (2 * MLSTM_HEADS)

    c_pad = jnp.pad(c, ((0, 8 - bsz), (0, 0)))
    mod = _mod_call(c_pad, mod_w, mod_b)[:, :bsz]
    mod = mod.reshape(depth, bsz, 6, 1, d)
    sh_m, sc_m, g_m, sh_f, sc_f, g_f = (mod[:, :, j] for j in range(6))

    cos, sin = _rope_call(positions)
    x2 = x.reshape(m, d)
    h = _modulate_call(x2, sc_m[0], sh_m[0], seq)

    attn_w_o_bf = _cast_call(attn_w_o, "cast_attn_w_o")
    mlstm_w_o_bf = _cast_call(mlstm_w_o, "cast_mlstm_w_o")
    mlp_w2_bf = _cast_call(mlp_w2, "cast_mlp_w2")
    mlstm_w_in_t = jnp.swapaxes(mlstm_w_in, 1, 2)

    for i in range(depth):
        j = i // 2
        if i % 2 == 0:
            proj = _mm_call(h, attn_w_qkv, j, attn_b_qkv[j], n_cols=attn_w_qkv.shape[-1], out_dtype=F32,
                            name="attn_qkv_proj")
            a = _attn_call(proj.reshape(bsz, seq, -1), cos, sin, attn_sink[j], bsz, seq).reshape(m, d)
            x2, h = _mm_ln_call(a, attn_w_o_bf, j, attn_b_o[j], x2, g_m[i], ln_mix_g[i], ln_mix_b[i],
                                (sc_f[i], sh_f[i]), rows_per_batch=seq, tm=512, name="attn_out_ln")
        else:
            proj = _mm_call(h, mlstm_w_in_t, j, mlstm_b_in[j], n_cols=ml_main, out_dtype=F32, w_transposed=True,
                            name="mlstm_in_proj")
            gates_t = _gates_call(h, mlstm_w_in_t, j, mlstm_b_in[j][ml_main:].reshape(MLSTM_GATES, 1), bsz, seq)
            hs = _mlstm_call(proj.reshape(bsz, seq, ml_main), gates_t, bsz, seq, ml_dk, ml_dv).reshape(m, d)
            x2, h = _mm_ln_call(hs, mlstm_w_o_bf, j, mlstm_b_o[j], x2, g_m[i], ln_mix_g[i], ln_mix_b[i],
                                (sc_f[i], sh_f[i]), rows_per_batch=seq, tm=512, name="mlstm_out_ln",
                                ogate=(proj, (ml_main - d) // d), normw=mlstm_norm_w[j])
        u = _mm_call(h, mlp_w1, i, mlp_b1[i], n_cols=d_ff, out_dtype=BF16, act="relu2", name="mlp_up")
        nxt = (sc_m[i + 1], sh_m[i + 1]) if i + 1 < depth else None
        x2, h = _mm_ln_call(u, mlp_w2_bf, i, mlp_b2[i], x2, g_f[i], ln_mlp_g[i], ln_mlp_b[i], nxt,
                            rows_per_batch=seq, tm=256, name="mlp_down_ln")
    return x2.reshape(bsz, seq, d)
```

```python
import functools

import numpy as np
import jax
import jax.numpy as jnp
from jax import lax
from jax.experimental import pallas as pl
from jax.experimental.pallas import tpu as pltpu

F32 = jnp.float32
BF16 = jnp.bfloat16

DEPTH = 2
ATTN_HEAD_DIM = 64
ATTN_KV_HEADS = 8
WINDOW = 128
ATTN_BLOCK = 128
ROPE_THETA = 10000.0
MLSTM_HEADS = 4
MLSTM_GATES = 4 * MLSTM_HEADS
DEEPNORM_ALPHA = (2.0 * DEPTH) ** 0.25
LN_EPS = 1e-5
HEAD_NORM_EPS = 1e-6
MLSTM_KERNEL_CHUNK = 256
LOG2_E = 1.4426950408889634

V7X_LANES = 128
V7X_VMEM_BYTES = 64 * 1024 * 1024
VMEM_LIMIT_CAP = 60000 * 1024


def _params(semantics, vmem_bytes):
    limit = min(int(vmem_bytes * 1.25) + (4 << 20), VMEM_LIMIT_CAP)
    return pltpu.CompilerParams(dimension_semantics=semantics, vmem_limit_bytes=limit)


def _mod_kernel(c_ref, w_ref, b_ref, o_ref):
    c = c_ref[...]
    c_act = (c * jax.nn.sigmoid(c)).astype(BF16)
    o_ref[0] = jnp.dot(c_act, w_ref[0].astype(BF16), preferred_element_type=F32) + b_ref[0]


def _mod_call(c_pad, mod_w, mod_b):
    depth, d, n = mod_w.shape
    rows = c_pad.shape[0]
    tn = 1024
    return pl.pallas_call(
        _mod_kernel,
        grid=(depth, n // tn),
        in_specs=[
            pl.BlockSpec((rows, d), lambda i, j: (0, 0)),
            pl.BlockSpec((1, d, tn), lambda i, j: (i, 0, j)),
            pl.BlockSpec((1, 1, tn), lambda i, j: (i, 0, j)),
        ],
        out_specs=pl.BlockSpec((1, rows, tn), lambda i, j: (i, 0, j)),
        out_shape=jax.ShapeDtypeStruct((depth, rows, n), F32),
        compiler_params=_params(("arbitrary", "arbitrary"), 2 * d * tn * 4 + d * tn * 2),
        name="adaln_mod",
    )(c_pad, mod_w, mod_b.reshape(depth, 1, n))


def _modulate_kernel(x_ref, sc_ref, sh_ref, o_ref):
    o_ref[...] = (x_ref[...] * (1.0 + sc_ref[0]) + sh_ref[0]).astype(o_ref.dtype)


def _modulate_call(x2, sc, sh, rows_per_batch):
    m, d = x2.shape
    tm = 512
    tpb = rows_per_batch // tm
    vec = pl.BlockSpec((1, 1, d), lambda i: (i // tpb, 0, 0))
    return pl.pallas_call(
        _modulate_kernel,
        grid=(m // tm,),
        in_specs=[pl.BlockSpec((tm, d), lambda i: (i, 0)), vec, vec],
        out_specs=pl.BlockSpec((tm, d), lambda i: (i, 0)),
        out_shape=jax.ShapeDtypeStruct((m, d), BF16),
        compiler_params=_params(("arbitrary",), 2 * tm * d * 6),
        name="modulate_in",
    )(x2, sc, sh)


_CONTRACT_LAST = (((1,), (1,)), ((), ()))


def _mm_kernel(*refs, act, w_transposed, side_cast):
    if side_cast:
        x_ref, w_ref, b_ref, side_ref, o_ref, side_o_ref, wb_ref = refs
        side_o_ref[...] = side_ref[...].astype(side_o_ref.dtype)
    else:
        x_ref, w_ref, b_ref, o_ref, wb_ref = refs

    @pl.when(pl.program_id(1) == 0)
    def _():
        wb_ref[...] = w_ref[...].astype(BF16)

    if w_transposed:
        acc = lax.dot_general(x_ref[...], wb_ref[...], _CONTRACT_LAST, preferred_element_type=F32)
    else:
        acc = jnp.dot(x_ref[...], wb_ref[...], preferred_element_type=F32)
    acc = acc + b_ref[...]
    if act == "relu2":
        acc = jnp.square(jnp.maximum(acc, 0.0))
    o_ref[...] = acc.astype(o_ref.dtype)


def _mm_call(x, w, layer, b, *, n_cols, out_dtype, act=None, w_transposed=False, side=None, tm=1024, tn=1024, name):
    m, k = x.shape
    osz = jnp.dtype(out_dtype).itemsize
    vmem = 2 * k * tn * 4 + k * tn * 2 + 2 * tm * k * 2 + 2 * tm * tn * osz + tm * tn * 4
    if w_transposed:
        w_spec = pl.BlockSpec((None, tn, k), lambda j, i: (layer, j, 0))
        w_tile = (tn, k)
    else:
        w_spec = pl.BlockSpec((None, k, tn), lambda j, i: (layer, 0, j))
        w_tile = (k, tn)
    n_i = m // tm
    grid = (n_cols // tn, n_i)
    in_specs = [pl.BlockSpec((tm, k), lambda j, i: (i, 0)), w_spec, pl.BlockSpec((1, tn), lambda j, i: (0, j))]
    args = [x, w, b.reshape(1, -1)]
    out_specs = [pl.BlockSpec((tm, tn), lambda j, i: (i, j))]
    out_shape = [jax.ShapeDtypeStruct((m, n_cols), out_dtype)]
    if side is not None:
        w2, layer2 = side
        _, k2, n2 = w2.shape
        n_blk = 1 << ((grid[0] * grid[1]).bit_length() - 1)
        rows = k2 // n_blk
        assert rows * n_blk == k2 and rows % 16 == 0
        in_specs.append(pl.BlockSpec((None, rows, n2), lambda j, i: (layer2, jnp.minimum(j * n_i + i, n_blk - 1), 0)))
        args.append(w2)
        out_specs.append(pl.BlockSpec((rows, n2), lambda j, i: (jnp.minimum(j * n_i + i, n_blk - 1), 0)))
        out_shape.append(jax.ShapeDtypeStruct((k2, n2), BF16))
        vmem += 2 * rows * n2 * 6
    outs = pl.pallas_call(
        functools.partial(_mm_kernel, act=act, w_transposed=w_transposed, side_cast=side is not None),
        grid=grid,
        in_specs=in_specs,
        out_specs=out_specs,
        out_shape=out_shape,
        scratch_shapes=[pltpu.VMEM(w_tile, BF16)],
        compiler_params=_params(("arbitrary", "arbitrary"), vmem),
        name=name,
    )(*args)
    return (outs[0], outs[1]) if side is not None else outs[0]


def _head_norm_gate(hs, og, nw, heads):
    dv = hs.shape[-1] // heads
    outs = []
    for h in range(heads):
        t = hs[:, h * dv:(h + 1) * dv]
        mu = jnp.mean(t, axis=-1, keepdims=True)
        var = jnp.mean(jnp.square(t - mu), axis=-1, keepdims=True)
        outs.append((t - mu) * lax.rsqrt(var + HEAD_NORM_EPS))
    hn = jnp.concatenate(outs, axis=-1) * nw
    return jax.nn.sigmoid(og) * hn


def _mm_ln_kernel(*refs, mlstm_prologue, emit_h, n_sub):
    refs = list(refs)
    a_ref = refs.pop(0)
    if mlstm_prologue:
        og_ref = refs.pop(0)
        nw_ref = refs.pop(0)
    w_ref, b_ref, res_ref, gate_ref, lng_ref, lnb_ref = refs[:6]
    refs = refs[6:]
    if emit_h:
        scn_ref, shn_ref = refs[:2]
        refs = refs[2:]
    xo_ref = refs.pop(0)
    if emit_h:
        ho_ref = refs.pop(0)

    tm = a_ref.shape[0]
    sub = tm // n_sub
    for r in range(n_sub):
        rows = slice(r * sub, (r + 1) * sub)
        if mlstm_prologue:
            a = _head_norm_gate(a_ref[rows, :], og_ref[rows, :], nw_ref[...], MLSTM_HEADS).astype(BF16)
        else:
            a = a_ref[rows, :]
        y = jnp.dot(a, w_ref[...], preferred_element_type=F32) + b_ref[...]
        z = DEEPNORM_ALPHA * res_ref[rows, :] + (1.0 + gate_ref[0]) * y
        mu = jnp.mean(z, axis=-1, keepdims=True)
        var = jnp.mean(jnp.square(z - mu), axis=-1, keepdims=True)
        xn = (z - mu) * lax.rsqrt(var + LN_EPS) * lng_ref[...] + lnb_ref[...]
        xo_ref[rows, :] = xn
        if emit_h:
            ho_ref[rows, :] = (xn * (1.0 + scn_ref[0]) + shn_ref[0]).astype(ho_ref.dtype)


def _mm_ln_call(a, w_bf, b, res, gate, lng, lnb, nxt, *, rows_per_batch, tm, n_sub, name,
                ogate=None, normw=None):
    m = a.shape[0]
    k, d = w_bf.shape
    tpb = rows_per_batch // tm
    mlstm_prologue = ogate is not None
    emit_h = nxt is not None
    vec_b = pl.BlockSpec((1, 1, d), lambda i: (i // tpb, 0, 0))
    vec = pl.BlockSpec((1, d), lambda i: (0, 0))
    row_tile = pl.BlockSpec((tm, d), lambda i: (i, 0))

    in_specs = [pl.BlockSpec((tm, k), lambda i: (i, 0))]
    args = [a]
    asz = a.dtype.itemsize
    if mlstm_prologue:
        ocb = ogate[1]
        in_specs += [pl.BlockSpec((tm, k), lambda i: (i, ocb)), pl.BlockSpec((1, k), lambda i: (0, 0))]
        args += [ogate[0], normw.reshape(1, -1)]
    in_specs += [pl.BlockSpec((k, d), lambda i: (0, 0), pipeline_mode=pl.Buffered(1)),
                 vec, row_tile, vec_b, vec, vec]
    args += [w_bf, b.reshape(1, d), res, gate, lng.reshape(1, d), lnb.reshape(1, d)]
    out_specs = [row_tile]
    out_shape = [jax.ShapeDtypeStruct((m, d), F32)]
    if emit_h:
        in_specs += [vec_b, vec_b]
        args += [nxt[0], nxt[1]]
        out_specs.append(row_tile)
        out_shape.append(jax.ShapeDtypeStruct((m, d), BF16))
    vmem = (k * d * 2 + 2 * tm * k * asz * (2 if mlstm_prologue else 1)
            + 2 * tm * d * (4 + 4 + 2) + 2 * tm * d * 4)
    outs = pl.pallas_call(
        functools.partial(_mm_ln_kernel, mlstm_prologue=mlstm_prologue, emit_h=emit_h, n_sub=n_sub),
        grid=(m // tm,),
        in_specs=in_specs,
        out_specs=out_specs,
        out_shape=out_shape,
        compiler_params=_params(("arbitrary",), vmem),
        name=name,
    )(*args)
    return (outs[0], outs[1]) if emit_h else (outs[0], None)


def _rope_kernel(pos_ref, freq_ref, cos_ref, sin_ref):
    ang = pos_ref[0].astype(F32) * freq_ref[...]
    lane = lax.broadcasted_iota(jnp.int32, ang.shape, 1)
    first_half = (lane % ATTN_HEAD_DIM) < ATTN_HEAD_DIM // 2
    cos_ref[0] = jnp.cos(ang)
    s = jnp.sin(ang)
    sin_ref[0] = jnp.where(first_half, -s, s)


def _rope_call(positions):
    bsz, s = positions.shape
    half = ATTN_HEAD_DIM // 2
    inv_freq = 1.0 / (ROPE_THETA ** (np.arange(0, ATTN_HEAD_DIM, 2, dtype=np.float32) / ATTN_HEAD_DIM))
    freq = jnp.asarray(np.tile(inv_freq.astype(np.float32), V7X_LANES // half).reshape(1, V7X_LANES))
    tbl = jax.ShapeDtypeStruct((bsz, s, V7X_LANES), F32)
    return pl.pallas_call(
        _rope_kernel,
        grid=(bsz,),
        in_specs=[pl.BlockSpec((1, s, 1), lambda b: (b, 0, 0)), pl.BlockSpec((1, V7X_LANES), lambda b: (0, 0))],
        out_specs=[pl.BlockSpec((1, s, V7X_LANES), lambda b: (b, 0, 0))] * 2,
        out_shape=[tbl, tbl],
        compiler_params=_params(("arbitrary",), 6 * s * V7X_LANES * 4),
        name="rope_tables",
    )(positions.reshape(bsz, s, 1), freq)


def _rope(t, cos, sin_signed, first_half):
    half = ATTN_HEAD_DIM // 2
    rot = jnp.where(first_half, pltpu.roll(t, V7X_LANES - half, 1), pltpu.roll(t, half, 1))
    return t * cos + rot * sin_signed


ATTN_VT_ROWS = ATTN_HEAD_DIM + 16


def _attn_kernel(sink_ref, q_ref, k_ref, v_ref, cos_ref, sin_ref, o_ref, qs_ref, kp_ref, vt_ref, bias_ref, *, seq):
    blk = ATTN_BLOCK
    hd = ATTN_HEAD_DIM
    nb = seq // blk
    hp = pl.program_id(1)
    lane = lax.broadcasted_iota(jnp.int32, (seq, V7X_LANES), 1)
    low = lane < hd
    first_half = (lane % hd) < hd // 2
    cos = cos_ref[0]
    sin = sin_ref[0]

    scale = hd ** -0.5 * LOG2_E
    for t in range(4):
        qt = q_ref[0, :, t * V7X_LANES:(t + 1) * V7X_LANES]
        qs_ref[t] = (_rope(qt, cos, sin, first_half) * scale).astype(BF16)

    kr = _rope(k_ref[0], cos, sin, first_half)
    lo = jnp.where(low, kr, 0.0)
    hi = jnp.where(low, 0.0, kr)
    zeros_pad = jnp.zeros((blk, V7X_LANES), BF16)
    for idx, val in enumerate((lo, pltpu.roll(lo, hd, 1), pltpu.roll(hi, hd, 1), hi)):
        kp_ref[idx, 0:blk, :] = zeros_pad
        kp_ref[idx, blk:blk + seq, :] = val.astype(BF16)
        kp_ref[idx, blk + seq:, :] = zeros_pad

    vt = jnp.transpose(v_ref[0])
    ones_row = lax.broadcasted_iota(jnp.int32, (ATTN_VT_ROWS - hd, seq), 0) == 0
    zeros_vt = jnp.zeros((ATTN_VT_ROWS, blk), BF16)
    for j in range(2):
        vt_ref[j, :, 0:blk] = zeros_vt
        vt_ref[j, 0:hd, blk:blk + seq] = vt[j * hd:(j + 1) * hd].astype(BF16)
        vt_ref[j, hd:, blk:blk + seq] = jnp.where(ones_row, 1.0, 0.0).astype(BF16)
        vt_ref[j, :, blk + seq:] = zeros_vt

    ci = lax.broadcasted_iota(jnp.int32, (blk, 2 * blk), 0)
    ri = lax.broadcasted_iota(jnp.int32, (blk, 2 * blk), 1) % blk
    neg = jnp.full((blk, 2 * blk), -jnp.inf, F32)
    bias_ref[0] = jnp.where(ci >= ri, 0.0, -jnp.inf)
    bias_ref[1] = neg
    bias_ref[2] = jnp.where(ci <= ri, 0.0, -jnp.inf)
    bias_ref[3] = neg
    lane2 = lax.broadcasted_iota(jnp.int32, (1, 2 * blk), 1)

    def body(i, carry):
        r0 = pl.multiple_of(i * blk, blk)
        bias_prev = bias_ref[jnp.where(i == 0, 1, 0)]
        bias_next = bias_ref[jnp.where(i == nb - 1, 3, 2)]
        chains = [(j, half) for j in range(2) for half in range(2)]
        scores = []
        for j, half in chains:
            qcat = jnp.concatenate([qs_ref[2 * j, pl.ds(r0, blk), :], qs_ref[2 * j + 1, pl.ds(r0, blk), :]], axis=0)
            kk = kp_ref[2 * j + half, pl.ds(r0, 3 * blk), :]
            scores.append(lax.dot_general(kk, qcat, _CONTRACT_LAST, preferred_element_type=F32))
        probs = []
        for (j, half), st in zip(chains, scores):
            head = hp * 8 + 4 * j + half
            snk = jnp.where(lane2 < blk, sink_ref[head], sink_ref[head + 2]) * LOG2_E
            parts = (st[0:blk] + bias_prev, st[blk:2 * blk], st[2 * blk:] + bias_next)
            m = snk
            for part in parts:
                m = jnp.maximum(m, jnp.max(part, axis=0, keepdims=True))
            p = jnp.concatenate([jnp.exp2(part - m) for part in parts], axis=0)
            probs.append((p.astype(BF16), jnp.exp2(snk - m)))
        outs = []
        for (j, half), (p, sink_p) in zip(chains, probs):
            vtw = vt_ref[j, :, pl.ds(r0, 3 * blk)]
            ot = jnp.dot(vtw, p, preferred_element_type=F32)
            outs.append(ot[0:hd] * (1.0 / (ot[hd:hd + 1] + sink_p)))
        for t in range(4):
            j, tt = divmod(t, 2)
            cols = slice(tt * blk, (tt + 1) * blk)
            tile_t = jnp.concatenate([outs[2 * j][:, cols], outs[2 * j + 1][:, cols]], axis=0)
            o_ref[0, pl.ds(r0, blk), t * V7X_LANES:(t + 1) * V7X_LANES] = jnp.transpose(tile_t).astype(o_ref.dtype)
        return carry

    lax.fori_loop(0, nb, body, 0, unroll=2)


def _attn_call(proj, cos, sin, sink, bsz, seq):
    n_pairs = ATTN_KV_HEADS // 2
    qw = 4 * V7X_LANES
    k_off = (proj.shape[-1] - 2 * ATTN_KV_HEADS * ATTN_HEAD_DIM) // V7X_LANES
    v_off = k_off + n_pairs
    assert seq // ATTN_BLOCK >= 2 and ATTN_BLOCK == V7X_LANES and 2 * ATTN_HEAD_DIM == V7X_LANES
    assert WINDOW == ATTN_BLOCK
    pad_seq = seq + 2 * ATTN_BLOCK
    tbl = pl.BlockSpec((1, seq, V7X_LANES), lambda b, p: (b, 0, 0))
    vmem = (2 * seq * (qw * 4 + 4 * V7X_LANES * 4 + qw * 2) + seq * qw * 2 + 4 * pad_seq * V7X_LANES * 2
            + 2 * ATTN_VT_ROWS * pad_seq * 2 + 3 * 3 * ATTN_BLOCK * 2 * ATTN_BLOCK * 4 + 6 * seq * V7X_LANES * 4)
    return pl.pallas_call(
        functools.partial(_attn_kernel, seq=seq),
        grid=(bsz, n_pairs),
        in_specs=[
            pl.BlockSpec(memory_space=pltpu.SMEM),
            pl.BlockSpec((1, seq, qw), lambda b, p: (b, 0, p)),
            pl.BlockSpec((1, seq, V7X_LANES), lambda b, p: (b, 0, k_off + p)),
            pl.BlockSpec((1, seq, V7X_LANES), lambda b, p: (b, 0, v_off + p)),
            tbl, tbl,
        ],
        out_specs=pl.BlockSpec((1, seq, qw), lambda b, p: (b, 0, p)),
        out_shape=jax.ShapeDtypeStruct((bsz, seq, n_pairs * qw), BF16),
        scratch_shapes=[
            pltpu.VMEM((4, seq, V7X_LANES), BF16),
            pltpu.VMEM((4, pad_seq, V7X_LANES), BF16),
            pltpu.VMEM((2, ATTN_VT_ROWS, pad_seq), BF16),
            pltpu.VMEM((4, ATTN_BLOCK, 2 * ATTN_BLOCK), F32),
        ],
        compiler_params=_params(("arbitrary", "arbitrary"), vmem),
        name="swa_sink_attention",
    )(sink, proj, proj, proj, cos, sin)


def _gates_kernel(h_ref, w_ref, b_ref, o_ref):
    g_t = lax.dot_general(w_ref[...].astype(BF16), h_ref[...], _CONTRACT_LAST, preferred_element_type=F32)
    o_ref[0] = g_t + b_ref[...]


def _gates_call(h, w_t, layer, b_col, bsz, seq):
    m, k = h.shape
    tm = 1024
    tpb = seq // tm
    gate_blk = (w_t.shape[1] - MLSTM_GATES) // MLSTM_GATES
    return pl.pallas_call(
        _gates_kernel,
        grid=(m // tm,),
        in_specs=[
            pl.BlockSpec((tm, k), lambda i: (i, 0)),
            pl.BlockSpec((None, MLSTM_GATES, k), lambda i: (layer, gate_blk, 0)),
            pl.BlockSpec((MLSTM_GATES, 1), lambda i: (0, 0)),
        ],
        out_specs=pl.BlockSpec((1, MLSTM_GATES, tm), lambda i: (i // tpb, 0, i % tpb)),
        out_shape=jax.ShapeDtypeStruct((bsz, MLSTM_GATES, seq), F32),
        compiler_params=_params(("arbitrary",), 2 * tm * k * 2 + 4 * MLSTM_GATES * (k + tm) * 4),
        name="mlstm_gates",
    )(h, w_t, b_col)


def _log_sigmoid(x):
    return jnp.minimum(x, 0.0) - jnp.log1p(jnp.exp(-jnp.abs(x)))


def _mlstm_kernel(q_ref, k_ref, v_ref, g_ref, o_ref, qb_ref, kt_ref, vx_ref, c_ref, *, seq, chunk, dk, dv):
    nh = MLSTM_HEADS
    head = pl.program_id(1)
    nc = seq // chunk
    qb_ref[...] = (q_ref[0] * (dk ** -0.5)).astype(BF16)
    kt_ref[...] = jnp.transpose(k_ref[0])
    vx_ref[:, :dv] = v_ref[0].astype(BF16)
    ones_lane = lax.broadcasted_iota(jnp.int32, (seq, V7X_LANES), 1) == 0
    vx_ref[:, dv:] = jnp.where(ones_lane, 1.0, 0.0).astype(BF16)

    ti = lax.broadcasted_iota(jnp.int32, (chunk, chunk), 0)
    ui = lax.broadcasted_iota(jnp.int32, (chunk, chunk), 1)
    eye = ti == ui
    for direction in range(2):
        seen = (ui <= ti) if direction == 0 else (ui >= ti)
        c_ref[...] = jnp.zeros_like(c_ref)
        m_prev = jnp.full((1, 1), -1e30, F32)
        order = range(nc) if direction == 0 else range(nc - 1, -1, -1)
        for c in order:
            rows = slice(c * chunk, (c + 1) * chunk)
            gate_row = 2 * direction * nh + head
            li = g_ref[0, pl.ds(gate_row, 1), rows]
            lf = _log_sigmoid(g_ref[0, pl.ds(gate_row + nh, 1), rows])
            g_col = jnp.sum(jnp.where(seen, lf, 0.0), axis=1, keepdims=True)
            g_row = jnp.sum(jnp.where(eye, g_col, 0.0), axis=0, keepdims=True)
            g_tot = jnp.sum(lf, axis=1, keepdims=True)
            dm = jnp.where(seen, g_col - g_row + li, -jnp.inf)
            a = g_col + m_prev
            m_t = jnp.maximum(a, jnp.max(dm, axis=1, keepdims=True))
            p = jnp.exp(dm - m_t)
            ea = jnp.exp(a - m_t)

            qc = qb_ref[rows, :]
            ktc = kt_ref[:, rows]
            vxc = vx_ref[rows, :]
            sqk = jnp.dot(qc, ktc.astype(BF16), preferred_element_type=F32)
            sc = (sqk * p).astype(BF16)
            tot = (ea * jnp.dot(qc, c_ref[...].astype(BF16), preferred_element_type=F32)
                   + jnp.dot(sc, vxc, preferred_element_type=F32))
            den = tot[:, dv:dv + 1]
            hh = tot[:, :dv] * (1.0 / jnp.maximum(jnp.abs(den), jnp.exp(-m_t)))
            if direction == 0:
                o_ref[0, rows, :] = hh
            else:
                o_ref[0, rows, :] += hh

            w_log = g_tot - g_row + li
            m_new = jnp.maximum(g_tot + m_prev, jnp.max(w_log, axis=1, keepdims=True))
            decay = jnp.exp(g_tot + m_prev - m_new)
            w = jnp.exp(w_log - m_new)
            c_ref[...] = decay * c_ref[...] + jnp.dot((ktc * w).astype(BF16), vxc, preferred_element_type=F32)
            m_prev = m_new


def _mlstm_call(proj, gates_t, bsz, seq, dk, dv):
    nh = MLSTM_HEADS
    chunk = MLSTM_KERNEL_CHUNK
    k_blk = nh
    v_blk = (2 * nh * dk) // dv
    dvx = dv + V7X_LANES
    vmem = (2 * seq * (2 * dk + 2 * dv) * 4 + 2 * MLSTM_GATES * seq * 4
            + seq * dk * 2 + dk * seq * 4 + seq * dvx * 2 + dk * dvx * 4 + 8 * chunk * dvx * 4)
    return pl.pallas_call(
        functools.partial(_mlstm_kernel, seq=seq, chunk=chunk, dk=dk, dv=dv),
        grid=(bsz, nh),
        in_specs=[
            pl.BlockSpec((1, seq, dk), lambda b, h: (b, 0, h)),
            pl.BlockSpec((1, seq, dk), lambda b, h: (b, 0, k_blk + h)),
            pl.BlockSpec((1, seq, dv), lambda b, h: (b, 0, v_blk + h)),
            pl.BlockSpec((1, MLSTM_GATES, seq), lambda b, h: (b, 0, 0)),
        ],
        out_specs=pl.BlockSpec((1, seq, dv), lambda b, h: (b, 0, h)),
        out_shape=jax.ShapeDtypeStruct((bsz, seq, nh * dv), F32),
        scratch_shapes=[
            pltpu.VMEM((seq, dk), BF16),
            pltpu.VMEM((dk, seq), F32),
            pltpu.VMEM((seq, dvx), BF16),
            pltpu.VMEM((dk, dvx), F32),
        ],
        compiler_params=_params(("arbitrary", "arbitrary"), vmem),
        name="bidir_mlstm",
    )(proj, proj, proj, gates_t)


def kernel(x, c, positions, attn_w_qkv, attn_b_qkv, attn_sink, attn_w_o, attn_b_o, mlstm_w_in, mlstm_b_in,
           mlstm_norm_w, mlstm_w_o, mlstm_b_o, mod_w, mod_b, mlp_w1, mlp_b1, mlp_w2, mlp_b2,
           ln_mix_g, ln_mix_b, ln_mlp_g, ln_mlp_b):
    bsz, seq, d = x.shape
    depth = mod_w.shape[0]
    assert depth == DEPTH
    m = bsz * seq
    d_ff = mlp_w1.shape[-1]
    ml_main = mlstm_w_in.shape[-1] - MLSTM_GATES
    ml_dv = mlstm_w_o.shape[1] // MLSTM_HEADS
    ml_dk = (ml_main - 2 * MLSTM_HEADS * ml_dv) // (2 * MLSTM_HEADS)

    c_pad = jnp.pad(c, ((0, 8 - bsz), (0, 0)))
    mod = _mod_call(c_pad, mod_w, mod_b)[:, :bsz]
    mod = mod.reshape(depth, bsz, 6, 1, d)
    sh_m, sc_m, g_m, sh_f, sc_f, g_f = (mod[:, :, j] for j in range(6))

    cos, sin = _rope_call(positions)
    x2 = x.reshape(m, d)
    h = _modulate_call(x2, sc_m[0], sh_m[0], seq)

    mlstm_w_in_t = jnp.swapaxes(mlstm_w_in, 1, 2)

    for i in range(depth):
        j = i // 2
        if i % 2 == 0:
            proj, w_o_bf = _mm_call(h, attn_w_qkv, j, attn_b_qkv[j], n_cols=attn_w_qkv.shape[-1], out_dtype=F32,
                                    side=(attn_w_o, j), name="attn_qkv_proj")
            a = _attn_call(proj.reshape(bsz, seq, -1), cos, sin, attn_sink[j], bsz, seq).reshape(m, d)
            x2, h = _mm_ln_call(a, w_o_bf, attn_b_o[j], x2, g_m[i], ln_mix_g[i], ln_mix_b[i],
                                (sc_f[i], sh_f[i]), rows_per_batch=seq, tm=512, n_sub=2, name="attn_out_ln")
        else:
            proj, w_o_bf = _mm_call(h, mlstm_w_in_t, j, mlstm_b_in[j], n_cols=ml_main, out_dtype=F32,
                                    w_transposed=True, side=(mlstm_w_o, j), name="mlstm_in_proj")
            gates_t = _gates_call(h, mlstm_w_in_t, j, mlstm_b_in[j][ml_main:].reshape(MLSTM_GATES, 1), bsz, seq)
            hs = _mlstm_call(proj.reshape(bsz, seq, ml_main), gates_t, bsz, seq, ml_dk, ml_dv).reshape(m, d)
            x2, h = _mm_ln_call(hs, w_o_bf, mlstm_b_o[j], x2, g_m[i], ln_mix_g[i], ln_mix_b[i],
                                (sc_f[i], sh_f[i]), rows_per_batch=seq, tm=512, n_sub=2, name="mlstm_out_ln",
                                ogate=(proj, (ml_main - d) // d), normw=mlstm_norm_w[j])
        u, w2_bf = _mm_call(h, mlp_w1, i, mlp_b1[i], n_cols=d_ff, out_dtype=BF16, act="relu2", side=(mlp_w2, i),
                            name="mlp_up")
        nxt = (sc_m[i + 1], sh_m[i + 1]) if i + 1 < depth else None
        x2, h = _mm_ln_call(u, w2_bf, mlp_b2[i], x2, g_f[i], ln_mlp_g[i], ln_mlp_b[i], nxt,
                            rows_per_batch=seq, tm=256, n_sub=2, name="mlp_down_ln")
    return x2.reshape(bsz, seq, d)
```

```python
import functools

import numpy as np
import jax
import jax.numpy as jnp
from jax import lax
from jax.experimental import pallas as pl
from jax.experimental.pallas import tpu as pltpu

F32 = jnp.float32
BF16 = jnp.bfloat16

DEPTH = 2
ATTN_HEAD_DIM = 64
ATTN_KV_HEADS = 8
WINDOW = 128
ATTN_BLOCK = 128
ROPE_THETA = 10000.0
MLSTM_HEADS = 4
MLSTM_GATES = 4 * MLSTM_HEADS
DEEPNORM_ALPHA = (2.0 * DEPTH) ** 0.25
LN_EPS = 1e-5
HEAD_NORM_EPS = 1e-6
MLSTM_KERNEL_CHUNK = 256
LOG2_E = 1.4426950408889634

V7X_LANES = 128
V7X_VMEM_BYTES = 64 * 1024 * 1024
VMEM_LIMIT_CAP = 60000 * 1024
VMEM_LIMIT_FLOOR = V7X_VMEM_BYTES - 28 * 1024 * 1024


def _params(semantics, vmem_bytes):
    limit = max(min(int(vmem_bytes * 1.25) + (4 << 20), VMEM_LIMIT_CAP), VMEM_LIMIT_FLOOR)
    return pltpu.CompilerParams(dimension_semantics=semantics, vmem_limit_bytes=limit)


def _mod_kernel(c_ref, w_ref, b_ref, o_ref):
    c = c_ref[...]
    c_act = (c * jax.nn.sigmoid(c)).astype(BF16)
    o_ref[0] = jnp.dot(c_act, w_ref[0].astype(BF16), preferred_element_type=F32) + b_ref[0]


def _mod_call(c_pad, mod_w, mod_b):
    depth, d, n = mod_w.shape
    rows = c_pad.shape[0]
    tn = 1024
    return pl.pallas_call(
        _mod_kernel,
        grid=(depth, n // tn),
        in_specs=[
            pl.BlockSpec((rows, d), lambda i, j: (0, 0)),
            pl.BlockSpec((1, d, tn), lambda i, j: (i, 0, j)),
            pl.BlockSpec((1, 1, tn), lambda i, j: (i, 0, j)),
        ],
        out_specs=pl.BlockSpec((1, rows, tn), lambda i, j: (i, 0, j)),
        out_shape=jax.ShapeDtypeStruct((depth, rows, n), F32),
        compiler_params=_params(("arbitrary", "arbitrary"), 2 * d * tn * 4 + d * tn * 2),
        name="adaln_mod",
    )(c_pad, mod_w, mod_b.reshape(depth, 1, n))


def _modulate_kernel(x_ref, sc_ref, sh_ref, o_ref):
    o_ref[...] = (x_ref[...] * (1.0 + sc_ref[0]) + sh_ref[0]).astype(o_ref.dtype)


def _modulate_call(x2, sc, sh, rows_per_batch):
    m, d = x2.shape
    tm = 512
    tpb = rows_per_batch // tm
    vec = pl.BlockSpec((1, 1, d), lambda i: (i // tpb, 0, 0))
    return pl.pallas_call(
        _modulate_kernel,
        grid=(m // tm,),
        in_specs=[pl.BlockSpec((tm, d), lambda i: (i, 0)), vec, vec],
        out_specs=pl.BlockSpec((tm, d), lambda i: (i, 0)),
        out_shape=jax.ShapeDtypeStruct((m, d), BF16),
        compiler_params=_params(("arbitrary",), 2 * tm * d * 6),
        name="modulate_in",
    )(x2, sc, sh)


_CONTRACT_LAST = (((1,), (1,)), ((), ()))


def _mm_kernel(*refs, act, w_transposed, side_cast):
    if side_cast:
        x_ref, w_ref, b_ref, side_ref, o_ref, side_o_ref, wb_ref = refs
        side_o_ref[...] = side_ref[...].astype(side_o_ref.dtype)
    else:
        x_ref, w_ref, b_ref, o_ref, wb_ref = refs

    @pl.when(pl.program_id(1) == 0)
    def _():
        wb_ref[...] = w_ref[...].astype(BF16)

    if w_transposed:
        acc = lax.dot_general(x_ref[...], wb_ref[...], _CONTRACT_LAST, preferred_element_type=F32)
    else:
        acc = jnp.dot(x_ref[...], wb_ref[...], preferred_element_type=F32)
    acc = acc + b_ref[...]
    if act == "relu2":
        acc = jnp.square(jnp.maximum(acc, 0.0))
    o_ref[...] = acc.astype(o_ref.dtype)


def _mm_call(x, w, layer, b, *, n_cols, out_dtype, act=None, w_transposed=False, side=None, tm=1024, tn=1024, name):
    m, k = x.shape
    osz = jnp.dtype(out_dtype).itemsize
    vmem = 2 * k * tn * 4 + k * tn * 2 + 2 * tm * k * 2 + 2 * tm * tn * osz + tm * tn * 4
    if w_transposed:
        w_spec = pl.BlockSpec((None, tn, k), lambda j, i: (layer, j, 0))
        w_tile = (tn, k)
    else:
        w_spec = pl.BlockSpec((None, k, tn), lambda j, i: (layer, 0, j))
        w_tile = (k, tn)
    n_i = m // tm
    grid = (n_cols // tn, n_i)
    in_specs = [pl.BlockSpec((tm, k), lambda j, i: (i, 0)), w_spec, pl.BlockSpec((1, tn), lambda j, i: (0, j))]
    args = [x, w, b.reshape(1, -1)]
    out_specs = [pl.BlockSpec((tm, tn), lambda j, i: (i, j))]
    out_shape = [jax.ShapeDtypeStruct((m, n_cols), out_dtype)]
    if side is not None:
        w2, layer2 = side
        _, k2, n2 = w2.shape
        n_blk = 1 << ((grid[0] * grid[1]).bit_length() - 1)
        rows = k2 // n_blk
        assert rows * n_blk == k2 and rows % 16 == 0
        in_specs.append(pl.BlockSpec((None, rows, n2), lambda j, i: (layer2, jnp.minimum(j * n_i + i, n_blk - 1), 0)))
        args.append(w2)
        out_specs.append(pl.BlockSpec((rows, n2), lambda j, i: (jnp.minimum(j * n_i + i, n_blk - 1), 0)))
        out_shape.append(jax.ShapeDtypeStruct((k2, n2), BF16))
        vmem += 2 * rows * n2 * 6
    outs = pl.pallas_call(
        functools.partial(_mm_kernel, act=act, w_transposed=w_transposed, side_cast=side is not None),
        grid=grid,
        in_specs=in_specs,
        out_specs=out_specs,
        out_shape=out_shape,
        scratch_shapes=[pltpu.VMEM(w_tile, BF16)],
        compiler_params=_params(("arbitrary", "arbitrary"), vmem),
        name=name,
    )(*args)
    return (outs[0], outs[1]) if side is not None else outs[0]


def _head_norm_gate(hs, og, nw, heads):
    dv = hs.shape[-1] // heads
    outs = []
    for h in range(heads):
        t = hs[:, h * dv:(h + 1) * dv]
        mu = jnp.mean(t, axis=-1, keepdims=True)
        var = jnp.mean(jnp.square(t - mu), axis=-1, keepdims=True)
        outs.append((t - mu) * lax.rsqrt(var + HEAD_NORM_EPS))
    hn = jnp.concatenate(outs, axis=-1) * nw
    return jax.nn.sigmoid(og) * hn


def _mm_ln_kernel(*refs, mlstm_prologue, emit_h):
    refs = list(refs)
    a_ref = refs.pop(0)
    if mlstm_prologue:
        og_ref = refs.pop(0)
        nw_ref = refs.pop(0)
    w_ref, b_ref, res_ref, gate_ref, lng_ref, lnb_ref = refs[:6]
    refs = refs[6:]
    if emit_h:
        scn_ref, shn_ref = refs[:2]
        refs = refs[2:]
    xo_ref = refs.pop(0)
    if emit_h:
        ho_ref = refs.pop(0)

    if mlstm_prologue:
        a = _head_norm_gate(a_ref[...], og_ref[...], nw_ref[...], MLSTM_HEADS).astype(BF16)
    else:
        a = a_ref[...]
    y = jnp.dot(a, w_ref[...], preferred_element_type=F32) + b_ref[...]
    z = DEEPNORM_ALPHA * res_ref[...] + (1.0 + gate_ref[0]) * y
    mu = jnp.mean(z, axis=-1, keepdims=True)
    var = jnp.mean(jnp.square(z - mu), axis=-1, keepdims=True)
    xn = (z - mu) * lax.rsqrt(var + LN_EPS) * lng_ref[...] + lnb_ref[...]
    xo_ref[...] = xn
    if emit_h:
        ho_ref[...] = (xn * (1.0 + scn_ref[0]) + shn_ref[0]).astype(ho_ref.dtype)


def _mm_ln_call(a, w_bf, b, res, gate, lng, lnb, nxt, *, rows_per_batch, tm, name, ogate=None, normw=None):
    m = a.shape[0]
    k, d = w_bf.shape
    tpb = rows_per_batch // tm
    mlstm_prologue = ogate is not None
    emit_h = nxt is not None
    vec_b = pl.BlockSpec((1, 1, d), lambda i: (i // tpb, 0, 0))
    vec = pl.BlockSpec((1, d), lambda i: (0, 0))
    row_tile = pl.BlockSpec((tm, d), lambda i: (i, 0))

    in_specs = [pl.BlockSpec((tm, k), lambda i: (i, 0))]
    args = [a]
    asz = a.dtype.itemsize
    if mlstm_prologue:
        ocb = ogate[1]
        in_specs += [pl.BlockSpec((tm, k), lambda i: (i, ocb)), pl.BlockSpec((1, k), lambda i: (0, 0))]
        args += [ogate[0], normw.reshape(1, -1)]
    in_specs += [pl.BlockSpec((k, d), lambda i: (0, 0), pipeline_mode=pl.Buffered(1)),
                 vec, row_tile, vec_b, vec, vec]
    args += [w_bf, b.reshape(1, d), res, gate, lng.reshape(1, d), lnb.reshape(1, d)]
    out_specs = [row_tile]
    out_shape = [jax.ShapeDtypeStruct((m, d), F32)]
    if emit_h:
        in_specs += [vec_b, vec_b]
        args += [nxt[0], nxt[1]]
        out_specs.append(row_tile)
        out_shape.append(jax.ShapeDtypeStruct((m, d), BF16))
    vmem = (k * d * 2 + 2 * tm * k * asz * (2 if mlstm_prologue else 1)
            + 2 * tm * d * (4 + 4 + 2) + 2 * tm * d * 4)
    outs = pl.pallas_call(
        functools.partial(_mm_ln_kernel, mlstm_prologue=mlstm_prologue, emit_h=emit_h),
        grid=(m // tm,),
        in_specs=in_specs,
        out_specs=out_specs,
        out_shape=out_shape,
        compiler_params=_params(("arbitrary",), vmem),
        name=name,
    )(*args)
    return (outs[0], outs[1]) if emit_h else (outs[0], None)


def _rope_kernel(pos_ref, freq_ref, cos_ref, sin_ref):
    ang = pos_ref[0].astype(F32) * freq_ref[...]
    lane = lax.broadcasted_iota(jnp.int32, ang.shape, 1)
    first_half = (lane % ATTN_HEAD_DIM) < ATTN_HEAD_DIM // 2
    cos_ref[0] = jnp.cos(ang)
    s = jnp.sin(ang)
    sin_ref[0] = jnp.where(first_half, -s, s)


def _rope_call(positions):
    bsz, s = positions.shape
    half = ATTN_HEAD_DIM // 2
    inv_freq = 1.0 / (ROPE_THETA ** (np.arange(0, ATTN_HEAD_DIM, 2, dtype=np.float32) / ATTN_HEAD_DIM))
    freq = jnp.asarray(np.tile(inv_freq.astype(np.float32), V7X_LANES // half).reshape(1, V7X_LANES))
    tbl = jax.ShapeDtypeStruct((bsz, s, V7X_LANES), F32)
    return pl.pallas_call(
        _rope_kernel,
        grid=(bsz,),
        in_specs=[pl.BlockSpec((1, s, 1), lambda b: (b, 0, 0)), pl.BlockSpec((1, V7X_LANES), lambda b: (0, 0))],
        out_specs=[pl.BlockSpec((1, s, V7X_LANES), lambda b: (b, 0, 0))] * 2,
        out_shape=[tbl, tbl],
        compiler_params=_params(("arbitrary",), 6 * s * V7X_LANES * 4),
        name="rope_tables",
    )(positions.reshape(bsz, s, 1), freq)


def _rope(t, cos, sin_signed, first_half):
    half = ATTN_HEAD_DIM // 2
    rot = jnp.where(first_half, pltpu.roll(t, V7X_LANES - half, 1), pltpu.roll(t, half, 1))
    return t * cos + rot * sin_signed


ATTN_VT_ROWS = ATTN_HEAD_DIM + 16


def _attn_kernel(sink_ref, q_ref, k_ref, v_ref, cos_ref, sin_ref, o_ref, qs_ref, kp_ref, vt_ref, bias_ref, *, seq):
    blk = ATTN_BLOCK
    hd = ATTN_HEAD_DIM
    nb = seq // blk
    hp = pl.program_id(1)
    lane = lax.broadcasted_iota(jnp.int32, (seq, V7X_LANES), 1)
    low = lane < hd
    first_half = (lane % hd) < hd // 2
    cos = cos_ref[0]
    sin = sin_ref[0]

    scale = hd ** -0.5 * LOG2_E
    for t in range(4):
        qt = q_ref[0, :, t * V7X_LANES:(t + 1) * V7X_LANES]
        qs_ref[t] = (_rope(qt, cos, sin, first_half) * scale).astype(BF16)

    kr = _rope(k_ref[0], cos, sin, first_half)
    lo = jnp.where(low, kr, 0.0)
    hi = jnp.where(low, 0.0, kr)
    zeros_pad = jnp.zeros((blk, V7X_LANES), BF16)
    for idx, val in enumerate((lo, pltpu.roll(lo, hd, 1), pltpu.roll(hi, hd, 1), hi)):
        kp_ref[idx, 0:blk, :] = zeros_pad
        kp_ref[idx, blk:blk + seq, :] = val.astype(BF16)
        kp_ref[idx, blk + seq:, :] = zeros_pad

    vt = jnp.transpose(v_ref[0])
    ones_row = lax.broadcasted_iota(jnp.int32, (ATTN_VT_ROWS - hd, seq), 0) == 0
    zeros_vt = jnp.zeros((ATTN_VT_ROWS, blk), BF16)
    for j in range(2):
        vt_ref[j, :, 0:blk] = zeros_vt
        vt_ref[j, 0:hd, blk:blk + seq] = vt[j * hd:(j + 1) * hd].astype(BF16)
        vt_ref[j, hd:, blk:blk + seq] = jnp.where(ones_row, 1.0, 0.0).astype(BF16)
        vt_ref[j, :, blk + seq:] = zeros_vt

    ci = lax.broadcasted_iota(jnp.int32, (blk, 2 * blk), 0)
    ri = lax.broadcasted_iota(jnp.int32, (blk, 2 * blk), 1) % blk
    neg = jnp.full((blk, 2 * blk), -jnp.inf, F32)
    bias_ref[0] = jnp.where(ci >= ri, 0.0, -jnp.inf)
    bias_ref[1] = neg
    bias_ref[2] = jnp.where(ci <= ri, 0.0, -jnp.inf)
    bias_ref[3] = neg
    lane2 = lax.broadcasted_iota(jnp.int32, (1, 2 * blk), 1)

    def body(i, carry):
        r0 = pl.multiple_of(i * blk, blk)
        bias_prev = bias_ref[jnp.where(i == 0, 1, 0)]
        bias_next = bias_ref[jnp.where(i == nb - 1, 3, 2)]
        chains = [(j, half) for j in range(2) for half in range(2)]
        scores = []
        for j, half in chains:
            qcat = jnp.concatenate([qs_ref[2 * j, pl.ds(r0, blk), :], qs_ref[2 * j + 1, pl.ds(r0, blk), :]], axis=0)
            kk = kp_ref[2 * j + half, pl.ds(r0, 3 * blk), :]
            scores.append(lax.dot_general(kk, qcat, _CONTRACT_LAST, preferred_element_type=F32))
        probs = []
        for (j, half), st in zip(chains, scores):
            head = hp * 8 + 4 * j + half
            snk = jnp.where(lane2 < blk, sink_ref[head], sink_ref[head + 2]) * LOG2_E
            parts = (st[0:blk] + bias_prev, st[blk:2 * blk], st[2 * blk:] + bias_next)
            m = snk
            for part in parts:
                m = jnp.maximum(m, jnp.max(part, axis=0, keepdims=True))
            p = jnp.concatenate([jnp.exp2(part - m) for part in parts], axis=0)
            probs.append((p.astype(BF16), jnp.exp2(snk - m)))
        outs = []
        for (j, half), (p, sink_p) in zip(chains, probs):
            vtw = vt_ref[j, :, pl.ds(r0, 3 * blk)]
            ot = jnp.dot(vtw, p, preferred_element_type=F32)
            outs.append(ot[0:hd] * (1.0 / (ot[hd:hd + 1] + sink_p)))
        for t in range(4):
            j, tt = divmod(t, 2)
            cols = slice(tt * blk, (tt + 1) * blk)
            tile_t = jnp.concatenate([outs[2 * j][:, cols], outs[2 * j + 1][:, cols]], axis=0)
            o_ref[0, pl.ds(r0, blk), t * V7X_LANES:(t + 1) * V7X_LANES] = jnp.transpose(tile_t).astype(o_ref.dtype)
        return carry

    lax.fori_loop(0, nb, body, 0, unroll=4)


def _attn_call(proj, cos, sin, sink, bsz, seq):
    n_pairs = ATTN_KV_HEADS // 2
    qw = 4 * V7X_LANES
    k_off = (proj.shape[-1] - 2 * ATTN_KV_HEADS * ATTN_HEAD_DIM) // V7X_LANES
    v_off = k_off + n_pairs
    assert seq // ATTN_BLOCK >= 2 and ATTN_BLOCK == V7X_LANES and 2 * ATTN_HEAD_DIM == V7X_LANES
    assert WINDOW == ATTN_BLOCK
    pad_seq = seq + 2 * ATTN_BLOCK
    tbl = pl.BlockSpec((1, seq, V7X_LANES), lambda b, p: (b, 0, 0))
    vmem = (2 * seq * (qw * 4 + 4 * V7X_LANES * 4 + qw * 2) + seq * qw * 2 + 4 * pad_seq * V7X_LANES * 2
            + 2 * ATTN_VT_ROWS * pad_seq * 2 + 3 * 3 * ATTN_BLOCK * 2 * ATTN_BLOCK * 4 + 6 * seq * V7X_LANES * 4)
    return pl.pallas_call(
        functools.partial(_attn_kernel, seq=seq),
        grid=(bsz, n_pairs),
        in_specs=[
            pl.BlockSpec(memory_space=pltpu.SMEM),
            pl.BlockSpec((1, seq, qw), lambda b, p: (b, 0, p)),
            pl.BlockSpec((1, seq, V7X_LANES), lambda b, p: (b, 0, k_off + p)),
            pl.BlockSpec((1, seq, V7X_LANES), lambda b, p: (b, 0, v_off + p)),
            tbl, tbl,
        ],
        out_specs=pl.BlockSpec((1, seq, qw), lambda b, p: (b, 0, p)),
        out_shape=jax.ShapeDtypeStruct((bsz, seq, n_pairs * qw), BF16),
        scratch_shapes=[
            pltpu.VMEM((4, seq, V7X_LANES), BF16),
            pltpu.VMEM((4, pad_seq, V7X_LANES), BF16),
            pltpu.VMEM((2, ATTN_VT_ROWS, pad_seq), BF16),
            pltpu.VMEM((4, ATTN_BLOCK, 2 * ATTN_BLOCK), F32),
        ],
        compiler_params=_params(("arbitrary", "arbitrary"), vmem),
        name="swa_sink_attention",
    )(sink, proj, proj, proj, cos, sin)


def _gates_kernel(h_ref, w_ref, b_ref, o_ref):
    g_t = lax.dot_general(w_ref[...].astype(BF16), h_ref[...], _CONTRACT_LAST, preferred_element_type=F32)
    o_ref[0] = g_t + b_ref[...]


def _gates_call(h, w_t, layer, b_col, bsz, seq):
    m, k = h.shape
    tm = 1024
    tpb = seq // tm
    gate_blk = (w_t.shape[1] - MLSTM_GATES) // MLSTM_GATES
    return pl.pallas_call(
        _gates_kernel,
        grid=(m // tm,),
        in_specs=[
            pl.BlockSpec((tm, k), lambda i: (i, 0)),
            pl.BlockSpec((None, MLSTM_GATES, k), lambda i: (layer, gate_blk, 0)),
            pl.BlockSpec((MLSTM_GATES, 1), lambda i: (0, 0)),
        ],
        out_specs=pl.BlockSpec((1, MLSTM_GATES, tm), lambda i: (i // tpb, 0, i % tpb)),
        out_shape=jax.ShapeDtypeStruct((bsz, MLSTM_GATES, seq), F32),
        compiler_params=_params(("arbitrary",), 2 * tm * k * 2 + 4 * MLSTM_GATES * (k + tm) * 4),
        name="mlstm_gates",
    )(h, w_t, b_col)


def _log_sigmoid(x):
    return jnp.minimum(x, 0.0) - jnp.log1p(jnp.exp(-jnp.abs(x)))


def _mlstm_kernel(q_ref, k_ref, v_ref, g_ref, o_ref, qb_ref, kt_ref, vx_ref, c_ref, *, seq, chunk, dk, dv):
    nh = MLSTM_HEADS
    head = pl.program_id(1)
    nc = seq // chunk
    qb_ref[...] = (q_ref[0] * (dk ** -0.5)).astype(BF16)
    kt_ref[...] = jnp.transpose(k_ref[0])
    vx_ref[:, :dv] = v_ref[0].astype(BF16)
    ones_lane = lax.broadcasted_iota(jnp.int32, (seq, V7X_LANES), 1) == 0
    vx_ref[:, dv:] = jnp.where(ones_lane, 1.0, 0.0).astype(BF16)
    c_ref[...] = jnp.zeros_like(c_ref)

    ti = lax.broadcasted_iota(jnp.int32, (chunk, chunk), 0)
    ui = lax.broadcasted_iota(jnp.int32, (chunk, chunk), 1)
    eye = ti == ui
    seen = (ui <= ti, ui >= ti)
    m_prev = [jnp.full((1, 1), -1e30, F32), jnp.full((1, 1), -1e30, F32)]
    for s in range(nc):
        chunk_of = (s, nc - 1 - s)
        st = []
        for d in range(2):
            rows = slice(chunk_of[d] * chunk, (chunk_of[d] + 1) * chunk)
            gate_row = 2 * d * nh + head
            li = g_ref[0, pl.ds(gate_row, 1), rows]
            lf = _log_sigmoid(g_ref[0, pl.ds(gate_row + nh, 1), rows])
            g_col = jnp.sum(jnp.where(seen[d], lf, 0.0), axis=1, keepdims=True)
            g_row = jnp.sum(jnp.where(eye, g_col, 0.0), axis=0, keepdims=True)
            g_tot = jnp.sum(lf, axis=1, keepdims=True)
            dm = jnp.where(seen[d], g_col - g_row + li, -jnp.inf)
            a = g_col + m_prev[d]
            m_t = jnp.maximum(a, jnp.max(dm, axis=1, keepdims=True))
            w_log = g_tot - g_row + li
            m_new = jnp.maximum(g_tot + m_prev[d], jnp.max(w_log, axis=1, keepdims=True))
            st.append(dict(rows=rows, m_t=m_t, p=jnp.exp(dm - m_t), ea=jnp.exp(a - m_t),
                           decay=jnp.exp(g_tot + m_prev[d] - m_new), w=jnp.exp(w_log - m_new)))
            m_prev[d] = m_new
        for d in range(2):
            t = st[d]
            t["qc"] = qb_ref[t["rows"], :]
            t["ktc"] = kt_ref[:, t["rows"]]
            t["vxc"] = vx_ref[t["rows"], :]
            t["sqk"] = jnp.dot(t["qc"], t["ktc"].astype(BF16), preferred_element_type=F32)
        for d in range(2):
            t = st[d]
            eq = (t["qc"].astype(F32) * t["ea"]).astype(BF16)
            sc = (t["sqk"] * t["p"]).astype(BF16)
            lhs = jnp.concatenate([eq, sc], axis=1)
            rhs = jnp.concatenate([c_ref[d].astype(BF16), t["vxc"]], axis=0)
            tot = jnp.dot(lhs, rhs, preferred_element_type=F32)
            den = tot[:, dv:dv + 1]
            hh = tot[:, :dv] * (1.0 / jnp.maximum(jnp.abs(den), jnp.exp(-t["m_t"])))
            other = nc - 1 - s
            if s < other or (s == other and d == 0):
                o_ref[0, t["rows"], :] = hh
            else:
                o_ref[0, t["rows"], :] += hh
        for d in range(2):
            t = st[d]
            kw = (t["ktc"] * t["w"]).astype(BF16)
            c_ref[d] = t["decay"] * c_ref[d] + jnp.dot(kw, t["vxc"], preferred_element_type=F32)


def _mlstm_call(proj, gates_t, bsz, seq, dk, dv):
    nh = MLSTM_HEADS
    chunk = MLSTM_KERNEL_CHUNK
    k_blk = nh
    v_blk = (2 * nh * dk) // dv
    dvx = dv + V7X_LANES
    vmem = (2 * seq * (2 * dk + 2 * dv) * 4 + 2 * MLSTM_GATES * seq * 4
            + seq * dk * 2 + dk * seq * 4 + seq * dvx * 2 + dk * dvx * 4 + 8 * chunk * dvx * 4)
    return pl.pallas_call(
        functools.partial(_mlstm_kernel, seq=seq, chunk=chunk, dk=dk, dv=dv),
        grid=(bsz, nh),
        in_specs=[
            pl.BlockSpec((1, seq, dk), lambda b, h: (b, 0, h)),
            pl.BlockSpec((1, seq, dk), lambda b, h: (b, 0, k_blk + h)),
            pl.BlockSpec((1, seq, dv), lambda b, h: (b, 0, v_blk + h)),
            pl.BlockSpec((1, MLSTM_GATES, seq), lambda b, h: (b, 0, 0)),
        ],
        out_specs=pl.BlockSpec((1, seq, dv), lambda b, h: (b, 0, h)),
        out_shape=jax.ShapeDtypeStruct((bsz, seq, nh * dv), F32),
        scratch_shapes=[
            pltpu.VMEM((seq, dk), BF16),
            pltpu.VMEM((dk, seq), F32),
            pltpu.VMEM((seq, dvx), BF16),
            pltpu.VMEM((2, dk, dvx), F32),
        ],
        compiler_params=_params(("arbitrary", "arbitrary"), vmem),
        name="bidir_mlstm",
    )(proj, proj, proj, gates_t)


def kernel(x, c, positions, attn_w_qkv, attn_b_qkv, attn_sink, attn_w_o, attn_b_o, mlstm_w_in, mlstm_b_in,
           mlstm_norm_w, mlstm_w_o, mlstm_b_o, mod_w, mod_b, mlp_w1, mlp_b1, mlp_w2, mlp_b2,
           ln_mix_g, ln_mix_b, ln_mlp_g, ln_mlp_b):
    bsz, seq, d = x.shape
    depth = mod_w.shape[0]
    assert depth == DEPTH
    m = bsz * seq
    d_ff = mlp_w1.shape[-1]
    ml_main = mlstm_w_in.shape[-1] - MLSTM_GATES
    ml_dv = mlstm_w_o.shape[1] // MLSTM_HEADS
    ml_dk = (ml_main - 2 * MLSTM_HEADS * ml_dv) // (2 * MLSTM_HEADS)

    c_pad = jnp.pad(c, ((0, 8 - bsz), (0, 0)))
    mod = _mod_call(c_pad, mod_w, mod_b)[:, :bsz]
    mod = mod.reshape(depth, bsz, 6, 1, d)
    sh_m, sc_m, g_m, sh_f, sc_f, g_f = (mod[:, :, j] for j in range(6))

    cos, sin = _rope_call(positions)
    x2 = x.reshape(m, d)
    h = _modulate_call(x2, sc_m[0], sh_m[0], seq)

    mlstm_w_in_t = jnp.swapaxes(mlstm_w_in, 1, 2)

    for i in range(depth):
        j = i // 2
        if i % 2 == 0:
            proj, w_o_bf = _mm_call(h, attn_w_qkv, j, attn_b_qkv[j], n_cols=attn_w_qkv.shape[-1], out_dtype=F32,
                                    side=(attn_w_o, j), name="attn_qkv_proj")
            a = _attn_call(proj.reshape(bsz, seq, -1), cos, sin, attn_sink[j], bsz, seq).reshape(m, d)
            x2, h = _mm_ln_call(a, w_o_bf, attn_b_o[j], x2, g_m[i], ln_mix_g[i], ln_mix_b[i],
                                (sc_f[i], sh_f[i]), rows_per_batch=seq, tm=512, name="attn_out_ln")
        else:
            proj, w_o_bf = _mm_call(h, mlstm_w_in_t, j, mlstm_b_in[j], n_cols=ml_main, out_dtype=F32,
                                    w_transposed=True, side=(mlstm_w_o, j), name="mlstm_in_proj")
            gates_t = _gates_call(h, mlstm_w_in_t, j, mlstm_b_in[j][ml_main:].reshape(MLSTM_GATES, 1), bsz, seq)
            hs = _mlstm_call(proj.reshape(bsz, seq, ml_main), gates_t, bsz, seq, ml_dk, ml_dv).reshape(m, d)
            x2, h = _mm_ln_call(hs, w_o_bf, mlstm_b_o[j], x2, g_m[i], ln_mix_g[i], ln_mix_b[i],
                                (sc_f[i], sh_f[i]), rows_per_batch=seq, tm=512, name="mlstm_out_ln",
                                ogate=(proj, (ml_main - d) // d), normw=mlstm_norm_w[j])
        u, w2_bf = _mm_call(h, mlp_w1, i, mlp_b1[i], n_cols=d_ff, out_dtype=BF16, act="relu2", side=(mlp_w2, i),
                            name="mlp_up")
        nxt = (sc_m[i + 1], sh_m[i + 1]) if i + 1 < depth else None
        x2, h = _mm_ln_call(u, w2_bf, mlp_b2[i], x2, g_f[i], ln_mlp_g[i], ln_mlp_b[i], nxt,
                            rows_per_batch=seq, tm=256, name="mlp_down_ln")
    return x2.reshape(bsz, seq, d)
```

```python
import functools

import numpy as np
import jax
import jax.numpy as jnp
from jax import lax
from jax.experimental import pallas as pl
from jax.experimental.pallas import tpu as pltpu

F32 = jnp.float32
BF16 = jnp.bfloat16

DEPTH = 2
ATTN_HEAD_DIM = 64
ATTN_KV_HEADS = 8
WINDOW = 128
ATTN_BLOCK = 128
ROPE_THETA = 10000.0
MLSTM_HEADS = 4
MLSTM_GATES = 4 * MLSTM_HEADS
DEEPNORM_ALPHA = (2.0 * DEPTH) ** 0.25
LN_EPS = 1e-5
HEAD_NORM_EPS = 1e-6
MLSTM_KERNEL_CHUNK = 256
LOG2_E = 1.4426950408889634

V7X_LANES = 128
V7X_VMEM_BYTES = 64 * 1024 * 1024
VMEM_LIMIT_CAP = 60000 * 1024
VMEM_LIMIT_FLOOR = V7X_VMEM_BYTES - 28 * 1024 * 1024


def _params(semantics, vmem_bytes):
    limit = max(min(int(vmem_bytes * 1.25) + (4 << 20), VMEM_LIMIT_CAP), VMEM_LIMIT_FLOOR)
    return pltpu.CompilerParams(dimension_semantics=semantics, vmem_limit_bytes=limit)


def _mod_kernel(c_ref, w_ref, b_ref, o_ref):
    c = c_ref[...]
    c_act = (c * jax.nn.sigmoid(c)).astype(BF16)
    o_ref[0] = jnp.dot(c_act, w_ref[0].astype(BF16), preferred_element_type=F32) + b_ref[0]


def _mod_call(c_pad, mod_w, mod_b):
    depth, d, n = mod_w.shape
    rows = c_pad.shape[0]
    tn = 1024
    return pl.pallas_call(
        _mod_kernel,
        grid=(depth, n // tn),
        in_specs=[
            pl.BlockSpec((rows, d), lambda i, j: (0, 0)),
            pl.BlockSpec((1, d, tn), lambda i, j: (i, 0, j)),
            pl.BlockSpec((1, 1, tn), lambda i, j: (i, 0, j)),
        ],
        out_specs=pl.BlockSpec((1, rows, tn), lambda i, j: (i, 0, j)),
        out_shape=jax.ShapeDtypeStruct((depth, rows, n), F32),
        compiler_params=_params(("arbitrary", "arbitrary"), 2 * d * tn * 4 + d * tn * 2),
        name="adaln_mod",
    )(c_pad, mod_w, mod_b.reshape(depth, 1, n))


def _modulate_kernel(x_ref, sc_ref, sh_ref, o_ref):
    o_ref[...] = (x_ref[...] * (1.0 + sc_ref[0]) + sh_ref[0]).astype(o_ref.dtype)


def _modulate_call(x2, sc, sh, rows_per_batch):
    m, d = x2.shape
    tm = 512
    tpb = rows_per_batch // tm
    vec = pl.BlockSpec((1, 1, d), lambda i: (i // tpb, 0, 0))
    return pl.pallas_call(
        _modulate_kernel,
        grid=(m // tm,),
        in_specs=[pl.BlockSpec((tm, d), lambda i: (i, 0)), vec, vec],
        out_specs=pl.BlockSpec((tm, d), lambda i: (i, 0)),
        out_shape=jax.ShapeDtypeStruct((m, d), BF16),
        compiler_params=_params(("arbitrary",), 2 * tm * d * 6),
        name="modulate_in",
    )(x2, sc, sh)


_CONTRACT_LAST = (((1,), (1,)), ((), ()))


def _mm_kernel(*refs, act, w_transposed, side_cast):
    if side_cast:
        x_ref, w_ref, b_ref, side_ref, o_ref, side_o_ref, wb_ref = refs
        side_o_ref[...] = side_ref[...].astype(side_o_ref.dtype)
    else:
        x_ref, w_ref, b_ref, o_ref, wb_ref = refs

    @pl.when(pl.program_id(1) == 0)
    def _():
        wb_ref[...] = w_ref[...].astype(BF16)

    if w_transposed:
        acc = lax.dot_general(x_ref[...], wb_ref[...], _CONTRACT_LAST, preferred_element_type=F32)
    else:
        acc = jnp.dot(x_ref[...], wb_ref[...], preferred_element_type=F32)
    acc = acc + b_ref[...]
    if act == "relu2":
        acc = jnp.square(jnp.maximum(acc, 0.0))
    o_ref[...] = acc.astype(o_ref.dtype)


def _mm_call(x, w, layer, b, *, n_cols, out_dtype, act=None, w_transposed=False, side=None, tm=1024, tn=1024, name):
    m, k = x.shape
    osz = jnp.dtype(out_dtype).itemsize
    vmem = 2 * k * tn * 4 + k * tn * 2 + 2 * tm * k * 2 + 2 * tm * tn * osz + tm * tn * 4
    if w_transposed:
        w_spec = pl.BlockSpec((None, tn, k), lambda j, i: (layer, j, 0))
        w_tile = (tn, k)
    else:
        w_spec = pl.BlockSpec((None, k, tn), lambda j, i: (layer, 0, j))
        w_tile = (k, tn)
    n_i = m // tm
    grid = (n_cols // tn, n_i)
    in_specs = [pl.BlockSpec((tm, k), lambda j, i: (i, 0)), w_spec, pl.BlockSpec((1, tn), lambda j, i: (0, j))]
    args = [x, w, b.reshape(1, -1)]
    out_specs = [pl.BlockSpec((tm, tn), lambda j, i: (i, j))]
    out_shape = [jax.ShapeDtypeStruct((m, n_cols), out_dtype)]
    if side is not None:
        w2, layer2 = side
        _, k2, n2 = w2.shape
        n_blk = 1 << ((grid[0] * grid[1]).bit_length() - 1)
        rows = k2 // n_blk
        assert rows * n_blk == k2 and rows % 16 == 0
        in_specs.append(pl.BlockSpec((None, rows, n2), lambda j, i: (layer2, jnp.minimum(j * n_i + i, n_blk - 1), 0)))
        args.append(w2)
        out_specs.append(pl.BlockSpec((rows, n2), lambda j, i: (jnp.minimum(j * n_i + i, n_blk - 1), 0)))
        out_shape.append(jax.ShapeDtypeStruct((k2, n2), BF16))
        vmem += 2 * rows * n2 * 6
    outs = pl.pallas_call(
        functools.partial(_mm_kernel, act=act, w_transposed=w_transposed, side_cast=side is not None),
        grid=grid,
        in_specs=in_specs,
        out_specs=out_specs,
        out_shape=out_shape,
        scratch_shapes=[pltpu.VMEM(w_tile, BF16)],
        compiler_params=_params(("arbitrary", "arbitrary"), vmem),
        name=name,
    )(*args)
    return (outs[0], outs[1]) if side is not None else outs[0]


def _head_norm_gate(hs, og, nw, heads):
    dv = hs.shape[-1] // heads
    outs = []
    for h in range(heads):
        t = hs[:, h * dv:(h + 1) * dv]
        mu = jnp.mean(t, axis=-1, keepdims=True)
        var = jnp.mean(jnp.square(t - mu), axis=-1, keepdims=True)
        outs.append((t - mu) * lax.rsqrt(var + HEAD_NORM_EPS))
    hn = jnp.concatenate(outs, axis=-1) * nw
    return jax.nn.sigmoid(og) * hn


def _mm_ln_kernel(*refs, mlstm_prologue, emit_h):
    refs = list(refs)
    a_ref = refs.pop(0)
    if mlstm_prologue:
        og_ref = refs.pop(0)
        nw_ref = refs.pop(0)
    w_ref, b_ref, res_ref, gate_ref, lng_ref, lnb_ref = refs[:6]
    refs = refs[6:]
    if emit_h:
        scn_ref, shn_ref = refs[:2]
        refs = refs[2:]
    xo_ref = refs.pop(0)
    if emit_h:
        ho_ref = refs.pop(0)

    if mlstm_prologue:
        a = _head_norm_gate(a_ref[...], og_ref[...], nw_ref[...], MLSTM_HEADS).astype(BF16)
    else:
        a = a_ref[...]
    y = jnp.dot(a, w_ref[...], preferred_element_type=F32) + b_ref[...]
    z = DEEPNORM_ALPHA * res_ref[...] + (1.0 + gate_ref[0]) * y
    mu = jnp.mean(z, axis=-1, keepdims=True)
    var = jnp.mean(jnp.square(z - mu), axis=-1, keepdims=True)
    xn = (z - mu) * lax.rsqrt(var + LN_EPS) * lng_ref[...] + lnb_ref[...]
    xo_ref[...] = xn
    if emit_h:
        ho_ref[...] = (xn * (1.0 + scn_ref[0]) + shn_ref[0]).astype(ho_ref.dtype)


def _mm_ln_call(a, w_bf, b, res, gate, lng, lnb, nxt, *, rows_per_batch, tm, name, ogate=None, normw=None):
    m = a.shape[0]
    k, d = w_bf.shape
    tpb = rows_per_batch // tm
    mlstm_prologue = ogate is not None
    emit_h = nxt is not None
    vec_b = pl.BlockSpec((1, 1, d), lambda i: (i // tpb, 0, 0))
    vec = pl.BlockSpec((1, d), lambda i: (0, 0))
    row_tile = pl.BlockSpec((tm, d), lambda i: (i, 0))

    in_specs = [pl.BlockSpec((tm, k), lambda i: (i, 0))]
    args = [a]
    asz = a.dtype.itemsize
    if mlstm_prologue:
        ocb = ogate[1]
        in_specs += [pl.BlockSpec((tm, k), lambda i: (i, ocb)), pl.BlockSpec((1, k), lambda i: (0, 0))]
        args += [ogate[0], normw.reshape(1, -1)]
    in_specs += [pl.BlockSpec((k, d), lambda i: (0, 0), pipeline_mode=pl.Buffered(1)),
                 vec, row_tile, vec_b, vec, vec]
    args += [w_bf, b.reshape(1, d), res, gate, lng.reshape(1, d), lnb.reshape(1, d)]
    out_specs = [row_tile]
    out_shape = [jax.ShapeDtypeStruct((m, d), F32)]
    if emit_h:
        in_specs += [vec_b, vec_b]
        args += [nxt[0], nxt[1]]
        out_specs.append(row_tile)
        out_shape.append(jax.ShapeDtypeStruct((m, d), BF16))
    vmem = (k * d * 2 + 2 * tm * k * asz * (2 if mlstm_prologue else 1)
            + 2 * tm * d * (4 + 4 + 2) + 2 * tm * d * 4)
    outs = pl.pallas_call(
        functools.partial(_mm_ln_kernel, mlstm_prologue=mlstm_prologue, emit_h=emit_h),
        grid=(m // tm,),
        in_specs=in_specs,
        out_specs=out_specs,
        out_shape=out_shape,
        compiler_params=_params(("arbitrary",), vmem),
        name=name,
    )(*args)
    return (outs[0], outs[1]) if emit_h else (outs[0], None)


def _rope_kernel(pos_ref, freq_ref, cos_ref, sin_ref):
    ang = pos_ref[0].astype(F32) * freq_ref[...]
    lane = lax.broadcasted_iota(jnp.int32, ang.shape, 1)
    first_half = (lane % ATTN_HEAD_DIM) < ATTN_HEAD_DIM // 2
    cos_ref[0] = jnp.cos(ang)
    s = jnp.sin(ang)
    sin_ref[0] = jnp.where(first_half, -s, s)


def _rope_call(positions):
    bsz, s = positions.shape
    half = ATTN_HEAD_DIM // 2
    inv_freq = 1.0 / (ROPE_THETA ** (np.arange(0, ATTN_HEAD_DIM, 2, dtype=np.float32) / ATTN_HEAD_DIM))
    freq = jnp.asarray(np.tile(inv_freq.astype(np.float32), V7X_LANES // half).reshape(1, V7X_LANES))
    tbl = jax.ShapeDtypeStruct((bsz, s, V7X_LANES), F32)
    return pl.pallas_call(
        _rope_kernel,
        grid=(bsz,),
        in_specs=[pl.BlockSpec((1, s, 1), lambda b: (b, 0, 0)), pl.BlockSpec((1, V7X_LANES), lambda b: (0, 0))],
        out_specs=[pl.BlockSpec((1, s, V7X_LANES), lambda b: (b, 0, 0))] * 2,
        out_shape=[tbl, tbl],
        compiler_params=_params(("arbitrary",), 6 * s * V7X_LANES * 4),
        name="rope_tables",
    )(positions.reshape(bsz, s, 1), freq)


def _rope(t, cos, sin_signed, first_half):
    half = ATTN_HEAD_DIM // 2
    rot = jnp.where(first_half, pltpu.roll(t, V7X_LANES - half, 1), pltpu.roll(t, half, 1))
    return t * cos + rot * sin_signed


ATTN_VT_ROWS = ATTN_HEAD_DIM + 16


def _attn_kernel(sink_ref, q_ref, k_ref, v_ref, cos_ref, sin_ref, o_ref, qs_ref, kp_ref, vt_ref, bias_ref, *, seq):
    blk = ATTN_BLOCK
    hd = ATTN_HEAD_DIM
    nb = seq // blk
    hp = pl.program_id(1)
    lane = lax.broadcasted_iota(jnp.int32, (seq, V7X_LANES), 1)
    low = lane < hd
    first_half = (lane % hd) < hd // 2
    cos = cos_ref[0]
    sin = sin_ref[0]

    scale = hd ** -0.5 * LOG2_E
    for t in range(4):
        qt = q_ref[0, :, t * V7X_LANES:(t + 1) * V7X_LANES]
        qs_ref[t] = (_rope(qt, cos, sin, first_half) * scale).astype(BF16)

    kr = _rope(k_ref[0], cos, sin, first_half)
    lo = jnp.where(low, kr, 0.0)
    hi = jnp.where(low, 0.0, kr)
    zeros_pad = jnp.zeros((blk, V7X_LANES), BF16)
    for idx, val in enumerate((lo, pltpu.roll(lo, hd, 1), pltpu.roll(hi, hd, 1), hi)):
        kp_ref[idx, 0:blk, :] = zeros_pad
        kp_ref[idx, blk:blk + seq, :] = val.astype(BF16)
        kp_ref[idx, blk + seq:, :] = zeros_pad

    vt = jnp.transpose(v_ref[0])
    ones_row = lax.broadcasted_iota(jnp.int32, (ATTN_VT_ROWS - hd, seq), 0) == 0
    zeros_vt = jnp.zeros((ATTN_VT_ROWS, blk), BF16)
    for j in range(2):
        vt_ref[j, :, 0:blk] = zeros_vt
        vt_ref[j, 0:hd, blk:blk + seq] = vt[j * hd:(j + 1) * hd].astype(BF16)
        vt_ref[j, hd:, blk:blk + seq] = jnp.where(ones_row, 1.0, 0.0).astype(BF16)
        vt_ref[j, :, blk + seq:] = zeros_vt

    ci = lax.broadcasted_iota(jnp.int32, (blk, 2 * blk), 0)
    ri = lax.broadcasted_iota(jnp.int32, (blk, 2 * blk), 1) % blk
    neg = jnp.full((blk, 2 * blk), -jnp.inf, F32)
    bias_ref[0] = jnp.where(ci >= ri, 0.0, -jnp.inf)
    bias_ref[1] = neg
    bias_ref[2] = jnp.where(ci <= ri, 0.0, -jnp.inf)
    bias_ref[3] = neg
    lane2 = lax.broadcasted_iota(jnp.int32, (1, 2 * blk), 1)

    def body(i, carry):
        r0 = pl.multiple_of(i * blk, blk)
        bias_prev = bias_ref[jnp.where(i == 0, 1, 0)]
        bias_next = bias_ref[jnp.where(i == nb - 1, 3, 2)]
        chains = [(j, half) for j in range(2) for half in range(2)]
        scores = []
        for j, half in chains:
            qcat = jnp.concatenate([qs_ref[2 * j, pl.ds(r0, blk), :], qs_ref[2 * j + 1, pl.ds(r0, blk), :]], axis=0)
            kk = kp_ref[2 * j + half, pl.ds(r0, 3 * blk), :]
            scores.append(lax.dot_general(kk, qcat, _CONTRACT_LAST, preferred_element_type=F32))
        probs = []
        for (j, half), st in zip(chains, scores):
            head = hp * 8 + 4 * j + half
            snk = jnp.where(lane2 < blk, sink_ref[head], sink_ref[head + 2]) * LOG2_E
            parts = (st[0:blk] + bias_prev, st[blk:2 * blk], st[2 * blk:] + bias_next)
            m = snk
            for part in parts:
                m = jnp.maximum(m, jnp.max(part, axis=0, keepdims=True))
            p = jnp.concatenate([jnp.exp2(part - m) for part in parts], axis=0)
            probs.append((p.astype(BF16), jnp.exp2(snk - m)))
        outs = []
        for (j, half), (p, sink_p) in zip(chains, probs):
            vtw = vt_ref[j, :, pl.ds(r0, 3 * blk)]
            ot = jnp.dot(vtw, p, preferred_element_type=F32)
            outs.append(ot[0:hd] * (1.0 / (ot[hd:hd + 1] + sink_p)))
        for t in range(4):
            j, tt = divmod(t, 2)
            cols = slice(tt * blk, (tt + 1) * blk)
            tile_t = jnp.concatenate([outs[2 * j][:, cols], outs[2 * j + 1][:, cols]], axis=0)
            o_ref[0, pl.ds(r0, blk), t * V7X_LANES:(t + 1) * V7X_LANES] = jnp.transpose(tile_t).astype(o_ref.dtype)
        return carry

    lax.fori_loop(0, nb, body, 0, unroll=8)


def _attn_call(proj, cos, sin, sink, bsz, seq):
    n_pairs = ATTN_KV_HEADS // 2
    qw = 4 * V7X_LANES
    k_off = (proj.shape[-1] - 2 * ATTN_KV_HEADS * ATTN_HEAD_DIM) // V7X_LANES
    v_off = k_off + n_pairs
    assert seq // ATTN_BLOCK >= 2 and ATTN_BLOCK == V7X_LANES and 2 * ATTN_HEAD_DIM == V7X_LANES
    assert WINDOW == ATTN_BLOCK
    pad_seq = seq + 2 * ATTN_BLOCK
    tbl = pl.BlockSpec((1, seq, V7X_LANES), lambda b, p: (b, 0, 0))
    vmem = (2 * seq * (qw * 4 + 4 * V7X_LANES * 4 + qw * 2) + seq * qw * 2 + 4 * pad_seq * V7X_LANES * 2
            + 2 * ATTN_VT_ROWS * pad_seq * 2 + 3 * 3 * ATTN_BLOCK * 2 * ATTN_BLOCK * 4 + 6 * seq * V7X_LANES * 4)
    return pl.pallas_call(
        functools.partial(_attn_kernel, seq=seq),
        grid=(bsz, n_pairs),
        in_specs=[
            pl.BlockSpec(memory_space=pltpu.SMEM),
            pl.BlockSpec((1, seq, qw), lambda b, p: (b, 0, p)),
            pl.BlockSpec((1, seq, V7X_LANES), lambda b, p: (b, 0, k_off + p)),
            pl.BlockSpec((1, seq, V7X_LANES), lambda b, p: (b, 0, v_off + p)),
            tbl, tbl,
        ],
        out_specs=pl.BlockSpec((1, seq, qw), lambda b, p: (b, 0, p)),
        out_shape=jax.ShapeDtypeStruct((bsz, seq, n_pairs * qw), BF16),
        scratch_shapes=[
            pltpu.VMEM((4, seq, V7X_LANES), BF16),
            pltpu.VMEM((4, pad_seq, V7X_LANES), BF16),
            pltpu.VMEM((2, ATTN_VT_ROWS, pad_seq), BF16),
            pltpu.VMEM((4, ATTN_BLOCK, 2 * ATTN_BLOCK), F32),
        ],
        compiler_params=_params(("arbitrary", "arbitrary"), vmem),
        name="swa_sink_attention",
    )(sink, proj, proj, proj, cos, sin)


def _gates_kernel(h_ref, w_ref, b_ref, o_ref):
    g_t = lax.dot_general(w_ref[...].astype(BF16), h_ref[...], _CONTRACT_LAST, preferred_element_type=F32)
    o_ref[0] = g_t + b_ref[...]


def _gates_call(h, w_t, layer, b_col, bsz, seq):
    m, k = h.shape
    tm = 1024
    tpb = seq // tm
    gate_blk = (w_t.shape[1] - MLSTM_GATES) // MLSTM_GATES
    return pl.pallas_call(
        _gates_kernel,
        grid=(m // tm,),
        in_specs=[
            pl.BlockSpec((tm, k), lambda i: (i, 0)),
            pl.BlockSpec((None, MLSTM_GATES, k), lambda i: (layer, gate_blk, 0)),
            pl.BlockSpec((MLSTM_GATES, 1), lambda i: (0, 0)),
        ],
        out_specs=pl.BlockSpec((1, MLSTM_GATES, tm), lambda i: (i // tpb, 0, i % tpb)),
        out_shape=jax.ShapeDtypeStruct((bsz, MLSTM_GATES, seq), F32),
        compiler_params=_params(("arbitrary",), 2 * tm * k * 2 + 4 * MLSTM_GATES * (k + tm) * 4),
        name="mlstm_gates",
    )(h, w_t, b_col)


def _log_sigmoid(x):
    return jnp.minimum(x, 0.0) - jnp.log1p(jnp.exp(-jnp.abs(x)))


def _mlstm_kernel(q_ref, k_ref, v_ref, g_ref, o_ref, qb_ref, kt_ref, vx_ref, c_ref, *, seq, chunk, dk, dv):
    nh = MLSTM_HEADS
    head = pl.program_id(1)
    nc = seq // chunk
    qb_ref[...] = (q_ref[0] * (dk ** -0.5)).astype(BF16)
    kt_ref[...] = jnp.transpose(k_ref[0])
    vx_ref[:, :dv] = v_ref[0].astype(BF16)
    ones_lane = lax.broadcasted_iota(jnp.int32, (seq, V7X_LANES), 1) == 0
    vx_ref[:, dv:] = jnp.where(ones_lane, 1.0, 0.0).astype(BF16)

    ti = lax.broadcasted_iota(jnp.int32, (chunk, chunk), 0)
    ui = lax.broadcasted_iota(jnp.int32, (chunk, chunk), 1)
    eye = ti == ui
    for direction in range(2):
        seen = (ui <= ti) if direction == 0 else (ui >= ti)
        c_ref[...] = jnp.zeros_like(c_ref)
        m_prev = jnp.full((1, 1), -1e30, F32)
        order = range(nc) if direction == 0 else range(nc - 1, -1, -1)
        for c in order:
            rows = slice(c * chunk, (c + 1) * chunk)
            gate_row = 2 * direction * nh + head
            li = g_ref[0, pl.ds(gate_row, 1), rows]
            lf = _log_sigmoid(g_ref[0, pl.ds(gate_row + nh, 1), rows])
            g_col = jnp.sum(jnp.where(seen, lf, 0.0), axis=1, keepdims=True)
            g_row = jnp.sum(jnp.where(eye, g_col, 0.0), axis=0, keepdims=True)
            g_tot = jnp.sum(lf, axis=1, keepdims=True)
            dm = jnp.where(seen, g_col - g_row + li, -jnp.inf)
            a = g_col + m_prev
            m_t = jnp.maximum(a, jnp.max(dm, axis=1, keepdims=True))
            p = jnp.exp(dm - m_t)
            ea = jnp.exp(a - m_t)

            qc = qb_ref[rows, :]
            ktc = kt_ref[:, rows]
            vxc = vx_ref[rows, :]
            sqk = jnp.dot(qc, ktc.astype(BF16), preferred_element_type=F32)
            sc = (sqk * p).astype(BF16)
            tot = (ea * jnp.dot(qc, c_ref[...].astype(BF16), preferred_element_type=F32)
                   + jnp.dot(sc, vxc, preferred_element_type=F32))
            den = tot[:, dv:dv + 1]
            hh = tot[:, :dv] * (1.0 / jnp.maximum(jnp.abs(den), jnp.exp(-m_t)))
            if direction == 0:
                o_ref[0, rows, :] = hh
            else:
                o_ref[0, rows, :] += hh

            w_log = g_tot - g_row + li
            m_new = jnp.maximum(g_tot + m_prev, jnp.max(w_log, axis=1, keepdims=True))
            decay = jnp.exp(g_tot + m_prev - m_new)
            w = jnp.exp(w_log - m_new)
            c_ref[...] = decay * c_ref[...] + jnp.dot((ktc * w).astype(BF16), vxc, preferred_element_type=F32)
            m_prev = m_new


def _mlstm_call(proj, gates_t, bsz, seq, dk, dv):
    nh = MLSTM_HEADS
    chunk = MLSTM_KERNEL_CHUNK
    k_blk = nh
    v_blk = (2 * nh * dk) // dv
    dvx = dv + V7X_LANES
    vmem = (2 * seq * (2 * dk + 2 * dv) * 4 + 2 * MLSTM_GATES * seq * 4
            + seq * dk * 2 + dk * seq * 4 + seq * dvx * 2 + dk * dvx * 4 + 8 * chunk * dvx * 4)
    return pl.pallas_call(
        functools.partial(_mlstm_kernel, seq=seq, chunk=chunk, dk=dk, dv=dv),
        grid=(bsz, nh),
        in_specs=[
            pl.BlockSpec((1, seq, dk), lambda b, h: (b, 0, h)),
            pl.BlockSpec((1, seq, dk), lambda b, h: (b, 0, k_blk + h)),
            pl.BlockSpec((1, seq, dv), lambda b, h: (b, 0, v_blk + h)),
            pl.BlockSpec((1, MLSTM_GATES, seq), lambda b, h: (b, 0, 0)),
        ],
        out_specs=pl.BlockSpec((1, seq, dv), lambda b, h: (b, 0, h)),
        out_shape=jax.ShapeDtypeStruct((bsz, seq, nh * dv), F32),
        scratch_shapes=[
            pltpu.VMEM((seq, dk), BF16),
            pltpu.VMEM((dk, seq), F32),
            pltpu.VMEM((seq, dvx), BF16),
            pltpu.VMEM((dk, dvx), F32),
        ],
        compiler_params=_params(("arbitrary", "arbitrary"), vmem),
        name="bidir_mlstm",
    )(proj, proj, proj, gates_t)


def kernel(x, c, positions, attn_w_qkv, attn_b_qkv, attn_sink, attn_w_o, attn_b_o, mlstm_w_in, mlstm_b_in,
           mlstm_norm_w, mlstm_w_o, mlstm_b_o, mod_w, mod_b, mlp_w1, mlp_b1, mlp_w2, mlp_b2,
           ln_mix_g, ln_mix_b, ln_mlp_g, ln_mlp_b):
    bsz, seq, d = x.shape
    depth = mod_w.shape[0]
    assert depth == DEPTH
    m = bsz * seq
    d_ff = mlp_w1.shape[-1]
    ml_main = mlstm_w_in.shape[-1] - MLSTM_GATES
    ml_dv = mlstm_w_o.shape[1] // MLSTM_HEADS
    ml_dk = (ml_main - 2 * MLSTM_HEADS * ml_dv) // (2 * MLSTM_HEADS)

    c_pad = jnp.pad(c, ((0, 8 - bsz), (0, 0)))
    mod = _mod_call(c_pad, mod_w, mod_b)[:, :bsz]
    mod = mod.reshape(depth, bsz, 6, 1, d)
    sh_m, sc_m, g_m, sh_f, sc_f, g_f = (mod[:, :, j] for j in range(6))

    cos, sin = _rope_call(positions)
    x2 = x.reshape(m, d)
    h = _modulate_call(x2, sc_m[0], sh_m[0], seq)

    mlstm_w_in_t = jnp.swapaxes(mlstm_w_in, 1, 2)

    for i in range(depth):
        j = i // 2
        if i % 2 == 0:
            proj, w_o_bf = _mm_call(h, attn_w_qkv, j, attn_b_qkv[j], n_cols=attn_w_qkv.shape[-1], out_dtype=F32,
                                    side=(attn_w_o, j), name="attn_qkv_proj")
            a = _attn_call(proj.reshape(bsz, seq, -1), cos, sin, attn_sink[j], bsz, seq).reshape(m, d)
            x2, h = _mm_ln_call(a, w_o_bf, attn_b_o[j], x2, g_m[i], ln_mix_g[i], ln_mix_b[i],
                                (sc_f[i], sh_f[i]), rows_per_batch=seq, tm=512, name="attn_out_ln")
        else:
            proj, w_o_bf = _mm_call(h, mlstm_w_in_t, j, mlstm_b_in[j], n_cols=ml_main, out_dtype=F32,
                                    w_transposed=True, side=(mlstm_w_o, j), name="mlstm_in_proj")
            gates_t = _gates_call(h, mlstm_w_in_t, j, mlstm_b_in[j][ml_main:].reshape(MLSTM_GATES, 1), bsz, seq)
            hs = _mlstm_call(proj.reshape(bsz, seq, ml_main), gates_t, bsz, seq, ml_dk, ml_dv).reshape(m, d)
            x2, h = _mm_ln_call(hs, w_o_bf, mlstm_b_o[j], x2, g_m[i], ln_mix_g[i], ln_mix_b[i],
                                (sc_f[i], sh_f[i]), rows_per_batch=seq, tm=512, name="mlstm_out_ln",
                                ogate=(proj, (ml_main - d) // d), normw=mlstm_norm_w[j])
        u, w2_bf = _mm_call(h, mlp_w1, i, mlp_b1[i], n_cols=d_ff, out_dtype=BF16, act="relu2", side=(mlp_w2, i),
                            name="mlp_up")
        nxt = (sc_m[i + 1], sh_m[i + 1]) if i + 1 < depth else None
        x2, h = _mm_ln_call(u, w2_bf, mlp_b2[i], x2, g_f[i], ln_mlp_g[i], ln_mlp_b[i], nxt,
                            rows_per_batch=seq, tm=256, name="mlp_down_ln")
    return x2.reshape(bsz, seq, d)
```

```python
import functools

import numpy as np
import jax
import jax.numpy as jnp
from jax import lax
from jax.experimental import pallas as pl
from jax.experimental.pallas import tpu as pltpu

F32 = jnp.float32
BF16 = jnp.bfloat16

DEPTH = 2
ATTN_HEAD_DIM = 64
ATTN_KV_HEADS = 8
WINDOW = 128
ATTN_BLOCK = 128
ROPE_THETA = 10000.0
MLSTM_HEADS = 4
MLSTM_GATES = 4 * MLSTM_HEADS
DEEPNORM_ALPHA = (2.0 * DEPTH) ** 0.25
LN_EPS = 1e-5
HEAD_NORM_EPS = 1e-6
MLSTM_KERNEL_CHUNK = 256
LOG2_E = 1.4426950408889634

V7X_LANES = 128
V7X_VMEM_BYTES = 64 * 1024 * 1024
VMEM_LIMIT_CAP = 60000 * 1024
VMEM_LIMIT_FLOOR = V7X_VMEM_BYTES - 28 * 1024 * 1024


def _params(semantics, vmem_bytes):
    limit = max(min(int(vmem_bytes * 1.25) + (4 << 20), VMEM_LIMIT_CAP), VMEM_LIMIT_FLOOR)
    return pltpu.CompilerParams(dimension_semantics=semantics, vmem_limit_bytes=limit)


def _mod_kernel(c_ref, w_ref, b_ref, o_ref):
    c = c_ref[...]
    c_act = (c * jax.nn.sigmoid(c)).astype(BF16)
    o_ref[0] = jnp.dot(c_act, w_ref[0].astype(BF16), preferred_element_type=F32) + b_ref[0]


def _mod_call(c_pad, mod_w, mod_b):
    depth, d, n = mod_w.shape
    rows = c_pad.shape[0]
    tn = 1024
    return pl.pallas_call(
        _mod_kernel,
        grid=(depth, n // tn),
        in_specs=[
            pl.BlockSpec((rows, d), lambda i, j: (0, 0)),
            pl.BlockSpec((1, d, tn), lambda i, j: (i, 0, j)),
            pl.BlockSpec((1, 1, tn), lambda i, j: (i, 0, j)),
        ],
        out_specs=pl.BlockSpec((1, rows, tn), lambda i, j: (i, 0, j)),
        out_shape=jax.ShapeDtypeStruct((depth, rows, n), F32),
        compiler_params=_params(("arbitrary", "arbitrary"), 2 * d * tn * 4 + d * tn * 2),
        name="adaln_mod",
    )(c_pad, mod_w, mod_b.reshape(depth, 1, n))


def _modulate_kernel(x_ref, sc_ref, sh_ref, o_ref):
    o_ref[...] = (x_ref[...] * (1.0 + sc_ref[0]) + sh_ref[0]).astype(o_ref.dtype)


def _modulate_call(x2, sc, sh, rows_per_batch):
    m, d = x2.shape
    tm = 512
    tpb = rows_per_batch // tm
    vec = pl.BlockSpec((1, 1, d), lambda i: (i // tpb, 0, 0))
    return pl.pallas_call(
        _modulate_kernel,
        grid=(m // tm,),
        in_specs=[pl.BlockSpec((tm, d), lambda i: (i, 0)), vec, vec],
        out_specs=pl.BlockSpec((tm, d), lambda i: (i, 0)),
        out_shape=jax.ShapeDtypeStruct((m, d), BF16),
        compiler_params=_params(("arbitrary",), 2 * tm * d * 6),
        name="modulate_in",
    )(x2, sc, sh)


_CONTRACT_LAST = (((1,), (1,)), ((), ()))


def _mm_kernel(*refs, act, w_transposed, side_cast):
    if side_cast:
        x_ref, w_ref, b_ref, side_ref, o_ref, side_o_ref, wb_ref = refs
        side_o_ref[...] = side_ref[...].astype(side_o_ref.dtype)
    else:
        x_ref, w_ref, b_ref, o_ref, wb_ref = refs

    @pl.when(pl.program_id(1) == 0)
    def _():
        wb_ref[...] = w_ref[...].astype(BF16)

    if w_transposed:
        acc = lax.dot_general(x_ref[...], wb_ref[...], _CONTRACT_LAST, preferred_element_type=F32)
    else:
        acc = jnp.dot(x_ref[...], wb_ref[...], preferred_element_type=F32)
    acc = acc + b_ref[...]
    if act == "relu2":
        acc = jnp.square(jnp.maximum(acc, 0.0))
    o_ref[...] = acc.astype(o_ref.dtype)


def _mm_call(x, w, layer, b, *, n_cols, out_dtype, act=None, w_transposed=False, side=None, tm=1024, tn=1024, name):
    m, k = x.shape
    osz = jnp.dtype(out_dtype).itemsize
    vmem = 2 * k * tn * 4 + k * tn * 2 + 2 * tm * k * 2 + 2 * tm * tn * osz + tm * tn * 4
    if w_transposed:
        w_spec = pl.BlockSpec((None, tn, k), lambda j, i: (layer, j, 0))
        w_tile = (tn, k)
    else:
        w_spec = pl.BlockSpec((None, k, tn), lambda j, i: (layer, 0, j))
        w_tile = (k, tn)
    n_i = m // tm
    grid = (n_cols // tn, n_i)
    in_specs = [pl.BlockSpec((tm, k), lambda j, i: (i, 0)), w_spec, pl.BlockSpec((1, tn), lambda j, i: (0, j))]
    args = [x, w, b.reshape(1, -1)]
    out_specs = [pl.BlockSpec((tm, tn), lambda j, i: (i, j))]
    out_shape = [jax.ShapeDtypeStruct((m, n_cols), out_dtype)]
    if side is not None:
        w2, layer2 = side
        _, k2, n2 = w2.shape
        n_blk = 1 << ((grid[0] * grid[1]).bit_length() - 1)
        rows = k2 // n_blk
        assert rows * n_blk == k2 and rows % 16 == 0
        in_specs.append(pl.BlockSpec((None, rows, n2), lambda j, i: (layer2, jnp.minimum(j * n_i + i, n_blk - 1), 0)))
        args.append(w2)
        out_specs.append(pl.BlockSpec((rows, n2), lambda j, i: (jnp.minimum(j * n_i + i, n_blk - 1), 0)))
        out_shape.append(jax.ShapeDtypeStruct((k2, n2), BF16))
        vmem += 2 * rows * n2 * 6
    outs = pl.pallas_call(
        functools.partial(_mm_kernel, act=act, w_transposed=w_transposed, side_cast=side is not None),
        grid=grid,
        in_specs=in_specs,
        out_specs=out_specs,
        out_shape=out_shape,
        scratch_shapes=[pltpu.VMEM(w_tile, BF16)],
        compiler_params=_params(("arbitrary", "arbitrary"), vmem),
        name=name,
    )(*args)
    return (outs[0], outs[1]) if side is not None else outs[0]


def _head_norm_gate(hs, og, nw, heads):
    dv = hs.shape[-1] // heads
    outs = []
    for h in range(heads):
        t = hs[:, h * dv:(h + 1) * dv]
        mu = jnp.mean(t, axis=-1, keepdims=True)
        var = jnp.mean(jnp.square(t - mu), axis=-1, keepdims=True)
        outs.append((t - mu) * lax.rsqrt(var + HEAD_NORM_EPS))
    hn = jnp.concatenate(outs, axis=-1) * nw
    return jax.nn.sigmoid(og) * hn


def _mm_ln_kernel(*refs, mlstm_prologue, emit_h):
    refs = list(refs)
    a_ref = refs.pop(0)
    if mlstm_prologue:
        og_ref = refs.pop(0)
        nw_ref = refs.pop(0)
    w_ref, b_ref, res_ref, gate_ref, lng_ref, lnb_ref = refs[:6]
    refs = refs[6:]
    if emit_h:
        scn_ref, shn_ref = refs[:2]
        refs = refs[2:]
    xo_ref = refs.pop(0)
    if emit_h:
        ho_ref = refs.pop(0)

    if mlstm_prologue:
        a = _head_norm_gate(a_ref[...], og_ref[...], nw_ref[...], MLSTM_HEADS).astype(BF16)
    else:
        a = a_ref[...]
    y = jnp.dot(a, w_ref[...], preferred_element_type=F32) + b_ref[...]
    z = DEEPNORM_ALPHA * res_ref[...] + (1.0 + gate_ref[0]) * y
    mu = jnp.mean(z, axis=-1, keepdims=True)
    var = jnp.mean(jnp.square(z - mu), axis=-1, keepdims=True)
    xn = (z - mu) * lax.rsqrt(var + LN_EPS) * lng_ref[...] + lnb_ref[...]
    xo_ref[...] = xn
    if emit_h:
        ho_ref[...] = (xn * (1.0 + scn_ref[0]) + shn_ref[0]).astype(ho_ref.dtype)


def _mm_ln_call(a, w_bf, b, res, gate, lng, lnb, nxt, *, rows_per_batch, tm, name, ogate=None, normw=None):
    m = a.shape[0]
    k, d = w_bf.shape
    tpb = rows_per_batch // tm
    mlstm_prologue = ogate is not None
    emit_h = nxt is not None
    vec_b = pl.BlockSpec((1, 1, d), lambda i: (i // tpb, 0, 0))
    vec = pl.BlockSpec((1, d), lambda i: (0, 0))
    row_tile = pl.BlockSpec((tm, d), lambda i: (i, 0))

    in_specs = [pl.BlockSpec((tm, k), lambda i: (i, 0))]
    args = [a]
    asz = a.dtype.itemsize
    if mlstm_prologue:
        ocb = ogate[1]
        in_specs += [pl.BlockSpec((tm, k), lambda i: (i, ocb)), pl.BlockSpec((1, k), lambda i: (0, 0))]
        args += [ogate[0], normw.reshape(1, -1)]
    in_specs += [pl.BlockSpec((k, d), lambda i: (0, 0), pipeline_mode=pl.Buffered(1)),
                 vec, row_tile, vec_b, vec, vec]
    args += [w_bf, b.reshape(1, d), res, gate, lng.reshape(1, d), lnb.reshape(1, d)]
    out_specs = [row_tile]
    out_shape = [jax.ShapeDtypeStruct((m, d), F32)]
    if emit_h:
        in_specs += [vec_b, vec_b]
        args += [nxt[0], nxt[1]]
        out_specs.append(row_tile)
        out_shape.append(jax.ShapeDtypeStruct((m, d), BF16))
    vmem = (k * d * 2 + 2 * tm * k * asz * (2 if mlstm_prologue else 1)
            + 2 * tm * d * (4 + 4 + 2) + 2 * tm * d * 4)
    outs = pl.pallas_call(
        functools.partial(_mm_ln_kernel, mlstm_prologue=mlstm_prologue, emit_h=emit_h),
        grid=(m // tm,),
        in_specs=in_specs,
        out_specs=out_specs,
        out_shape=out_shape,
        compiler_params=_params(("arbitrary",), vmem),
        name=name,
    )(*args)
    return (outs[0], outs[1]) if emit_h else (outs[0], None)


def _rope_kernel(pos_ref, freq_ref, cos_ref, sin_ref):
    ang = pos_ref[0].astype(F32) * freq_ref[...]
    lane = lax.broadcasted_iota(jnp.int32, ang.shape, 1)
    first_half = (lane % ATTN_HEAD_DIM) < ATTN_HEAD_DIM // 2
    cos_ref[0] = jnp.cos(ang)
    s = jnp.sin(ang)
    sin_ref[0] = jnp.where(first_half, -s, s)


def _rope_call(positions):
    bsz, s = positions.shape
    half = ATTN_HEAD_DIM // 2
    inv_freq = 1.0 / (ROPE_THETA ** (np.arange(0, ATTN_HEAD_DIM, 2, dtype=np.float32) / ATTN_HEAD_DIM))
    freq = jnp.asarray(np.tile(inv_freq.astype(np.float32), V7X_LANES // half).reshape(1, V7X_LANES))
    tbl = jax.ShapeDtypeStruct((bsz, s, V7X_LANES), F32)
    return pl.pallas_call(
        _rope_kernel,
        grid=(bsz,),
        in_specs=[pl.BlockSpec((1, s, 1), lambda b: (b, 0, 0)), pl.BlockSpec((1, V7X_LANES), lambda b: (0, 0))],
        out_specs=[pl.BlockSpec((1, s, V7X_LANES), lambda b: (b, 0, 0))] * 2,
        out_shape=[tbl, tbl],
        compiler_params=_params(("arbitrary",), 6 * s * V7X_LANES * 4),
        name="rope_tables",
    )(positions.reshape(bsz, s, 1), freq)


def _rope(t, cos, sin_signed, first_half):
    half = ATTN_HEAD_DIM // 2
    rot = jnp.where(first_half, pltpu.roll(t, V7X_LANES - half, 1), pltpu.roll(t, half, 1))
    return t * cos + rot * sin_signed


ATTN_VT_ROWS = ATTN_HEAD_DIM + 16


def _attn_kernel(sink_ref, q_ref, k_ref, v_ref, cos_ref, sin_ref, o_ref, kp_ref, vt_ref, bias_ref, *, seq):
    blk = ATTN_BLOCK
    hd = ATTN_HEAD_DIM
    nb = seq // blk
    hp = pl.program_id(1)
    lane = lax.broadcasted_iota(jnp.int32, (seq, V7X_LANES), 1)
    low = lane < hd
    first_half = (lane % hd) < hd // 2
    cos = cos_ref[0]
    sin = sin_ref[0]

    scale = hd ** -0.5 * LOG2_E
    lane_q = lax.broadcasted_iota(jnp.int32, (blk, V7X_LANES), 1)
    first_half_q = (lane_q % hd) < hd // 2

    kr = _rope(k_ref[0], cos, sin, first_half)
    lo = jnp.where(low, kr, 0.0)
    hi = jnp.where(low, 0.0, kr)
    zeros_pad = jnp.zeros((blk, V7X_LANES), BF16)
    for idx, val in enumerate((lo, pltpu.roll(lo, hd, 1), pltpu.roll(hi, hd, 1), hi)):
        kp_ref[idx, 0:blk, :] = zeros_pad
        kp_ref[idx, blk:blk + seq, :] = val.astype(BF16)
        kp_ref[idx, blk + seq:, :] = zeros_pad

    vt = jnp.transpose(v_ref[0])
    ones_row = lax.broadcasted_iota(jnp.int32, (ATTN_VT_ROWS - hd, seq), 0) == 0
    zeros_vt = jnp.zeros((ATTN_VT_ROWS, blk), BF16)
    for j in range(2):
        vt_ref[j, :, 0:blk] = zeros_vt
        vt_ref[j, 0:hd, blk:blk + seq] = vt[j * hd:(j + 1) * hd].astype(BF16)
        vt_ref[j, hd:, blk:blk + seq] = jnp.where(ones_row, 1.0, 0.0).astype(BF16)
        vt_ref[j, :, blk + seq:] = zeros_vt

    ci = lax.broadcasted_iota(jnp.int32, (blk, 2 * blk), 0)
    ri = lax.broadcasted_iota(jnp.int32, (blk, 2 * blk), 1) % blk
    neg = jnp.full((blk, 2 * blk), -jnp.inf, F32)
    bias_ref[0] = jnp.where(ci >= ri, 0.0, -jnp.inf)
    bias_ref[1] = neg
    bias_ref[2] = jnp.where(ci <= ri, 0.0, -jnp.inf)
    bias_ref[3] = neg
    lane2 = lax.broadcasted_iota(jnp.int32, (1, 2 * blk), 1)

    def body(i, carry):
        r0 = pl.multiple_of(i * blk, blk)
        bias_prev = bias_ref[jnp.where(i == 0, 1, 0)]
        bias_next = bias_ref[jnp.where(i == nb - 1, 3, 2)]
        chains = [(j, half) for j in range(2) for half in range(2)]
        cos_q = cos_ref[0, pl.ds(r0, blk), :]
        sin_q = sin_ref[0, pl.ds(r0, blk), :]
        q_tiles = []
        for t in range(4):
            qt = q_ref[0, pl.ds(r0, blk), t * V7X_LANES:(t + 1) * V7X_LANES]
            q_tiles.append((_rope(qt, cos_q, sin_q, first_half_q) * scale).astype(BF16))
        scores = []
        for j, half in chains:
            qcat = jnp.concatenate([q_tiles[2 * j], q_tiles[2 * j + 1]], axis=0)
            kk = kp_ref[2 * j + half, pl.ds(r0, 3 * blk), :]
            scores.append(lax.dot_general(kk, qcat, _CONTRACT_LAST, preferred_element_type=F32))
        probs = []
        for (j, half), st in zip(chains, scores):
            head = hp * 8 + 4 * j + half
            snk = jnp.where(lane2 < blk, sink_ref[head], sink_ref[head + 2]) * LOG2_E
            parts = (st[0:blk] + bias_prev, st[blk:2 * blk], st[2 * blk:] + bias_next)
            m = snk
            for part in parts:
                m = jnp.maximum(m, jnp.max(part, axis=0, keepdims=True))
            p = jnp.concatenate([jnp.exp2(part - m) for part in parts], axis=0)
            probs.append((p.astype(BF16), jnp.exp2(snk - m)))
        outs = []
        for (j, half), (p, sink_p) in zip(chains, probs):
            vtw = vt_ref[j, :, pl.ds(r0, 3 * blk)]
            ot = jnp.dot(vtw, p, preferred_element_type=F32)
            outs.append(ot[0:hd] * (1.0 / (ot[hd:hd + 1] + sink_p)))
        for t in range(4):
            j, tt = divmod(t, 2)
            cols = slice(tt * blk, (tt + 1) * blk)
            tile_t = jnp.concatenate([outs[2 * j][:, cols], outs[2 * j + 1][:, cols]], axis=0)
            o_ref[0, pl.ds(r0, blk), t * V7X_LANES:(t + 1) * V7X_LANES] = jnp.transpose(tile_t).astype(o_ref.dtype)
        return carry

    lax.fori_loop(0, nb, body, 0, unroll=8)


def _attn_call(proj, cos, sin, sink, bsz, seq):
    n_pairs = ATTN_KV_HEADS // 2
    qw = 4 * V7X_LANES
    k_off = (proj.shape[-1] - 2 * ATTN_KV_HEADS * ATTN_HEAD_DIM) // V7X_LANES
    v_off = k_off + n_pairs
    assert seq // ATTN_BLOCK >= 2 and ATTN_BLOCK == V7X_LANES and 2 * ATTN_HEAD_DIM == V7X_LANES
    assert WINDOW == ATTN_BLOCK
    pad_seq = seq + 2 * ATTN_BLOCK
    tbl = pl.BlockSpec((1, seq, V7X_LANES), lambda b, p: (b, 0, 0))
    vmem = (2 * seq * (qw * 4 + 4 * V7X_LANES * 4 + qw * 2) + 4 * pad_seq * V7X_LANES * 2
            + 2 * ATTN_VT_ROWS * pad_seq * 2 + 3 * 3 * ATTN_BLOCK * 2 * ATTN_BLOCK * 4 + 6 * seq * V7X_LANES * 4)
    return pl.pallas_call(
        functools.partial(_attn_kernel, seq=seq),
        grid=(bsz, n_pairs),
        in_specs=[
            pl.BlockSpec(memory_space=pltpu.SMEM),
            pl.BlockSpec((1, seq, qw), lambda b, p: (b, 0, p)),
            pl.BlockSpec((1, seq, V7X_LANES), lambda b, p: (b, 0, k_off + p)),
            pl.BlockSpec((1, seq, V7X_LANES), lambda b, p: (b, 0, v_off + p)),
            tbl, tbl,
        ],
        out_specs=pl.BlockSpec((1, seq, qw), lambda b, p: (b, 0, p)),
        out_shape=jax.ShapeDtypeStruct((bsz, seq, n_pairs * qw), BF16),
        scratch_shapes=[
            pltpu.VMEM((4, pad_seq, V7X_LANES), BF16),
            pltpu.VMEM((2, ATTN_VT_ROWS, pad_seq), BF16),
            pltpu.VMEM((4, ATTN_BLOCK, 2 * ATTN_BLOCK), F32),
        ],
        compiler_params=_params(("arbitrary", "arbitrary"), vmem),
        name="swa_sink_attention",
    )(sink, proj, proj, proj, cos, sin)


def _gates_kernel(h_ref, w_ref, b_ref, o_ref):
    g_t = lax.dot_general(w_ref[...].astype(BF16), h_ref[...], _CONTRACT_LAST, preferred_element_type=F32)
    o_ref[0] = g_t + b_ref[...]


def _gates_call(h, w_t, layer, b_col, bsz, seq):
    m, k = h.shape
    tm = 1024
    tpb = seq // tm
    gate_blk = (w_t.shape[1] - MLSTM_GATES) // MLSTM_GATES
    return pl.pallas_call(
        _gates_kernel,
        grid=(m // tm,),
        in_specs=[
            pl.BlockSpec((tm, k), lambda i: (i, 0)),
            pl.BlockSpec((None, MLSTM_GATES, k), lambda i: (layer, gate_blk, 0)),
            pl.BlockSpec((MLSTM_GATES, 1), lambda i: (0, 0)),
        ],
        out_specs=pl.BlockSpec((1, MLSTM_GATES, tm), lambda i: (i // tpb, 0, i % tpb)),
        out_shape=jax.ShapeDtypeStruct((bsz, MLSTM_GATES, seq), F32),
        compiler_params=_params(("arbitrary",), 2 * tm * k * 2 + 4 * MLSTM_GATES * (k + tm) * 4),
        name="mlstm_gates",
    )(h, w_t, b_col)


def _log_sigmoid(x):
    return jnp.minimum(x, 0.0) - jnp.log1p(jnp.exp(-jnp.abs(x)))


def _mlstm_kernel(q_ref, k_ref, v_ref, g_ref, o_ref, qb_ref, kt_ref, vx_ref, c_ref, *, seq, chunk, dk, dv):
    nh = MLSTM_HEADS
    head = pl.program_id(1)
    nc = seq // chunk
    qb_ref[...] = (q_ref[0] * (dk ** -0.5)).astype(BF16)
    kt_ref[...] = jnp.transpose(k_ref[0])
    vx_ref[:, :dv] = v_ref[0].astype(BF16)
    ones_lane = lax.broadcasted_iota(jnp.int32, (seq, V7X_LANES), 1) == 0
    vx_ref[:, dv:] = jnp.where(ones_lane, 1.0, 0.0).astype(BF16)

    ti = lax.broadcasted_iota(jnp.int32, (chunk, chunk), 0)
    ui = lax.broadcasted_iota(jnp.int32, (chunk, chunk), 1)
    eye = ti == ui
    for direction in range(2):
        seen = (ui <= ti) if direction == 0 else (ui >= ti)
        c_ref[...] = jnp.zeros_like(c_ref)
        m_prev = jnp.full((1, 1), -1e30, F32)
        order = range(nc) if direction == 0 else range(nc - 1, -1, -1)
        for c in order:
            rows = slice(c * chunk, (c + 1) * chunk)
            gate_row = 2 * direction * nh + head
            li = g_ref[0, pl.ds(gate_row, 1), rows]
            lf = _log_sigmoid(g_ref[0, pl.ds(gate_row + nh, 1), rows])
            g_col = jnp.sum(jnp.where(seen, lf, 0.0), axis=1, keepdims=True)
            g_row = jnp.sum(jnp.where(eye, g_col, 0.0), axis=0, keepdims=True)
            g_tot = jnp.sum(lf, axis=1, keepdims=True)
            dm = jnp.where(seen, g_col - g_row + li, -jnp.inf)
            a = g_col + m_prev
            m_t = jnp.maximum(a, jnp.max(dm, axis=1, keepdims=True))
            p = jnp.exp(dm - m_t)
            ea = jnp.exp(a - m_t)

            qc = qb_ref[rows, :]
            ktc = kt_ref[:, rows]
            vxc = vx_ref[rows, :]
            sqk = jnp.dot(qc, ktc.astype(BF16), preferred_element_type=F32)
            sc = (sqk * p).astype(BF16)
            tot = (ea * jnp.dot(qc, c_ref[...].astype(BF16), preferred_element_type=F32)
                   + jnp.dot(sc, vxc, preferred_element_type=F32))
            den = tot[:, dv:dv + 1]
            hh = tot[:, :dv] * (1.0 / jnp.maximum(jnp.abs(den), jnp.exp(-m_t)))
            if direction == 0:
                o_ref[0, rows, :] = hh
            else:
                o_ref[0, rows, :] += hh

            w_log = g_tot - g_row + li
            m_new = jnp.maximum(g_tot + m_prev, jnp.max(w_log, axis=1, keepdims=True))
            decay = jnp.exp(g_tot + m_prev - m_new)
            w = jnp.exp(w_log - m_new)
            c_ref[...] = decay * c_ref[...] + jnp.dot((ktc * w).astype(BF16), vxc, preferred_element_type=F32)
            m_prev = m_new


def _mlstm_call(proj, gates_t, bsz, seq, dk, dv):
    nh = MLSTM_HEADS
    chunk = MLSTM_KERNEL_CHUNK
    k_blk = nh
    v_blk = (2 * nh * dk) // dv
    dvx = dv + V7X_LANES
    vmem = (2 * seq * (2 * dk + 2 * dv) * 4 + 2 * MLSTM_GATES * seq * 4
            + seq * dk * 2 + dk * seq * 4 + seq * dvx * 2 + dk * dvx * 4 + 8 * chunk * dvx * 4)
    return pl.pallas_call(
        functools.partial(_mlstm_kernel, seq=seq, chunk=chunk, dk=dk, dv=dv),
        grid=(bsz, nh),
        in_specs=[
            pl.BlockSpec((1, seq, dk), lambda b, h: (b, 0, h)),
            pl.BlockSpec((1, seq, dk), lambda b, h: (b, 0, k_blk + h)),
            pl.BlockSpec((1, seq, dv), lambda b, h: (b, 0, v_blk + h)),
            pl.BlockSpec((1, MLSTM_GATES, seq), lambda b, h: (b, 0, 0)),
        ],
        out_specs=pl.BlockSpec((1, seq, dv), lambda b, h: (b, 0, h)),
        out_shape=jax.ShapeDtypeStruct((bsz, seq, nh * dv), F32),
        scratch_shapes=[
            pltpu.VMEM((seq, dk), BF16),
            pltpu.VMEM((dk, seq), F32),
            pltpu.VMEM((seq, dvx), BF16),
            pltpu.VMEM((dk, dvx), F32),
        ],
        compiler_params=_params(("arbitrary", "arbitrary"), vmem),
        name="bidir_mlstm",
    )(proj, proj, proj, gates_t)


def kernel(x, c, positions, attn_w_qkv, attn_b_qkv, attn_sink, attn_w_o, attn_b_o, mlstm_w_in, mlstm_b_in,
           mlstm_norm_w, mlstm_w_o, mlstm_b_o, mod_w, mod_b, mlp_w1, mlp_b1, mlp_w2, mlp_b2,
           ln_mix_g, ln_mix_b, ln_mlp_g, ln_mlp_b):
    bsz, seq, d = x.shape
    depth = mod_w.shape[0]
    assert depth == DEPTH
    m = bsz * seq
    d_ff = mlp_w1.shape[-1]
    ml_main = mlstm_w_in.shape[-1] - MLSTM_GATES
    ml_dv = mlstm_w_o.shape[1] // MLSTM_HEADS
    ml_dk = (ml_main - 2 * MLSTM_HEADS * ml_dv) // (2 * MLSTM_HEADS)

    c_pad = jnp.pad(c, ((0, 8 - bsz), (0, 0)))
    mod = _mod_call(c_pad, mod_w, mod_b)[:, :bsz]
    mod = mod.reshape(depth, bsz, 6, 1, d)
    sh_m, sc_m, g_m, sh_f, sc_f, g_f = (mod[:, :, j] for j in range(6))

    cos, sin = _rope_call(positions)
    x2 = x.reshape(m, d)
    h = _modulate_call(x2, sc_m[0], sh_m[0], seq)

    mlstm_w_in_t = jnp.swapaxes(mlstm_w_in, 1, 2)

    for i in range(depth):
        j = i // 2
        if i % 2 == 0:
            proj, w_o_bf = _mm_call(h, attn_w_qkv, j, attn_b_qkv[j], n_cols=attn_w_qkv.shape[-1], out_dtype=F32,
                                    side=(attn_w_o, j), name="attn_qkv_proj")
            a = _attn_call(proj.reshape(bsz, seq, -1), cos, sin, attn_sink[j], bsz, seq).reshape(m, d)
            x2, h = _mm_ln_call(a, w_o_bf, attn_b_o[j], x2, g_m[i], ln_mix_g[i], ln_mix_b[i],
                                (sc_f[i], sh_f[i]), rows_per_batch=seq, tm=512, name="attn_out_ln")
        else:
            proj, w_o_bf = _mm_call(h, mlstm_w_in_t, j, mlstm_b_in[j], n_cols=ml_main, out_dtype=F32,
                                    w_transposed=True, side=(mlstm_w_o, j), name="mlstm_in_proj")
            gates_t = _gates_call(h, mlstm_w_in_t, j, mlstm_b_in[j][ml_main:].reshape(MLSTM_GATES, 1), bsz, seq)
            hs = _mlstm_call(proj.reshape(bsz, seq, ml_main), gates_t, bsz, seq, ml_dk, ml_dv).reshape(m, d)
            x2, h = _mm_ln_call(hs, w_o_bf, mlstm_b_o[j], x2, g_m[i], ln_mix_g[i], ln_mix_b[i],
                                (sc_f[i], sh_f[i]), rows_per_batch=seq, tm=512, name="mlstm_out_ln",
                                ogate=(proj, (ml_main - d) // d), normw=mlstm_norm_w[j])
        u, w2_bf = _mm_call(h, mlp_w1, i, mlp_b1[i], n_cols=d_ff, out_dtype=BF16, act="relu2", side=(mlp_w2, i),
                            name="mlp_up")
        nxt = (sc_m[i + 1], sh_m[i + 1]) if i + 1 < depth else None
        x2, h = _mm_ln_call(u, w2_bf, mlp_b2[i], x2, g_f[i], ln_mlp_g[i], ln_mlp_b[i], nxt,
                            rows_per_batch=seq, tm=256, name="mlp_down_ln")
    return x2.reshape(bsz, seq, d)
```

```python
import functools

import numpy as np
import jax
import jax.numpy as jnp
from jax import lax
from jax.experimental import pallas as pl
from jax.experimental.pallas import tpu as pltpu

F32 = jnp.float32
BF16 = jnp.bfloat16

DEPTH = 2
ATTN_HEAD_DIM = 64
ATTN_KV_HEADS = 8
WINDOW = 128
ATTN_BLOCK = 128
ROPE_THETA = 10000.0
MLSTM_HEADS = 4
MLSTM_GATES = 4 * MLSTM_HEADS
DEEPNORM_ALPHA = (2.0 * DEPTH) ** 0.25
LN_EPS = 1e-5
HEAD_NORM_EPS = 1e-6
MLSTM_KERNEL_CHUNK = 256
LOG2_E = 1.4426950408889634

V7X_LANES = 128
V7X_SUBLANES = 8
V7X_VMEM_BYTES = 64 * 1024 * 1024
VMEM_LIMIT_CAP = 60000 * 1024

MOD_COL_TILE = 2048
ELEMENTWISE_ROW_TILE = 2048
MM_ROW_TILE = 1024
MM_COL_TILE = 1024
LN_PROJ_ROW_TILE = 512
MLP_DOWN_ROW_TILE = 256
GATES_ROW_TILE = 1024
ATTN_LOOP_UNROLL = 8

ATTN_GROUP = 4
ATTN_Q_TILES = 2 * ATTN_GROUP * ATTN_HEAD_DIM // V7X_LANES
ATTN_Q_HEADS_PER_PAIR = 2 * ATTN_GROUP
VMEM_LIMIT_FLOOR = V7X_VMEM_BYTES - 28 * 1024 * 1024


def _params(semantics, vmem_bytes):
    limit = max(min(int(vmem_bytes * 1.25) + (4 << 20), VMEM_LIMIT_CAP), VMEM_LIMIT_FLOOR)
    return pltpu.CompilerParams(dimension_semantics=semantics, vmem_limit_bytes=limit)


def _mod_kernel(c_ref, w_ref, b_ref, o_ref):
    c = c_ref[...]
    c_act = (c * jax.nn.sigmoid(c)).astype(BF16)
    o_ref[0] = jnp.dot(c_act, w_ref[0].astype(BF16), preferred_element_type=F32) + b_ref[0]


def _mod_call(c_pad, mod_w, mod_b):
    depth, d, n = mod_w.shape
    rows = c_pad.shape[0]
    tn = MOD_COL_TILE
    return pl.pallas_call(
        _mod_kernel,
        grid=(depth, n // tn),
        in_specs=[
            pl.BlockSpec((rows, d), lambda i, j: (0, 0)),
            pl.BlockSpec((1, d, tn), lambda i, j: (i, 0, j)),
            pl.BlockSpec((1, 1, tn), lambda i, j: (i, 0, j)),
        ],
        out_specs=pl.BlockSpec((1, rows, tn), lambda i, j: (i, 0, j)),
        out_shape=jax.ShapeDtypeStruct((depth, rows, n), F32),
        compiler_params=_params(("arbitrary", "arbitrary"), 2 * d * tn * 4 + d * tn * 2),
        name="adaln_mod",
    )(c_pad, mod_w, mod_b.reshape(depth, 1, n))


def _modulate_kernel(x_ref, sc_ref, sh_ref, o_ref):
    o_ref[...] = (x_ref[...] * (1.0 + sc_ref[0]) + sh_ref[0]).astype(o_ref.dtype)


def _modulate_call(x2, sc, sh, rows_per_batch):
    m, d = x2.shape
    tm = ELEMENTWISE_ROW_TILE
    tpb = rows_per_batch // tm
    vec = pl.BlockSpec((1, 1, d), lambda i: (i // tpb, 0, 0))
    return pl.pallas_call(
        _modulate_kernel,
        grid=(m // tm,),
        in_specs=[pl.BlockSpec((tm, d), lambda i: (i, 0)), vec, vec],
        out_specs=pl.BlockSpec((tm, d), lambda i: (i, 0)),
        out_shape=jax.ShapeDtypeStruct((m, d), BF16),
        compiler_params=_params(("arbitrary",), 2 * tm * d * 6),
        name="modulate_in",
    )(x2, sc, sh)


_CONTRACT_LAST = (((1,), (1,)), ((), ()))


def _mm_kernel(*refs, act, w_transposed, side_cast):
    if side_cast:
        x_ref, w_ref, b_ref, side_ref, o_ref, side_o_ref, wb_ref = refs
        side_o_ref[...] = side_ref[...].astype(side_o_ref.dtype)
    else:
        x_ref, w_ref, b_ref, o_ref, wb_ref = refs

    @pl.when(pl.program_id(1) == 0)
    def _():
        wb_ref[...] = w_ref[...].astype(BF16)

    if w_transposed:
        acc = lax.dot_general(x_ref[...], wb_ref[...], _CONTRACT_LAST, preferred_element_type=F32)
    else:
        acc = jnp.dot(x_ref[...], wb_ref[...], preferred_element_type=F32)
    acc = acc + b_ref[...]
    if act == "relu2":
        acc = jnp.square(jnp.maximum(acc, 0.0))
    o_ref[...] = acc.astype(o_ref.dtype)


def _mm_call(x, w, layer, b, *, n_cols, out_dtype, act=None, w_transposed=False, side=None, name):
    m, k = x.shape
    tm, tn = MM_ROW_TILE, MM_COL_TILE
    osz = jnp.dtype(out_dtype).itemsize
    vmem = 2 * k * tn * 4 + k * tn * 2 + 2 * tm * k * 2 + 2 * tm * tn * osz + tm * tn * 4
    if w_transposed:
        w_spec = pl.BlockSpec((None, tn, k), lambda j, i: (layer, j, 0))
        w_tile = (tn, k)
    else:
        w_spec = pl.BlockSpec((None, k, tn), lambda j, i: (layer, 0, j))
        w_tile = (k, tn)
    n_i = m // tm
    grid = (n_cols // tn, n_i)
    in_specs = [pl.BlockSpec((tm, k), lambda j, i: (i, 0)), w_spec, pl.BlockSpec((1, tn), lambda j, i: (0, j))]
    args = [x, w, b.reshape(1, -1)]
    out_specs = [pl.BlockSpec((tm, tn), lambda j, i: (i, j))]
    out_shape = [jax.ShapeDtypeStruct((m, n_cols), out_dtype)]
    if side is not None:
        w2, layer2 = side
        _, k2, n2 = w2.shape
        n_blk = 1 << ((grid[0] * grid[1]).bit_length() - 1)
        rows = k2 // n_blk
        assert rows * n_blk == k2 and rows % 16 == 0
        in_specs.append(pl.BlockSpec((None, rows, n2), lambda j, i: (layer2, jnp.minimum(j * n_i + i, n_blk - 1), 0)))
        args.append(w2)
        out_specs.append(pl.BlockSpec((rows, n2), lambda j, i: (jnp.minimum(j * n_i + i, n_blk - 1), 0)))
        out_shape.append(jax.ShapeDtypeStruct((k2, n2), BF16))
        vmem += 2 * rows * n2 * 6
    outs = pl.pallas_call(
        functools.partial(_mm_kernel, act=act, w_transposed=w_transposed, side_cast=side is not None),
        grid=grid,
        in_specs=in_specs,
        out_specs=out_specs,
        out_shape=out_shape,
        scratch_shapes=[pltpu.VMEM(w_tile, BF16)],
        compiler_params=_params(("arbitrary", "arbitrary"), vmem),
        name=name,
    )(*args)
    return (outs[0], outs[1]) if side is not None else outs[0]


def _head_norm_gate(hs, og, nw, heads):
    dv = hs.shape[-1] // heads
    outs = []
    for h in range(heads):
        t = hs[:, h * dv:(h + 1) * dv]
        mu = jnp.mean(t, axis=-1, keepdims=True)
        var = jnp.mean(jnp.square(t - mu), axis=-1, keepdims=True)
        outs.append((t - mu) * lax.rsqrt(var + HEAD_NORM_EPS))
    hn = jnp.concatenate(outs, axis=-1) * nw
    return jax.nn.sigmoid(og) * hn


def _mm_ln_kernel(*refs, mlstm_prologue, emit_h):
    refs = list(refs)
    a_ref = refs.pop(0)
    if mlstm_prologue:
        og_ref = refs.pop(0)
        nw_ref = refs.pop(0)
    w_ref, b_ref, res_ref, gate_ref, lng_ref, lnb_ref = refs[:6]
    refs = refs[6:]
    if emit_h:
        scn_ref, shn_ref = refs[:2]
        refs = refs[2:]
    xo_ref = refs.pop(0)
    if emit_h:
        ho_ref = refs.pop(0)

    if mlstm_prologue:
        a = _head_norm_gate(a_ref[...], og_ref[...], nw_ref[...], MLSTM_HEADS).astype(BF16)
    else:
        a = a_ref[...]
    y = jnp.dot(a, w_ref[...], preferred_element_type=F32) + b_ref[...]
    z = DEEPNORM_ALPHA * res_ref[...] + (1.0 + gate_ref[0]) * y
    mu = jnp.mean(z, axis=-1, keepdims=True)
    var = jnp.mean(jnp.square(z - mu), axis=-1, keepdims=True)
    xn = (z - mu) * lax.rsqrt(var + LN_EPS) * lng_ref[...] + lnb_ref[...]
    xo_ref[...] = xn
    if emit_h:
        ho_ref[...] = (xn * (1.0 + scn_ref[0]) + shn_ref[0]).astype(ho_ref.dtype)


def _mm_ln_call(a, w_bf, b, res, gate, lng, lnb, nxt, *, rows_per_batch, tm, name, ogate=None, normw=None):
    m = a.shape[0]
    k, d = w_bf.shape
    tpb = rows_per_batch // tm
    mlstm_prologue = ogate is not None
    emit_h = nxt is not None
    vec_b = pl.BlockSpec((1, 1, d), lambda i: (i // tpb, 0, 0))
    vec = pl.BlockSpec((1, d), lambda i: (0, 0))
    row_tile = pl.BlockSpec((tm, d), lambda i: (i, 0))

    in_specs = [pl.BlockSpec((tm, k), lambda i: (i, 0))]
    args = [a]
    asz = a.dtype.itemsize
    if mlstm_prologue:
        ocb = ogate[1]
        in_specs += [pl.BlockSpec((tm, k), lambda i: (i, ocb)), pl.BlockSpec((1, k), lambda i: (0, 0))]
        args += [ogate[0], normw.reshape(1, -1)]
    in_specs += [pl.BlockSpec((k, d), lambda i: (0, 0), pipeline_mode=pl.Buffered(1)),
                 vec, row_tile, vec_b, vec, vec]
    args += [w_bf, b.reshape(1, d), res, gate, lng.reshape(1, d), lnb.reshape(1, d)]
    out_specs = [row_tile]
    out_shape = [jax.ShapeDtypeStruct((m, d), F32)]
    if emit_h:
        in_specs += [vec_b, vec_b]
        args += [nxt[0], nxt[1]]
        out_specs.append(row_tile)
        out_shape.append(jax.ShapeDtypeStruct((m, d), BF16))
    vmem = (k * d * 2 + 2 * tm * k * asz * (2 if mlstm_prologue else 1)
            + 2 * tm * d * (4 + 4 + 2) + 2 * tm * d * 4)
    outs = pl.pallas_call(
        functools.partial(_mm_ln_kernel, mlstm_prologue=mlstm_prologue, emit_h=emit_h),
        grid=(m // tm,),
        in_specs=in_specs,
        out_specs=out_specs,
        out_shape=out_shape,
        compiler_params=_params(("arbitrary",), vmem),
        name=name,
    )(*args)
    return (outs[0], outs[1]) if emit_h else (outs[0], None)


def _rope_kernel(pos_ref, freq_ref, cos_ref, sin_ref):
    ang = pos_ref[0].astype(F32) * freq_ref[...]
    lane = lax.broadcasted_iota(jnp.int32, ang.shape, 1)
    first_half = (lane % ATTN_HEAD_DIM) < ATTN_HEAD_DIM // 2
    cos_ref[0] = jnp.cos(ang)
    s = jnp.sin(ang)
    sin_ref[0] = jnp.where(first_half, -s, s)


def _rope_call(positions):
    bsz, s = positions.shape
    half = ATTN_HEAD_DIM // 2
    inv_freq = 1.0 / (ROPE_THETA ** (np.arange(0, ATTN_HEAD_DIM, 2, dtype=np.float32) / ATTN_HEAD_DIM))
    freq = jnp.asarray(np.tile(inv_freq.astype(np.float32), V7X_LANES // half).reshape(1, V7X_LANES))
    tbl = jax.ShapeDtypeStruct((bsz, s, V7X_LANES), F32)
    return pl.pallas_call(
        _rope_kernel,
        grid=(bsz,),
        in_specs=[pl.BlockSpec((1, s, 1), lambda b: (b, 0, 0)), pl.BlockSpec((1, V7X_LANES), lambda b: (0, 0))],
        out_specs=[pl.BlockSpec((1, s, V7X_LANES), lambda b: (b, 0, 0))] * 2,
        out_shape=[tbl, tbl],
        compiler_params=_params(("arbitrary",), 6 * s * V7X_LANES * 4),
        name="rope_tables",
    )(positions.reshape(bsz, s, 1), freq)


def _rope(t, cos, sin_signed, first_half):
    half = ATTN_HEAD_DIM // 2
    rot = jnp.where(first_half, pltpu.roll(t, V7X_LANES - half, 1), pltpu.roll(t, half, 1))
    return t * cos + rot * sin_signed


ATTN_VT_ROWS = ATTN_HEAD_DIM + 16


def _attn_kernel(sink_ref, q_ref, k_ref, v_ref, cos_ref, sin_ref, o_ref, kp_ref, vt_ref, bias_ref, *, seq):
    blk = ATTN_BLOCK
    hd = ATTN_HEAD_DIM
    nb = seq // blk
    hp = pl.program_id(1)
    lane = lax.broadcasted_iota(jnp.int32, (seq, V7X_LANES), 1)
    low = lane < hd
    first_half = (lane % hd) < hd // 2
    cos = cos_ref[0]
    sin = sin_ref[0]

    scale = hd ** -0.5 * LOG2_E
    lane_q = lax.broadcasted_iota(jnp.int32, (blk, V7X_LANES), 1)
    first_half_q = (lane_q % hd) < hd // 2

    kr = _rope(k_ref[0], cos, sin, first_half)
    lo = jnp.where(low, kr, 0.0)
    hi = jnp.where(low, 0.0, kr)
    zeros_pad = jnp.zeros((blk, V7X_LANES), BF16)
    for idx, val in enumerate((lo, pltpu.roll(lo, hd, 1), pltpu.roll(hi, hd, 1), hi)):
        kp_ref[idx, 0:blk, :] = zeros_pad
        kp_ref[idx, blk:blk + seq, :] = val.astype(BF16)
        kp_ref[idx, blk + seq:, :] = zeros_pad

    vt = jnp.transpose(v_ref[0])
    ones_row = lax.broadcasted_iota(jnp.int32, (ATTN_VT_ROWS - hd, seq), 0) == 0
    zeros_vt = jnp.zeros((ATTN_VT_ROWS, blk), BF16)
    for j in range(2):
        vt_ref[j, :, 0:blk] = zeros_vt
        vt_ref[j, 0:hd, blk:blk + seq] = vt[j * hd:(j + 1) * hd].astype(BF16)
        vt_ref[j, hd:, blk:blk + seq] = jnp.where(ones_row, 1.0, 0.0).astype(BF16)
        vt_ref[j, :, blk + seq:] = zeros_vt

    ci = lax.broadcasted_iota(jnp.int32, (blk, 2 * blk), 0)
    ri = lax.broadcasted_iota(jnp.int32, (blk, 2 * blk), 1) % blk
    neg = jnp.full((blk, 2 * blk), -jnp.inf, F32)
    bias_ref[0] = jnp.where(ci >= ri, 0.0, -jnp.inf)
    bias_ref[1] = neg
    bias_ref[2] = jnp.where(ci <= ri, 0.0, -jnp.inf)
    bias_ref[3] = neg
    lane2 = lax.broadcasted_iota(jnp.int32, (1, 2 * blk), 1)

    def body(i, carry):
        r0 = pl.multiple_of(i * blk, blk)
        bias_prev = bias_ref[jnp.where(i == 0, 1, 0)]
        bias_next = bias_ref[jnp.where(i == nb - 1, 3, 2)]
        chains = [(j, half) for j in range(2) for half in range(2)]
        cos_q = cos_ref[0, pl.ds(r0, blk), :]
        sin_q = sin_ref[0, pl.ds(r0, blk), :]
        q_tiles = []
        for t in range(ATTN_Q_TILES):
            qt = q_ref[0, pl.ds(r0, blk), t * V7X_LANES:(t + 1) * V7X_LANES]
            q_tiles.append((_rope(qt, cos_q, sin_q, first_half_q) * scale).astype(BF16))
        scores = []
        for j, half in chains:
            qcat = jnp.concatenate([q_tiles[2 * j], q_tiles[2 * j + 1]], axis=0)
            kk = kp_ref[2 * j + half, pl.ds(r0, 3 * blk), :]
            scores.append(lax.dot_general(kk, qcat, _CONTRACT_LAST, preferred_element_type=F32))
        probs = []
        for (j, half), st in zip(chains, scores):
            head = hp * ATTN_Q_HEADS_PER_PAIR + ATTN_GROUP * j + half
            snk = jnp.where(lane2 < blk, sink_ref[head], sink_ref[head + 2]) * LOG2_E
            parts = (st[0:blk] + bias_prev, st[blk:2 * blk], st[2 * blk:] + bias_next)
            m = snk
            for part in parts:
                m = jnp.maximum(m, jnp.max(part, axis=0, keepdims=True))
            p = jnp.concatenate([jnp.exp2(part - m) for part in parts], axis=0)
            probs.append((p.astype(BF16), jnp.exp2(snk - m)))
        outs = []
        for (j, half), (p, sink_p) in zip(chains, probs):
            vtw = vt_ref[j, :, pl.ds(r0, 3 * blk)]
            ot = jnp.dot(vtw, p, preferred_element_type=F32)
            outs.append(ot[0:hd] * (1.0 / (ot[hd:hd + 1] + sink_p)))
        for t in range(ATTN_Q_TILES):
            j, tt = divmod(t, 2)
            cols = slice(tt * blk, (tt + 1) * blk)
            tile_t = jnp.concatenate([outs[2 * j][:, cols], outs[2 * j + 1][:, cols]], axis=0)
            o_ref[0, pl.ds(r0, blk), t * V7X_LANES:(t + 1) * V7X_LANES] = jnp.transpose(tile_t).astype(o_ref.dtype)
        return carry

    lax.fori_loop(0, nb, body, 0, unroll=ATTN_LOOP_UNROLL)


def _attn_call(proj, cos, sin, sink, bsz, seq):
    n_pairs = ATTN_KV_HEADS // 2
    qw = ATTN_Q_TILES * V7X_LANES
    k_off = (proj.shape[-1] - 2 * ATTN_KV_HEADS * ATTN_HEAD_DIM) // V7X_LANES
    v_off = k_off + n_pairs
    assert seq // ATTN_BLOCK >= 2 and ATTN_BLOCK == V7X_LANES and 2 * ATTN_HEAD_DIM == V7X_LANES
    assert WINDOW == ATTN_BLOCK
    pad_seq = seq + 2 * ATTN_BLOCK
    tbl = pl.BlockSpec((1, seq, V7X_LANES), lambda b, p: (b, 0, 0))
    vmem = (2 * seq * (qw * 4 + 4 * V7X_LANES * 4 + qw * 2) + 4 * pad_seq * V7X_LANES * 2
            + 2 * ATTN_VT_ROWS * pad_seq * 2 + 3 * 3 * ATTN_BLOCK * 2 * ATTN_BLOCK * 4 + 6 * seq * V7X_LANES * 4)
    return pl.pallas_call(
        functools.partial(_attn_kernel, seq=seq),
        grid=(bsz, n_pairs),
        in_specs=[
            pl.BlockSpec(memory_space=pltpu.SMEM),
            pl.BlockSpec((1, seq, qw), lambda b, p: (b, 0, p)),
            pl.BlockSpec((1, seq, V7X_LANES), lambda b, p: (b, 0, k_off + p)),
            pl.BlockSpec((1, seq, V7X_LANES), lambda b, p: (b, 0, v_off + p)),
            tbl, tbl,
        ],
        out_specs=pl.BlockSpec((1, seq, qw), lambda b, p: (b, 0, p)),
        out_shape=jax.ShapeDtypeStruct((bsz, seq, n_pairs * qw), BF16),
        scratch_shapes=[
            pltpu.VMEM((4, pad_seq, V7X_LANES), BF16),
            pltpu.VMEM((2, ATTN_VT_ROWS, pad_seq), BF16),
            pltpu.VMEM((4, ATTN_BLOCK, 2 * ATTN_BLOCK), F32),
        ],
        compiler_params=_params(("arbitrary", "arbitrary"), vmem),
        name="swa_sink_attention",
    )(sink, proj, proj, proj, cos, sin)


def _gates_kernel(h_ref, w_ref, b_ref, o_ref):
    g_t = lax.dot_general(w_ref[...].astype(BF16), h_ref[...], _CONTRACT_LAST, preferred_element_type=F32)
    o_ref[0] = g_t + b_ref[...]


def _gates_call(h, w_t, layer, b_col, bsz, seq):
    m, k = h.shape
    tm = GATES_ROW_TILE
    tpb = seq // tm
    gate_blk = (w_t.shape[1] - MLSTM_GATES) // MLSTM_GATES
    return pl.pallas_call(
        _gates_kernel,
        grid=(m // tm,),
        in_specs=[
            pl.BlockSpec((tm, k), lambda i: (i, 0)),
            pl.BlockSpec((None, MLSTM_GATES, k), lambda i: (layer, gate_blk, 0)),
            pl.BlockSpec((MLSTM_GATES, 1), lambda i: (0, 0)),
        ],
        out_specs=pl.BlockSpec((1, MLSTM_GATES, tm), lambda i: (i // tpb, 0, i % tpb)),
        out_shape=jax.ShapeDtypeStruct((bsz, MLSTM_GATES, seq), F32),
        compiler_params=_params(("arbitrary",), 2 * tm * k * 2 + 4 * MLSTM_GATES * (k + tm) * 4),
        name="mlstm_gates",
    )(h, w_t, b_col)


def _log_sigmoid(x):
    return jnp.minimum(x, 0.0) - jnp.log1p(jnp.exp(-jnp.abs(x)))


def _mlstm_kernel(q_ref, k_ref, v_ref, g_ref, o_ref, qb_ref, kt_ref, vx_ref, c_ref, *, seq, chunk, dk, dv):
    nh = MLSTM_HEADS
    head = pl.program_id(1)
    nc = seq // chunk
    qb_ref[...] = (q_ref[0] * (dk ** -0.5)).astype(BF16)
    kt_ref[...] = jnp.transpose(k_ref[0])
    vx_ref[:, :dv] = v_ref[0].astype(BF16)
    ones_lane = lax.broadcasted_iota(jnp.int32, (seq, V7X_LANES), 1) == 0
    vx_ref[:, dv:] = jnp.where(ones_lane, 1.0, 0.0).astype(BF16)

    ti = lax.broadcasted_iota(jnp.int32, (chunk, chunk), 0)
    ui = lax.broadcasted_iota(jnp.int32, (chunk, chunk), 1)
    eye = ti == ui
    for direction in range(2):
        seen = (ui <= ti) if direction == 0 else (ui >= ti)
        c_ref[...] = jnp.zeros_like(c_ref)
        m_prev = jnp.full((1, 1), -1e30, F32)
        order = range(nc) if direction == 0 else range(nc - 1, -1, -1)
        for c in order:
            rows = slice(c * chunk, (c + 1) * chunk)
            gate_row = 2 * direction * nh + head
            li = g_ref[0, pl.ds(gate_row, 1), rows]
            lf = _log_sigmoid(g_ref[0, pl.ds(gate_row + nh, 1), rows])
            g_col = jnp.sum(jnp.where(seen, lf, 0.0), axis=1, keepdims=True)
            g_row = jnp.sum(jnp.where(eye, g_col, 0.0), axis=0, keepdims=True)
            g_tot = jnp.sum(lf, axis=1, keepdims=True)
            dm = jnp.where(seen, g_col - g_row + li, -jnp.inf)
            a = g_col + m_prev
            m_t = jnp.maximum(a, jnp.max(dm, axis=1, keepdims=True))
            p = jnp.exp(dm - m_t)
            ea = jnp.exp(a - m_t)

            qc = qb_ref[rows, :]
            ktc = kt_ref[:, rows]
            vxc = vx_ref[rows, :]
            sqk = jnp.dot(qc, ktc.astype(BF16), preferred_element_type=F32)
            sc = (sqk * p).astype(BF16)
            tot = (ea * jnp.dot(qc, c_ref[...].astype(BF16), preferred_element_type=F32)
                   + jnp.dot(sc, vxc, preferred_element_type=F32))
            den = tot[:, dv:dv + 1]
            hh = tot[:, :dv] * (1.0 / jnp.maximum(jnp.abs(den), jnp.exp(-m_t)))
            if direction == 0:
                o_ref[0, rows, :] = hh
            else:
                o_ref[0, rows, :] += hh

            w_log = g_tot - g_row + li
            m_new = jnp.maximum(g_tot + m_prev, jnp.max(w_log, axis=1, keepdims=True))
            decay = jnp.exp(g_tot + m_prev - m_new)
            w = jnp.exp(w_log - m_new)
            c_ref[...] = decay * c_ref[...] + jnp.dot((ktc * w).astype(BF16), vxc, preferred_element_type=F32)
            m_prev = m_new


def _mlstm_call(proj, gates_t, bsz, seq, dk, dv):
    nh = MLSTM_HEADS
    chunk = MLSTM_KERNEL_CHUNK
    k_blk = nh
    v_blk = (2 * nh * dk) // dv
    dvx = dv + V7X_LANES
    vmem = (2 * seq * (2 * dk + 2 * dv) * 4 + 2 * MLSTM_GATES * seq * 4
            + seq * dk * 2 + dk * seq * 4 + seq * dvx * 2 + dk * dvx * 4 + 8 * chunk * dvx * 4)
    return pl.pallas_call(
        functools.partial(_mlstm_kernel, seq=seq, chunk=chunk, dk=dk, dv=dv),
        grid=(bsz, nh),
        in_specs=[
            pl.BlockSpec((1, seq, dk), lambda b, h: (b, 0, h)),
            pl.BlockSpec((1, seq, dk), lambda b, h: (b, 0, k_blk + h)),
            pl.BlockSpec((1, seq, dv), lambda b, h: (b, 0, v_blk + h)),
            pl.BlockSpec((1, MLSTM_GATES, seq), lambda b, h: (b, 0, 0)),
        ],
        out_specs=pl.BlockSpec((1, seq, dv), lambda b, h: (b, 0, h)),
        out_shape=jax.ShapeDtypeStruct((bsz, seq, nh * dv), F32),
        scratch_shapes=[
            pltpu.VMEM((seq, dk), BF16),
            pltpu.VMEM((dk, seq), F32),
            pltpu.VMEM((seq, dvx), BF16),
            pltpu.VMEM((dk, dvx), F32),
        ],
        compiler_params=_params(("arbitrary", "arbitrary"), vmem),
        name="bidir_mlstm",
    )(proj, proj, proj, gates_t)


def kernel(x, c, positions, attn_w_qkv, attn_b_qkv, attn_sink, attn_w_o, attn_b_o, mlstm_w_in, mlstm_b_in,
           mlstm_norm_w, mlstm_w_o, mlstm_b_o, mod_w, mod_b, mlp_w1, mlp_b1, mlp_w2, mlp_b2,
           ln_mix_g, ln_mix_b, ln_mlp_g, ln_mlp_b):
    bsz, seq, d = x.shape
    depth = mod_w.shape[0]
    assert depth == DEPTH
    m = bsz * seq
    d_ff = mlp_w1.shape[-1]
    ml_main = mlstm_w_in.shape[-1] - MLSTM_GATES
    ml_dv = mlstm_w_o.shape[1] // MLSTM_HEADS
    ml_dk = (ml_main - 2 * MLSTM_HEADS * ml_dv) // (2 * MLSTM_HEADS)

    c_pad = jnp.pad(c, ((0, V7X_SUBLANES - bsz), (0, 0)))
    mod = _mod_call(c_pad, mod_w, mod_b)[:, :bsz]
    mod = mod.reshape(depth, bsz, 6, 1, d)
    sh_m, sc_m, g_m, sh_f, sc_f, g_f = (mod[:, :, j] for j in range(6))

    cos, sin = _rope_call(positions)
    x2 = x.reshape(m, d)
    h = _modulate_call(x2, sc_m[0], sh_m[0], seq)

    mlstm_w_in_t = jnp.swapaxes(mlstm_w_in, 1, 2)

    for i in range(depth):
        j = i // 2
        if i % 2 == 0:
            proj, w_o_bf = _mm_call(h, attn_w_qkv, j, attn_b_qkv[j], n_cols=attn_w_qkv.shape[-1], out_dtype=F32,
                                    side=(attn_w_o, j), name="attn_qkv_proj")
            a = _attn_call(proj.reshape(bsz, seq, -1), cos, sin, attn_sink[j], bsz, seq).reshape(m, d)
            x2, h = _mm_ln_call(a, w_o_bf, attn_b_o[j], x2, g_m[i], ln_mix_g[i], ln_mix_b[i],
                                (sc_f[i], sh_f[i]), rows_per_batch=seq, tm=LN_PROJ_ROW_TILE, name="attn_out_ln")
        else:
            proj, w_o_bf = _mm_call(h, mlstm_w_in_t, j, mlstm_b_in[j], n_cols=ml_main, out_dtype=F32,
                                    w_transposed=True, side=(mlstm_w_o, j), name="mlstm_in_proj")
            gates_t = _gates_call(h, mlstm_w_in_t, j, mlstm_b_in[j][ml_main:].reshape(MLSTM_GATES, 1), bsz, seq)
            hs = _mlstm_call(proj.reshape(bsz, seq, ml_main), gates_t, bsz, seq, ml_dk, ml_dv).reshape(m, d)
            x2, h = _mm_ln_call(hs, w_o_bf, mlstm_b_o[j], x2, g_m[i], ln_mix_g[i], ln_mix_b[i],
                                (sc_f[i], sh_f[i]), rows_per_batch=seq, tm=LN_PROJ_ROW_TILE, name="mlstm_out_ln",
                                ogate=(proj, (ml_main - d) // d), normw=mlstm_norm_w[j])
        u, w2_bf = _mm_call(h, mlp_w1, i, mlp_b1[i], n_cols=d_ff, out_dtype=BF16, act="relu2", side=(mlp_w2, i),
                            name="mlp_up")
        nxt = (sc_m[i + 1], sh_m[i + 1]) if i + 1 < depth else None
        x2, h = _mm_ln_call(u, w2_bf, mlp_b2[i], x2, g_f[i], ln_mlp_g[i], ln_mlp_b[i], nxt,
                            rows_per_batch=seq, tm=MLP_DOWN_ROW_TILE, name="mlp_down_ln")
    return x2.reshape(bsz, seq, d)
```

```python
import functools

import numpy as np
import jax
import jax.numpy as jnp
from jax import lax
from jax.experimental import pallas as pl
from jax.experimental.pallas import tpu as pltpu

F32 = jnp.float32
BF16 = jnp.bfloat16

DEPTH = 2
ATTN_HEAD_DIM = 64
ATTN_KV_HEADS = 8
WINDOW = 128
ATTN_BLOCK = 128
ROPE_THETA = 10000.0
MLSTM_HEADS = 4
MLSTM_GATES = 4 * MLSTM_HEADS
DEEPNORM_ALPHA = (2.0 * DEPTH) ** 0.25
LN_EPS = 1e-5
HEAD_NORM_EPS = 1e-6
MLSTM_KERNEL_CHUNK = 256
LOG2_E = 1.4426950408889634

V7X_LANES = 128
V7X_SUBLANES = 8
V7X_VMEM_BYTES = 64 * 1024 * 1024
VMEM_LIMIT_CAP = 60000 * 1024

MOD_COL_TILE = 1024
ELEMENTWISE_ROW_TILE = 2048
MM_ROW_TILE = 1024
MM_COL_TILE = 1024
LN_PROJ_ROW_TILE = 512
MLP_DOWN_ROW_TILE = 256
GATES_ROW_TILE = 1024
ATTN_LOOP_UNROLL = 8

ATTN_GROUP = 4
ATTN_Q_TILES = 2 * ATTN_GROUP * ATTN_HEAD_DIM // V7X_LANES
ATTN_Q_HEADS_PER_PAIR = 2 * ATTN_GROUP
VMEM_LIMIT_FLOOR = V7X_VMEM_BYTES - 28 * 1024 * 1024


def _params(semantics, vmem_bytes):
    limit = max(min(int(vmem_bytes * 1.25) + (4 << 20), VMEM_LIMIT_CAP), VMEM_LIMIT_FLOOR)
    return pltpu.CompilerParams(dimension_semantics=semantics, vmem_limit_bytes=limit)


def _mod_kernel(c_ref, w_ref, b_ref, o_ref):
    c = c_ref[...]
    c_act = (c * jax.nn.sigmoid(c)).astype(BF16)
    o_ref[0] = jnp.dot(c_act, w_ref[0].astype(BF16), preferred_element_type=F32) + b_ref[0]


def _mod_call(c_pad, mod_w, mod_b):
    depth, d, n = mod_w.shape
    rows = c_pad.shape[0]
    tn = MOD_COL_TILE
    return pl.pallas_call(
        _mod_kernel,
        grid=(depth, n // tn),
        in_specs=[
            pl.BlockSpec((rows, d), lambda i, j: (0, 0)),
            pl.BlockSpec((1, d, tn), lambda i, j: (i, 0, j)),
            pl.BlockSpec((1, 1, tn), lambda i, j: (i, 0, j)),
        ],
        out_specs=pl.BlockSpec((1, rows, tn), lambda i, j: (i, 0, j)),
        out_shape=jax.ShapeDtypeStruct((depth, rows, n), F32),
        compiler_params=_params(("arbitrary", "arbitrary"), 2 * d * tn * 4 + d * tn * 2),
        name="adaln_mod",
    )(c_pad, mod_w, mod_b.reshape(depth, 1, n))


def _modulate_kernel(x_ref, sc_ref, sh_ref, o_ref):
    o_ref[...] = (x_ref[...] * (1.0 + sc_ref[0]) + sh_ref[0]).astype(o_ref.dtype)


def _modulate_call(x2, sc, sh, rows_per_batch):
    m, d = x2.shape
    tm = ELEMENTWISE_ROW_TILE
    tpb = rows_per_batch // tm
    vec = pl.BlockSpec((1, 1, d), lambda i: (i // tpb, 0, 0))
    return pl.pallas_call(
        _modulate_kernel,
        grid=(m // tm,),
        in_specs=[pl.BlockSpec((tm, d), lambda i: (i, 0)), vec, vec],
        out_specs=pl.BlockSpec((tm, d), lambda i: (i, 0)),
        out_shape=jax.ShapeDtypeStruct((m, d), BF16),
        compiler_params=_params(("arbitrary",), 2 * tm * d * 6),
        name="modulate_in",
    )(x2, sc, sh)


_CONTRACT_LAST = (((1,), (1,)), ((), ()))


def _mm_kernel(*refs, act, w_transposed, side_cast):
    if side_cast:
        x_ref, w_ref, b_ref, side_ref, o_ref, side_o_ref, wb_ref = refs
        side_o_ref[...] = side_ref[...].astype(side_o_ref.dtype)
    else:
        x_ref, w_ref, b_ref, o_ref, wb_ref = refs

    @pl.when(pl.program_id(1) == 0)
    def _():
        wb_ref[...] = w_ref[...].astype(BF16)

    if w_transposed:
        acc = lax.dot_general(x_ref[...], wb_ref[...], _CONTRACT_LAST, preferred_element_type=F32)
    else:
        acc = jnp.dot(x_ref[...], wb_ref[...], preferred_element_type=F32)
    acc = acc + b_ref[...]
    if act == "relu2":
        acc = jnp.square(jnp.maximum(acc, 0.0))
    o_ref[...] = acc.astype(o_ref.dtype)


def _mm_call(x, w, layer, b, *, n_cols, out_dtype, act=None, w_transposed=False, side=None, name):
    m, k = x.shape
    tm, tn = MM_ROW_TILE, MM_COL_TILE
    osz = jnp.dtype(out_dtype).itemsize
    vmem = 2 * k * tn * 4 + k * tn * 2 + 2 * tm * k * 2 + 2 * tm * tn * osz + tm * tn * 4
    if w_transposed:
        w_spec = pl.BlockSpec((None, tn, k), lambda j, i: (layer, j, 0))
        w_tile = (tn, k)
    else:
        w_spec = pl.BlockSpec((None, k, tn), lambda j, i: (layer, 0, j))
        w_tile = (k, tn)
    n_i = m // tm
    grid = (n_cols // tn, n_i)
    in_specs = [pl.BlockSpec((tm, k), lambda j, i: (i, 0)), w_spec, pl.BlockSpec((1, tn), lambda j, i: (0, j))]
    args = [x, w, b.reshape(1, -1)]
    out_specs = [pl.BlockSpec((tm, tn), lambda j, i: (i, j))]
    out_shape = [jax.ShapeDtypeStruct((m, n_cols), out_dtype)]
    if side is not None:
        w2, layer2 = side
        _, k2, n2 = w2.shape
        n_blk = 1 << ((grid[0] * grid[1]).bit_length() - 1)
        rows = k2 // n_blk
        assert rows * n_blk == k2 and rows % 16 == 0
        in_specs.append(pl.BlockSpec((None, rows, n2), lambda j, i: (layer2, jnp.minimum(j * n_i + i, n_blk - 1), 0)))
        args.append(w2)
        out_specs.append(pl.BlockSpec((rows, n2), lambda j, i: (jnp.minimum(j * n_i + i, n_blk - 1), 0)))
        out_shape.append(jax.ShapeDtypeStruct((k2, n2), BF16))
        vmem += 2 * rows * n2 * 6
    outs = pl.pallas_call(
        functools.partial(_mm_kernel, act=act, w_transposed=w_transposed, side_cast=side is not None),
        grid=grid,
        in_specs=in_specs,
        out_specs=out_specs,
        out_shape=out_shape,
        scratch_shapes=[pltpu.VMEM(w_tile, BF16)],
        compiler_params=_params(("arbitrary", "arbitrary"), vmem),
        name=name,
    )(*args)
    return (outs[0], outs[1]) if side is not None else outs[0]


def _head_norm_gate(hs, og, nw, heads):
    dv = hs.shape[-1] // heads
    outs = []
    for h in range(heads):
        t = hs[:, h * dv:(h + 1) * dv]
        mu = jnp.mean(t, axis=-1, keepdims=True)
        var = jnp.mean(jnp.square(t - mu), axis=-1, keepdims=True)
        outs.append((t - mu) * lax.rsqrt(var + HEAD_NORM_EPS))
    hn = jnp.concatenate(outs, axis=-1) * nw
    return jax.nn.sigmoid(og) * hn


def _mm_ln_kernel(*refs, mlstm_prologue, emit_h):
    refs = list(refs)
    a_ref = refs.pop(0)
    if mlstm_prologue:
        og_ref = refs.pop(0)
        nw_ref = refs.pop(0)
    w_ref, b_ref, res_ref, gate_ref, lng_ref, lnb_ref = refs[:6]
    refs = refs[6:]
    if emit_h:
        scn_ref, shn_ref = refs[:2]
        refs = refs[2:]
    xo_ref = refs.pop(0)
    if emit_h:
        ho_ref = refs.pop(0)

    if mlstm_prologue:
        a = _head_norm_gate(a_ref[...], og_ref[...], nw_ref[...], MLSTM_HEADS).astype(BF16)
    else:
        a = a_ref[...]
    y = jnp.dot(a, w_ref[...], preferred_element_type=F32) + b_ref[...]
    z = DEEPNORM_ALPHA * res_ref[...] + (1.0 + gate_ref[0]) * y
    mu = jnp.mean(z, axis=-1, keepdims=True)
    var = jnp.mean(jnp.square(z - mu), axis=-1, keepdims=True)
    xn = (z - mu) * lax.rsqrt(var + LN_EPS) * lng_ref[...] + lnb_ref[...]
    xo_ref[...] = xn
    if emit_h:
        ho_ref[...] = (xn * (1.0 + scn_ref[0]) + shn_ref[0]).astype(ho_ref.dtype)


def _mm_ln_call(a, w_bf, b, res, gate, lng, lnb, nxt, *, rows_per_batch, tm, name, ogate=None, normw=None):
    m = a.shape[0]
    k, d = w_bf.shape
    tpb = rows_per_batch // tm
    mlstm_prologue = ogate is not None
    emit_h = nxt is not None
    vec_b = pl.BlockSpec((1, 1, d), lambda i: (i // tpb, 0, 0))
    vec = pl.BlockSpec((1, d), lambda i: (0, 0))
    row_tile = pl.BlockSpec((tm, d), lambda i: (i, 0))

    in_specs = [pl.BlockSpec((tm, k), lambda i: (i, 0))]
    args = [a]
    asz = a.dtype.itemsize
    if mlstm_prologue:
        ocb = ogate[1]
        in_specs += [pl.BlockSpec((tm, k), lambda i: (i, ocb)), pl.BlockSpec((1, k), lambda i: (0, 0))]
        args += [ogate[0], normw.reshape(1, -1)]
    in_specs += [pl.BlockSpec((k, d), lambda i: (0, 0), pipeline_mode=pl.Buffered(1)),
                 vec, row_tile, vec_b, vec, vec]
    args += [w_bf, b.reshape(1, d), res, gate, lng.reshape(1, d), lnb.reshape(1, d)]
    out_specs = [row_tile]
    out_shape = [jax.ShapeDtypeStruct((m, d), F32)]
    if emit_h:
        in_specs += [vec_b, vec_b]
        args += [nxt[0], nxt[1]]
        out_specs.append(row_tile)
        out_shape.append(jax.ShapeDtypeStruct((m, d), BF16))
    vmem = (k * d * 2 + 2 * tm * k * asz * (2 if mlstm_prologue else 1)
            + 2 * tm * d * (4 + 4 + 2) + 2 * tm * d * 4)
    outs = pl.pallas_call(
        functools.partial(_mm_ln_kernel, mlstm_prologue=mlstm_prologue, emit_h=emit_h),
        grid=(m // tm,),
        in_specs=in_specs,
        out_specs=out_specs,
        out_shape=out_shape,
        compiler_params=_params(("arbitrary",), vmem),
        name=name,
    )(*args)
    return (outs[0], outs[1]) if emit_h else (outs[0], None)


def _rope_kernel(pos_ref, freq_ref, cos_ref, sin_ref):
    ang = pos_ref[0].astype(F32) * freq_ref[...]
    lane = lax.broadcasted_iota(jnp.int32, ang.shape, 1)
    first_half = (lane % ATTN_HEAD_DIM) < ATTN_HEAD_DIM // 2
    cos_ref[0] = jnp.cos(ang)
    s = jnp.sin(ang)
    sin_ref[0] = jnp.where(first_half, -s, s)


def _rope_call(positions):
    bsz, s = positions.shape
    half = ATTN_HEAD_DIM // 2
    inv_freq = 1.0 / (ROPE_THETA ** (np.arange(0, ATTN_HEAD_DIM, 2, dtype=np.float32) / ATTN_HEAD_DIM))
    freq = jnp.asarray(np.tile(inv_freq.astype(np.float32), V7X_LANES // half).reshape(1, V7X_LANES))
    tbl = jax.ShapeDtypeStruct((bsz, s, V7X_LANES), F32)
    return pl.pallas_call(
        _rope_kernel,
        grid=(bsz,),
        in_specs=[pl.BlockSpec((1, s, 1), lambda b: (b, 0, 0)), pl.BlockSpec((1, V7X_LANES), lambda b: (0, 0))],
        out_specs=[pl.BlockSpec((1, s, V7X_LANES), lambda b: (b, 0, 0))] * 2,
        out_shape=[tbl, tbl],
        compiler_params=_params(("arbitrary",), 6 * s * V7X_LANES * 4),
        name="rope_tables",
    )(positions.reshape(bsz, s, 1), freq)


def _rope(t, cos, sin_signed, first_half):
    half = ATTN_HEAD_DIM // 2
    rot = jnp.where(first_half, pltpu.roll(t, V7X_LANES - half, 1), pltpu.roll(t, half, 1))
    return t * cos + rot * sin_signed


ATTN_VT_ROWS = ATTN_HEAD_DIM + 16


def _attn_kernel(sink_ref, q_ref, k_ref, v_ref, cos_ref, sin_ref, o_ref, kp_ref, vt_ref, bias_ref, *, seq):
    blk = ATTN_BLOCK
    hd = ATTN_HEAD_DIM
    nb = seq // blk
    hp = pl.program_id(1)
    lane = lax.broadcasted_iota(jnp.int32, (seq, V7X_LANES), 1)
    low = lane < hd
    first_half = (lane % hd) < hd // 2
    cos = cos_ref[0]
    sin = sin_ref[0]

    scale = hd ** -0.5 * LOG2_E
    lane_q = lax.broadcasted_iota(jnp.int32, (blk, V7X_LANES), 1)
    first_half_q = (lane_q % hd) < hd // 2

    kr = _rope(k_ref[0], cos, sin, first_half)
    lo = jnp.where(low, kr, 0.0)
    hi = jnp.where(low, 0.0, kr)
    zeros_pad = jnp.zeros((blk, V7X_LANES), BF16)
    for idx, val in enumerate((lo, pltpu.roll(lo, hd, 1), pltpu.roll(hi, hd, 1), hi)):
        kp_ref[idx, 0:blk, :] = zeros_pad
        kp_ref[idx, blk:blk + seq, :] = val.astype(BF16)
        kp_ref[idx, blk + seq:, :] = zeros_pad

    vt = jnp.transpose(v_ref[0])
    ones_row = lax.broadcasted_iota(jnp.int32, (ATTN_VT_ROWS - hd, seq), 0) == 0
    zeros_vt = jnp.zeros((ATTN_VT_ROWS, blk), BF16)
    for j in range(2):
        vt_ref[j, :, 0:blk] = zeros_vt
        vt_ref[j, 0:hd, blk:blk + seq] = vt[j * hd:(j + 1) * hd].astype(BF16)
        vt_ref[j, hd:, blk:blk + seq] = jnp.where(ones_row, 1.0, 0.0).astype(BF16)
        vt_ref[j, :, blk + seq:] = zeros_vt

    ci = lax.broadcasted_iota(jnp.int32, (blk, 2 * blk), 0)
    ri = lax.broadcasted_iota(jnp.int32, (blk, 2 * blk), 1) % blk
    neg = jnp.full((blk, 2 * blk), -jnp.inf, F32)
    bias_ref[0] = jnp.where(ci >= ri, 0.0, -jnp.inf)
    bias_ref[1] = neg
    bias_ref[2] = jnp.where(ci <= ri, 0.0, -jnp.inf)
    bias_ref[3] = neg
    lane2 = lax.broadcasted_iota(jnp.int32, (1, 2 * blk), 1)

    def body(i, carry):
        r0 = pl.multiple_of(i * blk, blk)
        bias_prev = bias_ref[jnp.where(i == 0, 1, 0)]
        bias_next = bias_ref[jnp.where(i == nb - 1, 3, 2)]
        chains = [(j, half) for j in range(2) for half in range(2)]
        cos_q = cos_ref[0, pl.ds(r0, blk), :]
        sin_q = sin_ref[0, pl.ds(r0, blk), :]
        q_tiles = []
        for t in range(ATTN_Q_TILES):
            qt = q_ref[0, pl.ds(r0, blk), t * V7X_LANES:(t + 1) * V7X_LANES]
            q_tiles.append((_rope(qt, cos_q, sin_q, first_half_q) * scale).astype(BF16))
        scores = []
        for j, half in chains:
            qcat = jnp.concatenate([q_tiles[2 * j], q_tiles[2 * j + 1]], axis=0)
            kk = kp_ref[2 * j + half, pl.ds(r0, 3 * blk), :]
            scores.append(lax.dot_general(kk, qcat, _CONTRACT_LAST, preferred_element_type=F32))
        probs = []
        for (j, half), st in zip(chains, scores):
            head = hp * ATTN_Q_HEADS_PER_PAIR + ATTN_GROUP * j + half
            snk = jnp.where(lane2 < blk, sink_ref[head], sink_ref[head + 2]) * LOG2_E
            parts = (st[0:blk] + bias_prev, st[blk:2 * blk], st[2 * blk:] + bias_next)
            m = snk
            for part in parts:
                m = jnp.maximum(m, jnp.max(part, axis=0, keepdims=True))
            p = jnp.concatenate([jnp.exp2(part - m) for part in parts], axis=0)
            probs.append((p.astype(BF16), jnp.exp2(snk - m)))
        outs = []
        for (j, half), (p, sink_p) in zip(chains, probs):
            vtw = vt_ref[j, :, pl.ds(r0, 3 * blk)]
            ot = jnp.dot(vtw, p, preferred_element_type=F32)
            outs.append(ot[0:hd] * (1.0 / (ot[hd:hd + 1] + sink_p)))
        for t in range(ATTN_Q_TILES):
            j, tt = divmod(t, 2)
            cols = slice(tt * blk, (tt + 1) * blk)
            tile_t = jnp.concatenate([outs[2 * j][:, cols], outs[2 * j + 1][:, cols]], axis=0)
            o_ref[0, pl.ds(r0, blk), t * V7X_LANES:(t + 1) * V7X_LANES] = jnp.transpose(tile_t).astype(o_ref.dtype)
        return carry

    lax.fori_loop(0, nb, body, 0, unroll=ATTN_LOOP_UNROLL)


def _attn_call(proj, cos, sin, sink, bsz, seq):
    n_pairs = ATTN_KV_HEADS // 2
    qw = ATTN_Q_TILES * V7X_LANES
    k_off = (proj.shape[-1] - 2 * ATTN_KV_HEADS * ATTN_HEAD_DIM) // V7X_LANES
    v_off = k_off + n_pairs
    assert seq // ATTN_BLOCK >= 2 and ATTN_BLOCK == V7X_LANES and 2 * ATTN_HEAD_DIM == V7X_LANES
    assert WINDOW == ATTN_BLOCK
    pad_seq = seq + 2 * ATTN_BLOCK
    tbl = pl.BlockSpec((1, seq, V7X_LANES), lambda b, p: (b, 0, 0))
    vmem = (2 * seq * (qw * 4 + 4 * V7X_LANES * 4 + qw * 2) + 4 * pad_seq * V7X_LANES * 2
            + 2 * ATTN_VT_ROWS * pad_seq * 2 + 3 * 3 * ATTN_BLOCK * 2 * ATTN_BLOCK * 4 + 6 * seq * V7X_LANES * 4)
    return pl.pallas_call(
        functools.partial(_attn_kernel, seq=seq),
        grid=(bsz, n_pairs),
        in_specs=[
            pl.BlockSpec(memory_space=pltpu.SMEM),
            pl.BlockSpec((1, seq, qw), lambda b, p: (b, 0, p)),
            pl.BlockSpec((1, seq, V7X_LANES), lambda b, p: (b, 0, k_off + p)),
            pl.BlockSpec((1, seq, V7X_LANES), lambda b, p: (b, 0, v_off + p)),
            tbl, tbl,
        ],
        out_specs=pl.BlockSpec((1, seq, qw), lambda b, p: (b, 0, p)),
        out_shape=jax.ShapeDtypeStruct((bsz, seq, n_pairs * qw), BF16),
        scratch_shapes=[
            pltpu.VMEM((4, pad_seq, V7X_LANES), BF16),
            pltpu.VMEM((2, ATTN_VT_ROWS, pad_seq), BF16),
            pltpu.VMEM((4, ATTN_BLOCK, 2 * ATTN_BLOCK), F32),
        ],
        compiler_params=_params(("arbitrary", "arbitrary"), vmem),
        name="swa_sink_attention",
    )(sink, proj, proj, proj, cos, sin)


def _gates_kernel(h_ref, w_ref, b_ref, o_ref):
    g_t = lax.dot_general(w_ref[...].astype(BF16), h_ref[...], _CONTRACT_LAST, preferred_element_type=F32)
    o_ref[0] = g_t + b_ref[...]


def _gates_call(h, w_t, layer, b_col, bsz, seq):
    m, k = h.shape
    tm = GATES_ROW_TILE
    tpb = seq // tm
    gate_blk = (w_t.shape[1] - MLSTM_GATES) // MLSTM_GATES
    return pl.pallas_call(
        _gates_kernel,
        grid=(m // tm,),
        in_specs=[
            pl.BlockSpec((tm, k), lambda i: (i, 0)),
            pl.BlockSpec((None, MLSTM_GATES, k), lambda i: (layer, gate_blk, 0)),
            pl.BlockSpec((MLSTM_GATES, 1), lambda i: (0, 0)),
        ],
        out_specs=pl.BlockSpec((1, MLSTM_GATES, tm), lambda i: (i // tpb, 0, i % tpb)),
        out_shape=jax.ShapeDtypeStruct((bsz, MLSTM_GATES, seq), F32),
        compiler_params=_params(("arbitrary",), 2 * tm * k * 2 + 4 * MLSTM_GATES * (k + tm) * 4),
        name="mlstm_gates",
    )(h, w_t, b_col)


def _log_sigmoid(x):
    return jnp.minimum(x, 0.0) - jnp.log1p(jnp.exp(-jnp.abs(x)))


def _mlstm_kernel(q_ref, k_ref, v_ref, g_ref, o_ref, qb_ref, kt_ref, vx_ref, c_ref, *, seq, chunk, dk, dv):
    nh = MLSTM_HEADS
    head = pl.program_id(1)
    nc = seq // chunk
    qb_ref[...] = (q_ref[0] * (dk ** -0.5)).astype(BF16)
    kt_ref[...] = jnp.transpose(k_ref[0])
    vx_ref[:, :dv] = v_ref[0].astype(BF16)
    ones_lane = lax.broadcasted_iota(jnp.int32, (seq, V7X_LANES), 1) == 0
    vx_ref[:, dv:] = jnp.where(ones_lane, 1.0, 0.0).astype(BF16)

    ti = lax.broadcasted_iota(jnp.int32, (chunk, chunk), 0)
    ui = lax.broadcasted_iota(jnp.int32, (chunk, chunk), 1)
    eye = ti == ui
    for direction in range(2):
        seen = (ui <= ti) if direction == 0 else (ui >= ti)
        c_ref[...] = jnp.zeros_like(c_ref)
        m_prev = jnp.full((1, 1), -1e30, F32)
        order = range(nc) if direction == 0 else range(nc - 1, -1, -1)
        for c in order:
            rows = slice(c * chunk, (c + 1) * chunk)
            gate_row = 2 * direction * nh + head
            li = g_ref[0, pl.ds(gate_row, 1), rows]
            lf = _log_sigmoid(g_ref[0, pl.ds(gate_row + nh, 1), rows])
            g_col = jnp.sum(jnp.where(seen, lf, 0.0), axis=1, keepdims=True)
            g_row = jnp.sum(jnp.where(eye, g_col, 0.0), axis=0, keepdims=True)
            g_tot = jnp.sum(lf, axis=1, keepdims=True)
            dm = jnp.where(seen, g_col - g_row + li, -jnp.inf)
            a = g_col + m_prev
            m_t = jnp.maximum(a, jnp.max(dm, axis=1, keepdims=True))
            p = jnp.exp(dm - m_t)
            ea = jnp.exp(a - m_t)

            qc = qb_ref[rows, :]
            ktc = kt_ref[:, rows]
            vxc = vx_ref[rows, :]
            sqk = jnp.dot(qc, ktc.astype(BF16), preferred_element_type=F32)
            sc = (sqk * p).astype(BF16)
            tot = (ea * jnp.dot(qc, c_ref[...].astype(BF16), preferred_element_type=F32)
                   + jnp.dot(sc, vxc, preferred_element_type=F32))
            den = tot[:, dv:dv + 1]
            hh = tot[:, :dv] * (1.0 / jnp.maximum(jnp.abs(den), jnp.exp(-m_t)))
            if direction == 0:
                o_ref[0, rows, :] = hh
            else:
                o_ref[0, rows, :] += hh

            w_log = g_tot - g_row + li
            m_new = jnp.maximum(g_tot + m_prev, jnp.max(w_log, axis=1, keepdims=True))
            decay = jnp.exp(g_tot + m_prev - m_new)
            w = jnp.exp(w_log - m_new)
            c_ref[...] = decay * c_ref[...] + jnp.dot((ktc * w).astype(BF16), vxc, preferred_element_type=F32)
            m_prev = m_new


def _mlstm_call(proj, gates_t, bsz, seq, dk, dv):
    nh = MLSTM_HEADS
    chunk = MLSTM_KERNEL_CHUNK
    k_blk = nh
    v_blk = (2 * nh * dk) // dv
    dvx = dv + V7X_LANES
    vmem = (2 * seq * (2 * dk + 2 * dv) * 4 + 2 * MLSTM_GATES * seq * 4
            + seq * dk * 2 + dk * seq * 4 + seq * dvx * 2 + dk * dvx * 4 + 8 * chunk * dvx * 4)
    return pl.pallas_call(
        functools.partial(_mlstm_kernel, seq=seq, chunk=chunk, dk=dk, dv=dv),
        grid=(bsz, nh),
        in_specs=[
            pl.BlockSpec((1, seq, dk), lambda b, h: (b, 0, h)),
            pl.BlockSpec((1, seq, dk), lambda b, h: (b, 0, k_blk + h)),
            pl.BlockSpec((1, seq, dv), lambda b, h: (b, 0, v_blk + h)),
            pl.BlockSpec((1, MLSTM_GATES, seq), lambda b, h: (b, 0, 0)),
        ],
        out_specs=pl.BlockSpec((1, seq, dv), lambda b, h: (b, 0, h)),
        out_shape=jax.ShapeDtypeStruct((bsz, seq, nh * dv), F32),
        scratch_shapes=[
            pltpu.VMEM((seq, dk), BF16),
            pltpu.VMEM((dk, seq), F32),
            pltpu.VMEM((seq, dvx), BF16),
            pltpu.VMEM((dk, dvx), F32),
        ],
        compiler_params=_params(("arbitrary", "arbitrary"), vmem),
        name="bidir_mlstm",
    )(proj, proj, proj, gates_t)


def kernel(x, c, positions, attn_w_qkv, attn_b_qkv, attn_sink, attn_w_o, attn_b_o, mlstm_w_in, mlstm_b_in,
           mlstm_norm_w, mlstm_w_o, mlstm_b_o, mod_w, mod_b, mlp_w1, mlp_b1, mlp_w2, mlp_b2,
           ln_mix_g, ln_mix_b, ln_mlp_g, ln_mlp_b):
    bsz, seq, d = x.shape
    depth = mod_w.shape[0]
    assert depth == DEPTH
    m = bsz * seq
    d_ff = mlp_w1.shape[-1]
    ml_main = mlstm_w_in.shape[-1] - MLSTM_GATES
    ml_dv = mlstm_w_o.shape[1] // MLSTM_HEADS
    ml_dk = (ml_main - 2 * MLSTM_HEADS * ml_dv) // (2 * MLSTM_HEADS)

    c_pad = jnp.pad(c, ((0, V7X_SUBLANES - bsz), (0, 0)))
    mod = _mod_call(c_pad, mod_w, mod_b)[:, :bsz]
    mod = mod.reshape(depth, bsz, 6, 1, d)
    sh_m, sc_m, g_m, sh_f, sc_f, g_f = (mod[:, :, j] for j in range(6))

    cos, sin = _rope_call(positions)
    x2 = x.reshape(m, d)
    h = _modulate_call(x2, sc_m[0], sh_m[0], seq)

    mlstm_w_in_t = jnp.swapaxes(mlstm_w_in, 1, 2)

    for i in range(depth):
        j = i // 2
        if i % 2 == 0:
            proj, w_o_bf = _mm_call(h, attn_w_qkv, j, attn_b_qkv[j], n_cols=attn_w_qkv.shape[-1], out_dtype=F32,
                                    side=(attn_w_o, j), name="attn_qkv_proj")
            a = _attn_call(proj.reshape(bsz, seq, -1), cos, sin, attn_sink[j], bsz, seq).reshape(m, d)
            x2, h = _mm_ln_call(a, w_o_bf, attn_b_o[j], x2, g_m[i], ln_mix_g[i], ln_mix_b[i],
                                (sc_f[i], sh_f[i]), rows_per_batch=seq, tm=LN_PROJ_ROW_TILE, name="attn_out_ln")
        else:
            proj, w_o_bf = _mm_call(h, mlstm_w_in_t, j, mlstm_b_in[j], n_cols=ml_main, out_dtype=F32,
                                    w_transposed=True, side=(mlstm_w_o, j), name="mlstm_in_proj")
            gates_t = _gates_call(h, mlstm_w_in_t, j, mlstm_b_in[j][ml_main:].reshape(MLSTM_GATES, 1), bsz, seq)
            hs = _mlstm_call(proj.reshape(bsz, seq, ml_main), gates_t, bsz, seq, ml_dk, ml_dv).reshape(m, d)
            x2, h = _mm_ln_call(hs, w_o_bf, mlstm_b_o[j], x2, g_m[i], ln_mix_g[i], ln_mix_b[i],
                                (sc_f[i], sh_f[i]), rows_per_batch=seq, tm=LN_PROJ_ROW_TILE, name="mlstm_out_ln",
                                ogate=(proj, (ml_main - d) // d), normw=mlstm_norm_w[j])
        u, w2_bf = _mm_call(h, mlp_w1, i, mlp_b1[i], n_cols=d_ff, out_dtype=BF16, act="relu2", side=(mlp_w2, i),
                            name="mlp_up")
        nxt = (sc_m[i + 1], sh_m[i + 1]) if i + 1 < depth else None
        x2, h = _mm_ln_call(u, w2_bf, mlp_b2[i], x2, g_f[i], ln_mlp_g[i], ln_mlp_b[i], nxt,
                            rows_per_batch=seq, tm=MLP_DOWN_ROW_TILE, name="mlp_down_ln")
    return x2.reshape(bsz, seq, d)
```

```python
import functools

import numpy as np
import jax
import jax.numpy as jnp
from jax import lax
from jax.experimental import pallas as pl
from jax.experimental.pallas import tpu as pltpu

F32 = jnp.float32
BF16 = jnp.bfloat16

DEPTH = 2
ATTN_HEAD_DIM = 64
ATTN_KV_HEADS = 8
WINDOW = 128
ATTN_BLOCK = 128
ROPE_THETA = 10000.0
MLSTM_HEADS = 4
MLSTM_GATES = 4 * MLSTM_HEADS
DEEPNORM_ALPHA = (2.0 * DEPTH) ** 0.25
LN_EPS = 1e-5
HEAD_NORM_EPS = 1e-6
MLSTM_KERNEL_CHUNK = 256
LOG2_E = 1.4426950408889634

V7X_LANES = 128
V7X_SUBLANES = 8
V7X_VMEM_BYTES = 64 * 1024 * 1024
VMEM_LIMIT_CAP = 60000 * 1024

MOD_COL_TILE = 1024
ELEMENTWISE_ROW_TILE = 2048
MM_ROW_TILE = 1024
MM_COL_TILE = 1024
LN_PROJ_ROW_TILE = 512
MLP_DOWN_ROW_TILE = 256
GATES_ROW_TILE = 1024
ATTN_LOOP_UNROLL = 16

ATTN_GROUP = 4
ATTN_Q_TILES = 2 * ATTN_GROUP * ATTN_HEAD_DIM // V7X_LANES
ATTN_Q_HEADS_PER_PAIR = 2 * ATTN_GROUP
VMEM_LIMIT_FLOOR = V7X_VMEM_BYTES - 28 * 1024 * 1024


def _params(semantics, vmem_bytes):
    limit = max(min(int(vmem_bytes * 1.25) + (4 << 20), VMEM_LIMIT_CAP), VMEM_LIMIT_FLOOR)
    return pltpu.CompilerParams(dimension_semantics=semantics, vmem_limit_bytes=limit)


def _mod_kernel(c_ref, w_ref, b_ref, o_ref):
    c = c_ref[...]
    c_act = (c * jax.nn.sigmoid(c)).astype(BF16)
    o_ref[0] = jnp.dot(c_act, w_ref[0].astype(BF16), preferred_element_type=F32) + b_ref[0]


def _mod_call(c_pad, mod_w, mod_b):
    depth, d, n = mod_w.shape
    rows = c_pad.shape[0]
    tn = MOD_COL_TILE
    return pl.pallas_call(
        _mod_kernel,
        grid=(depth, n // tn),
        in_specs=[
            pl.BlockSpec((rows, d), lambda i, j: (0, 0)),
            pl.BlockSpec((1, d, tn), lambda i, j: (i, 0, j)),
            pl.BlockSpec((1, 1, tn), lambda i, j: (i, 0, j)),
        ],
        out_specs=pl.BlockSpec((1, rows, tn), lambda i, j: (i, 0, j)),
        out_shape=jax.ShapeDtypeStruct((depth, rows, n), F32),
        compiler_params=_params(("arbitrary", "arbitrary"), 2 * d * tn * 4 + d * tn * 2),
        name="adaln_mod",
    )(c_pad, mod_w, mod_b.reshape(depth, 1, n))


def _modulate_kernel(x_ref, sc_ref, sh_ref, o_ref):
    o_ref[...] = (x_ref[...] * (1.0 + sc_ref[0]) + sh_ref[0]).astype(o_ref.dtype)


def _modulate_call(x2, sc, sh, rows_per_batch):
    m, d = x2.shape
    tm = ELEMENTWISE_ROW_TILE
    tpb = rows_per_batch // tm
    vec = pl.BlockSpec((1, 1, d), lambda i: (i // tpb, 0, 0))
    return pl.pallas_call(
        _modulate_kernel,
        grid=(m // tm,),
        in_specs=[pl.BlockSpec((tm, d), lambda i: (i, 0)), vec, vec],
        out_specs=pl.BlockSpec((tm, d), lambda i: (i, 0)),
        out_shape=jax.ShapeDtypeStruct((m, d), BF16),
        compiler_params=_params(("arbitrary",), 2 * tm * d * 6),
        name="modulate_in",
    )(x2, sc, sh)


_CONTRACT_LAST = (((1,), (1,)), ((), ()))


def _mm_kernel(*refs, act, w_transposed, side_cast):
    if side_cast:
        x_ref, w_ref, b_ref, side_ref, o_ref, side_o_ref, wb_ref = refs
        side_o_ref[...] = side_ref[...].astype(side_o_ref.dtype)
    else:
        x_ref, w_ref, b_ref, o_ref, wb_ref = refs

    @pl.when(pl.program_id(1) == 0)
    def _():
        wb_ref[...] = w_ref[...].astype(BF16)

    if w_transposed:
        acc = lax.dot_general(x_ref[...], wb_ref[...], _CONTRACT_LAST, preferred_element_type=F32)
    else:
        acc = jnp.dot(x_ref[...], wb_ref[...], preferred_element_type=F32)
    acc = acc + b_ref[...]
    if act == "relu2":
        acc = jnp.square(jnp.maximum(acc, 0.0))
    o_ref[...] = acc.astype(o_ref.dtype)


def _mm_call(x, w, layer, b, *, n_cols, out_dtype, act=None, w_transposed=False, side=None, name):
    m, k = x.shape
    tm, tn = MM_ROW_TILE, MM_COL_TILE
    osz = jnp.dtype(out_dtype).itemsize
    vmem = 2 * k * tn * 4 + k * tn * 2 + 2 * tm * k * 2 + 2 * tm * tn * osz + tm * tn * 4
    if w_transposed:
        w_spec = pl.BlockSpec((None, tn, k), lambda j, i: (layer, j, 0))
        w_tile = (tn, k)
    else:
        w_spec = pl.BlockSpec((None, k, tn), lambda j, i: (layer, 0, j))
        w_tile = (k, tn)
    n_i = m // tm
    grid = (n_cols // tn, n_i)
    in_specs = [pl.BlockSpec((tm, k), lambda j, i: (i, 0)), w_spec, pl.BlockSpec((1, tn), lambda j, i: (0, j))]
    args = [x, w, b.reshape(1, -1)]
    out_specs = [pl.BlockSpec((tm, tn), lambda j, i: (i, j))]
    out_shape = [jax.ShapeDtypeStruct((m, n_cols), out_dtype)]
    if side is not None:
        w2, layer2 = side
        _, k2, n2 = w2.shape
        n_blk = 1 << ((grid[0] * grid[1]).bit_length() - 1)
        rows = k2 // n_blk
        assert rows * n_blk == k2 and rows % 16 == 0
        in_specs.append(pl.BlockSpec((None, rows, n2), lambda j, i: (layer2, jnp.minimum(j * n_i + i, n_blk - 1), 0)))
        args.append(w2)
        out_specs.append(pl.BlockSpec((rows, n2), lambda j, i: (jnp.minimum(j * n_i + i, n_blk - 1), 0)))
        out_shape.append(jax.ShapeDtypeStruct((k2, n2), BF16))
        vmem += 2 * rows * n2 * 6
    outs = pl.pallas_call(
        functools.partial(_mm_kernel, act=act, w_transposed=w_transposed, side_cast=side is not None),
        grid=grid,
        in_specs=in_specs,
        out_specs=out_specs,
        out_shape=out_shape,
        scratch_shapes=[pltpu.VMEM(w_tile, BF16)],
        compiler_params=_params(("arbitrary", "arbitrary"), vmem),
        name=name,
    )(*args)
    return (outs[0], outs[1]) if side is not None else outs[0]


def _head_norm_gate(hs, og, nw, heads):
    dv = hs.shape[-1] // heads
    outs = []
    for h in range(heads):
        t = hs[:, h * dv:(h + 1) * dv]
        mu = jnp.mean(t, axis=-1, keepdims=True)
        var = jnp.mean(jnp.square(t - mu), axis=-1, keepdims=True)
        outs.append((t - mu) * lax.rsqrt(var + HEAD_NORM_EPS))
    hn = jnp.concatenate(outs, axis=-1) * nw
    return jax.nn.sigmoid(og) * hn


def _mm_ln_kernel(*refs, mlstm_prologue, emit_h):
    refs = list(refs)
    a_ref = refs.pop(0)
    if mlstm_prologue:
        og_ref = refs.pop(0)
        nw_ref = refs.pop(0)
    w_ref, b_ref, res_ref, gate_ref, lng_ref, lnb_ref = refs[:6]
    refs = refs[6:]
    if emit_h:
        scn_ref, shn_ref = refs[:2]
        refs = refs[2:]
    xo_ref = refs.pop(0)
    if emit_h:
        ho_ref = refs.pop(0)

    if mlstm_prologue:
        a = _head_norm_gate(a_ref[...], og_ref[...], nw_ref[...], MLSTM_HEADS).astype(BF16)
    else:
        a = a_ref[...]
    y = jnp.dot(a, w_ref[...], preferred_element_type=F32) + b_ref[...]
    z = DEEPNORM_ALPHA * res_ref[...] + (1.0 + gate_ref[0]) * y
    mu = jnp.mean(z, axis=-1, keepdims=True)
    var = jnp.mean(jnp.square(z - mu), axis=-1, keepdims=True)
    xn = (z - mu) * lax.rsqrt(var + LN_EPS) * lng_ref[...] + lnb_ref[...]
    xo_ref[...] = xn
    if emit_h:
        ho_ref[...] = (xn * (1.0 + scn_ref[0]) + shn_ref[0]).astype(ho_ref.dtype)


def _mm_ln_call(a, w_bf, b, res, gate, lng, lnb, nxt, *, rows_per_batch, tm, name, ogate=None, normw=None):
    m = a.shape[0]
    k, d = w_bf.shape
    tpb = rows_per_batch // tm
    mlstm_prologue = ogate is not None
    emit_h = nxt is not None
    vec_b = pl.BlockSpec((1, 1, d), lambda i: (i // tpb, 0, 0))
    vec = pl.BlockSpec((1, d), lambda i: (0, 0))
    row_tile = pl.BlockSpec((tm, d), lambda i: (i, 0))

    in_specs = [pl.BlockSpec((tm, k), lambda i: (i, 0))]
    args = [a]
    asz = a.dtype.itemsize
    if mlstm_prologue:
        ocb = ogate[1]
        in_specs += [pl.BlockSpec((tm, k), lambda i: (i, ocb)), pl.BlockSpec((1, k), lambda i: (0, 0))]
        args += [ogate[0], normw.reshape(1, -1)]
    in_specs += [pl.BlockSpec((k, d), lambda i: (0, 0), pipeline_mode=pl.Buffered(1)),
                 vec, row_tile, vec_b, vec, vec]
    args += [w_bf, b.reshape(1, d), res, gate, lng.reshape(1, d), lnb.reshape(1, d)]
    out_specs = [row_tile]
    out_shape = [jax.ShapeDtypeStruct((m, d), F32)]
    if emit_h:
        in_specs += [vec_b, vec_b]
        args += [nxt[0], nxt[1]]
        out_specs.append(row_tile)
        out_shape.append(jax.ShapeDtypeStruct((m, d), BF16))
    vmem = (k * d * 2 + 2 * tm * k * asz * (2 if mlstm_prologue else 1)
            + 2 * tm * d * (4 + 4 + 2) + 2 * tm * d * 4)
    outs = pl.pallas_call(
        functools.partial(_mm_ln_kernel, mlstm_prologue=mlstm_prologue, emit_h=emit_h),
        grid=(m // tm,),
        in_specs=in_specs,
        out_specs=out_specs,
        out_shape=out_shape,
        compiler_params=_params(("arbitrary",), vmem),
        name=name,
    )(*args)
    return (outs[0], outs[1]) if emit_h else (outs[0], None)


def _rope_kernel(pos_ref, freq_ref, cos_ref, sin_ref):
    ang = pos_ref[0].astype(F32) * freq_ref[...]
    lane = lax.broadcasted_iota(jnp.int32, ang.shape, 1)
    first_half = (lane % ATTN_HEAD_DIM) < ATTN_HEAD_DIM // 2
    cos_ref[0] = jnp.cos(ang)
    s = jnp.sin(ang)
    sin_ref[0] = jnp.where(first_half, -s, s)


def _rope_call(positions):
    bsz, s = positions.shape
    half = ATTN_HEAD_DIM // 2
    inv_freq = 1.0 / (ROPE_THETA ** (np.arange(0, ATTN_HEAD_DIM, 2, dtype=np.float32) / ATTN_HEAD_DIM))
    freq = jnp.asarray(np.tile(inv_freq.astype(np.float32), V7X_LANES // half).reshape(1, V7X_LANES))
    tbl = jax.ShapeDtypeStruct((bsz, s, V7X_LANES), F32)
    return pl.pallas_call(
        _rope_kernel,
        grid=(bsz,),
        in_specs=[pl.BlockSpec((1, s, 1), lambda b: (b, 0, 0)), pl.BlockSpec((1, V7X_LANES), lambda b: (0, 0))],
        out_specs=[pl.BlockSpec((1, s, V7X_LANES), lambda b: (b, 0, 0))] * 2,
        out_shape=[tbl, tbl],
        compiler_params=_params(("arbitrary",), 6 * s * V7X_LANES * 4),
        name="rope_tables",
    )(positions.reshape(bsz, s, 1), freq)


def _rope(t, cos, sin_signed, first_half):
    half = ATTN_HEAD_DIM // 2
    rot = jnp.where(first_half, pltpu.roll(t, V7X_LANES - half, 1), pltpu.roll(t, half, 1))
    return t * cos + rot * sin_signed


ATTN_VT_ROWS = ATTN_HEAD_DIM + 16


def _attn_kernel(sink_ref, q_ref, k_ref, v_ref, cos_ref, sin_ref, o_ref, kp_ref, vt_ref, bias_ref, *, seq):
    blk = ATTN_BLOCK
    hd = ATTN_HEAD_DIM
    nb = seq // blk
    hp = pl.program_id(1)
    lane = lax.broadcasted_iota(jnp.int32, (seq, V7X_LANES), 1)
    low = lane < hd
    first_half = (lane % hd) < hd // 2
    cos = cos_ref[0]
    sin = sin_ref[0]

    scale = hd ** -0.5 * LOG2_E
    lane_q = lax.broadcasted_iota(jnp.int32, (blk, V7X_LANES), 1)
    first_half_q = (lane_q % hd) < hd // 2

    kr = _rope(k_ref[0], cos, sin, first_half)
    lo = jnp.where(low, kr, 0.0)
    hi = jnp.where(low, 0.0, kr)
    zeros_pad = jnp.zeros((blk, V7X_LANES), BF16)
    for idx, val in enumerate((lo, pltpu.roll(lo, hd, 1), pltpu.roll(hi, hd, 1), hi)):
        kp_ref[idx, 0:blk, :] = zeros_pad
        kp_ref[idx, blk:blk + seq, :] = val.astype(BF16)
        kp_ref[idx, blk + seq:, :] = zeros_pad

    vt = jnp.transpose(v_ref[0])
    ones_row = lax.broadcasted_iota(jnp.int32, (ATTN_VT_ROWS - hd, seq), 0) == 0
    zeros_vt = jnp.zeros((ATTN_VT_ROWS, blk), BF16)
    for j in range(2):
        vt_ref[j, :, 0:blk] = zeros_vt
        vt_ref[j, 0:hd, blk:blk + seq] = vt[j * hd:(j + 1) * hd].astype(BF16)
        vt_ref[j, hd:, blk:blk + seq] = jnp.where(ones_row, 1.0, 0.0).astype(BF16)
        vt_ref[j, :, blk + seq:] = zeros_vt

    ci = lax.broadcasted_iota(jnp.int32, (blk, 2 * blk), 0)
    ri = lax.broadcasted_iota(jnp.int32, (blk, 2 * blk), 1) % blk
    neg = jnp.full((blk, 2 * blk), -jnp.inf, F32)
    bias_ref[0] = jnp.where(ci >= ri, 0.0, -jnp.inf)
    bias_ref[1] = neg
    bias_ref[2] = jnp.where(ci <= ri, 0.0, -jnp.inf)
    bias_ref[3] = neg
    lane2 = lax.broadcasted_iota(jnp.int32, (1, 2 * blk), 1)

    def body(i, carry):
        r0 = pl.multiple_of(i * blk, blk)
        bias_prev = bias_ref[jnp.where(i == 0, 1, 0)]
        bias_next = bias_ref[jnp.where(i == nb - 1, 3, 2)]
        chains = [(j, half) for j in range(2) for half in range(2)]
        cos_q = cos_ref[0, pl.ds(r0, blk), :]
        sin_q = sin_ref[0, pl.ds(r0, blk), :]
        q_tiles = []
        for t in range(ATTN_Q_TILES):
            qt = q_ref[0, pl.ds(r0, blk), t * V7X_LANES:(t + 1) * V7X_LANES]
            q_tiles.append((_rope(qt, cos_q, sin_q, first_half_q) * scale).astype(BF16))
        scores = []
        for j, half in chains:
            qcat = jnp.concatenate([q_tiles[2 * j], q_tiles[2 * j + 1]], axis=0)
            kk = kp_ref[2 * j + half, pl.ds(r0, 3 * blk), :]
            scores.append(lax.dot_general(kk, qcat, _CONTRACT_LAST, preferred_element_type=F32))
        probs = []
        for (j, half), st in zip(chains, scores):
            head = hp * ATTN_Q_HEADS_PER_PAIR + ATTN_GROUP * j + half
            snk = jnp.where(lane2 < blk, sink_ref[head], sink_ref[head + 2]) * LOG2_E
            parts = (st[0:blk] + bias_prev, st[blk:2 * blk], st[2 * blk:] + bias_next)
            m = snk
            for part in parts:
                m = jnp.maximum(m, jnp.max(part, axis=0, keepdims=True))
            p = jnp.concatenate([jnp.exp2(part - m) for part in parts], axis=0)
            probs.append((p.astype(BF16), jnp.exp2(snk - m)))
        outs = []
        for (j, half), (p, sink_p) in zip(chains, probs):
            vtw = vt_ref[j, :, pl.ds(r0, 3 * blk)]
            ot = jnp.dot(vtw, p, preferred_element_type=F32)
            outs.append(ot[0:hd] * (1.0 / (ot[hd:hd + 1] + sink_p)))
        for t in range(ATTN_Q_TILES):
            j, tt = divmod(t, 2)
            cols = slice(tt * blk, (tt + 1) * blk)
            tile_t = jnp.concatenate([outs[2 * j][:, cols], outs[2 * j + 1][:, cols]], axis=0)
            o_ref[0, pl.ds(r0, blk), t * V7X_LANES:(t + 1) * V7X_LANES] = jnp.transpose(tile_t).astype(o_ref.dtype)
        return carry

    lax.fori_loop(0, nb, body, 0, unroll=ATTN_LOOP_UNROLL)


def _attn_call(proj, cos, sin, sink, bsz, seq):
    n_pairs = ATTN_KV_HEADS // 2
    qw = ATTN_Q_TILES * V7X_LANES
    k_off = (proj.shape[-1] - 2 * ATTN_KV_HEADS * ATTN_HEAD_DIM) // V7X_LANES
    v_off = k_off + n_pairs
    assert seq // ATTN_BLOCK >= 2 and ATTN_BLOCK == V7X_LANES and 2 * ATTN_HEAD_DIM == V7X_LANES
    assert WINDOW == ATTN_BLOCK
    pad_seq = seq + 2 * ATTN_BLOCK
    tbl = pl.BlockSpec((1, seq, V7X_LANES), lambda b, p: (b, 0, 0))
    vmem = (2 * seq * (qw * 4 + 4 * V7X_LANES * 4 + qw * 2) + 4 * pad_seq * V7X_LANES * 2
            + 2 * ATTN_VT_ROWS * pad_seq * 2 + 3 * 3 * ATTN_BLOCK * 2 * ATTN_BLOCK * 4 + 6 * seq * V7X_LANES * 4)
    return pl.pallas_call(
        functools.partial(_attn_kernel, seq=seq),
        grid=(bsz, n_pairs),
        in_specs=[
            pl.BlockSpec(memory_space=pltpu.SMEM),
            pl.BlockSpec((1, seq, qw), lambda b, p: (b, 0, p)),
            pl.BlockSpec((1, seq, V7X_LANES), lambda b, p: (b, 0, k_off + p)),
            pl.BlockSpec((1, seq, V7X_LANES), lambda b, p: (b, 0, v_off + p)),
            tbl, tbl,
        ],
        out_specs=pl.BlockSpec((1, seq, qw), lambda b, p: (b, 0, p)),
        out_shape=jax.ShapeDtypeStruct((bsz, seq, n_pairs * qw), BF16),
        scratch_shapes=[
            pltpu.VMEM((4, pad_seq, V7X_LANES), BF16),
            pltpu.VMEM((2, ATTN_VT_ROWS, pad_seq), BF16),
            pltpu.VMEM((4, ATTN_BLOCK, 2 * ATTN_BLOCK), F32),
        ],
        compiler_params=_params(("arbitrary", "arbitrary"), vmem),
        name="swa_sink_attention",
    )(sink, proj, proj, proj, cos, sin)


def _gates_kernel(h_ref, w_ref, b_ref, o_ref):
    g_t = lax.dot_general(w_ref[...].astype(BF16), h_ref[...], _CONTRACT_LAST, preferred_element_type=F32)
    o_ref[0] = g_t + b_ref[...]


def _gates_call(h, w_t, layer, b_col, bsz, seq):
    m, k = h.shape
    tm = GATES_ROW_TILE
    tpb = seq // tm
    gate_blk = (w_t.shape[1] - MLSTM_GATES) // MLSTM_GATES
    return pl.pallas_call(
        _gates_kernel,
        grid=(m // tm,),
        in_specs=[
            pl.BlockSpec((tm, k), lambda i: (i, 0)),
            pl.BlockSpec((None, MLSTM_GATES, k), lambda i: (layer, gate_blk, 0)),
            pl.BlockSpec((MLSTM_GATES, 1), lambda i: (0, 0)),
        ],
        out_specs=pl.BlockSpec((1, MLSTM_GATES, tm), lambda i: (i // tpb, 0, i % tpb)),
        out_shape=jax.ShapeDtypeStruct((bsz, MLSTM_GATES, seq), F32),
        compiler_params=_params(("arbitrary",), 2 * tm * k * 2 + 4 * MLSTM_GATES * (k + tm) * 4),
        name="mlstm_gates",
    )(h, w_t, b_col)


def _log_sigmoid(x):
    return jnp.minimum(x, 0.0) - jnp.log1p(jnp.exp(-jnp.abs(x)))


def _mlstm_kernel(q_ref, k_ref, v_ref, g_ref, o_ref, qb_ref, kt_ref, vx_ref, c_ref, *, seq, chunk, dk, dv):
    nh = MLSTM_HEADS
    head = pl.program_id(1)
    nc = seq // chunk
    qb_ref[...] = (q_ref[0] * (dk ** -0.5)).astype(BF16)
    kt_ref[...] = jnp.transpose(k_ref[0])
    vx_ref[:, :dv] = v_ref[0].astype(BF16)
    ones_lane = lax.broadcasted_iota(jnp.int32, (seq, V7X_LANES), 1) == 0
    vx_ref[:, dv:] = jnp.where(ones_lane, 1.0, 0.0).astype(BF16)

    ti = lax.broadcasted_iota(jnp.int32, (chunk, chunk), 0)
    ui = lax.broadcasted_iota(jnp.int32, (chunk, chunk), 1)
    eye = ti == ui
    for direction in range(2):
        seen = (ui <= ti) if direction == 0 else (ui >= ti)
        c_ref[...] = jnp.zeros_like(c_ref)
        m_prev = jnp.full((1, 1), -1e30, F32)
        order = range(nc) if direction == 0 else range(nc - 1, -1, -1)
        for c in order:
            rows = slice(c * chunk, (c + 1) * chunk)
            gate_row = 2 * direction * nh + head
            li = g_ref[0, pl.ds(gate_row, 1), rows]
            lf = _log_sigmoid(g_ref[0, pl.ds(gate_row + nh, 1), rows])
            g_col = jnp.sum(jnp.where(seen, lf, 0.0), axis=1, keepdims=True)
            g_row = jnp.sum(jnp.where(eye, g_col, 0.0), axis=0, keepdims=True)
            g_tot = jnp.sum(lf, axis=1, keepdims=True)
            dm = jnp.where(seen, g_col - g_row + li, -jnp.inf)
            a = g_col + m_prev
            m_t = jnp.maximum(a, jnp.max(dm, axis=1, keepdims=True))
            p = jnp.exp(dm - m_t)
            ea = jnp.exp(a - m_t)

            qc = qb_ref[rows, :]
            ktc = kt_ref[:, rows]
            vxc = vx_ref[rows, :]
            sqk = jnp.dot(qc, ktc.astype(BF16), preferred_element_type=F32)
            sc = (sqk * p).astype(BF16)
            tot = (ea * jnp.dot(qc, c_ref[...].astype(BF16), preferred_element_type=F32)
                   + jnp.dot(sc, vxc, preferred_element_type=F32))
            den = tot[:, dv:dv + 1]
            hh = tot[:, :dv] * (1.0 / jnp.maximum(jnp.abs(den), jnp.exp(-m_t)))
            if direction == 0:
                o_ref[0, rows, :] = hh
            else:
                o_ref[0, rows, :] += hh

            w_log = g_tot - g_row + li
            m_new = jnp.maximum(g_tot + m_prev, jnp.max(w_log, axis=1, keepdims=True))
            decay = jnp.exp(g_tot + m_prev - m_new)
            w = jnp.exp(w_log - m_new)
            c_ref[...] = decay * c_ref[...] + jnp.dot((ktc * w).astype(BF16), vxc, preferred_element_type=F32)
            m_prev = m_new


def _mlstm_call(proj, gates_t, bsz, seq, dk, dv):
    nh = MLSTM_HEADS
    chunk = MLSTM_KERNEL_CHUNK
    k_blk = nh
    v_blk = (2 * nh * dk) // dv
    dvx = dv + V7X_LANES
    vmem = (2 * seq * (2 * dk + 2 * dv) * 4 + 2 * MLSTM_GATES * seq * 4
            + seq * dk * 2 + dk * seq * 4 + seq * dvx * 2 + dk * dvx * 4 + 8 * chunk * dvx * 4)
    return pl.pallas_call(
        functools.partial(_mlstm_kernel, seq=seq, chunk=chunk, dk=dk, dv=dv),
        grid=(bsz, nh),
        in_specs=[
            pl.BlockSpec((1, seq, dk), lambda b, h: (b, 0, h)),
            pl.BlockSpec((1, seq, dk), lambda b, h: (b, 0, k_blk + h)),
            pl.BlockSpec((1, seq, dv), lambda b, h: (b, 0, v_blk + h)),
            pl.BlockSpec((1, MLSTM_GATES, seq), lambda b, h: (b, 0, 0)),
        ],
        out_specs=pl.BlockSpec((1, seq, dv), lambda b, h: (b, 0, h)),
        out_shape=jax.ShapeDtypeStruct((bsz, seq, nh * dv), F32),
        scratch_shapes=[
            pltpu.VMEM((seq, dk), BF16),
            pltpu.VMEM((dk, seq), F32),
            pltpu.VMEM((seq, dvx), BF16),
            pltpu.VMEM((dk, dvx), F32),
        ],
        compiler_params=_params(("arbitrary", "arbitrary"), vmem),
        name="bidir_mlstm",
    )(proj, proj, proj, gates_t)


def kernel(x, c, positions, attn_w_qkv, attn_b_qkv, attn_sink, attn_w_o, attn_b_o, mlstm_w_in, mlstm_b_in,
           mlstm_norm_w, mlstm_w_o, mlstm_b_o, mod_w, mod_b, mlp_w1, mlp_b1, mlp_w2, mlp_b2,
           ln_mix_g, ln_mix_b, ln_mlp_g, ln_mlp_b):
    bsz, seq, d = x.shape
    depth = mod_w.shape[0]
    assert depth == DEPTH
    m = bsz * seq
    d_ff = mlp_w1.shape[-1]
    ml_main = mlstm_w_in.shape[-1] - MLSTM_GATES
    ml_dv = mlstm_w_o.shape[1] // MLSTM_HEADS
    ml_dk = (ml_main - 2 * MLSTM_HEADS * ml_dv) // (2 * MLSTM_HEADS)

    c_pad = jnp.pad(c, ((0, V7X_SUBLANES - bsz), (0, 0)))
    mod = _mod_call(c_pad, mod_w, mod_b)[:, :bsz]
    mod = mod.reshape(depth, bsz, 6, 1, d)
    sh_m, sc_m, g_m, sh_f, sc_f, g_f = (mod[:, :, j] for j in range(6))

    cos, sin = _rope_call(positions)
    x2 = x.reshape(m, d)
    h = _modulate_call(x2, sc_m[0], sh_m[0], seq)

    mlstm_w_in_t = jnp.swapaxes(mlstm_w_in, 1, 2)

    for i in range(depth):
        j = i // 2
        if i % 2 == 0:
            proj, w_o_bf = _mm_call(h, attn_w_qkv, j, attn_b_qkv[j], n_cols=attn_w_qkv.shape[-1], out_dtype=F32,
                                    side=(attn_w_o, j), name="attn_qkv_proj")
            a = _attn_call(proj.reshape(bsz, seq, -1), cos, sin, attn_sink[j], bsz, seq).reshape(m, d)
            x2, h = _mm_ln_call(a, w_o_bf, attn_b_o[j], x2, g_m[i], ln_mix_g[i], ln_mix_b[i],
                                (sc_f[i], sh_f[i]), rows_per_batch=seq, tm=LN_PROJ_ROW_TILE, name="attn_out_ln")
        else:
            proj, w_o_bf = _mm_call(h, mlstm_w_in_t, j, mlstm_b_in[j], n_cols=ml_main, out_dtype=F32,
                                    w_transposed=True, side=(mlstm_w_o, j), name="mlstm_in_proj")
            gates_t = _gates_call(h, mlstm_w_in_t, j, mlstm_b_in[j][ml_main:].reshape(MLSTM_GATES, 1), bsz, seq)
            hs = _mlstm_call(proj.reshape(bsz, seq, ml_main), gates_t, bsz, seq, ml_dk, ml_dv).reshape(m, d)
            x2, h = _mm_ln_call(hs, w_o_bf, mlstm_b_o[j], x2, g_m[i], ln_mix_g[i], ln_mix_b[i],
                                (sc_f[i], sh_f[i]), rows_per_batch=seq, tm=LN_PROJ_ROW_TILE, name="mlstm_out_ln",
                                ogate=(proj, (ml_main - d) // d), normw=mlstm_norm_w[j])
        u, w2_bf = _mm_call(h, mlp_w1, i, mlp_b1[i], n_cols=d_ff, out_dtype=BF16, act="relu2", side=(mlp_w2, i),
                            name="mlp_up")
        nxt = (sc_m[i + 1], sh_m[i + 1]) if i + 1 < depth else None
        x2, h = _mm_ln_call(u, w2_bf, mlp_b2[i], x2, g_f[i], ln_mlp_g[i], ln_mlp_b[i], nxt,
                            rows_per_batch=seq, tm=MLP_DOWN_ROW_TILE, name="mlp_down_ln")
    return x2.reshape(bsz, seq, d)
```

```python
import functools

import numpy as np
import jax
import jax.numpy as jnp
from jax import lax
from jax.experimental import pallas as pl
from jax.experimental.pallas import tpu as pltpu

F32 = jnp.float32
BF16 = jnp.bfloat16

DEPTH = 2
ATTN_HEAD_DIM = 64
ATTN_KV_HEADS = 8
WINDOW = 128
ATTN_BLOCK = 128
ROPE_THETA = 10000.0
MLSTM_HEADS = 4
MLSTM_GATES = 4 * MLSTM_HEADS
DEEPNORM_ALPHA = (2.0 * DEPTH) ** 0.25
LN_EPS = 1e-5
HEAD_NORM_EPS = 1e-6
MLSTM_KERNEL_CHUNK = 256
LOG2_E = 1.4426950408889634

V7X_LANES = 128
V7X_SUBLANES = 8
V7X_VMEM_BYTES = 64 * 1024 * 1024
VMEM_LIMIT_CAP = 60000 * 1024

MOD_COL_TILE = 1024
ELEMENTWISE_ROW_TILE = 2048
MM_ROW_TILE = 1024
MM_COL_TILE = 1024
LN_PROJ_ROW_TILE = 512
MLP_DOWN_ROW_TILE = 256
GATES_ROW_TILE = 1024

ATTN_GROUP = 4
ATTN_Q_TILES = 2 * ATTN_GROUP * ATTN_HEAD_DIM // V7X_LANES
ATTN_Q_HEADS_PER_PAIR = 2 * ATTN_GROUP
VMEM_LIMIT_FLOOR = V7X_VMEM_BYTES - 28 * 1024 * 1024


def _params(semantics, vmem_bytes):
    limit = max(min(int(vmem_bytes * 1.25) + (4 << 20), VMEM_LIMIT_CAP), VMEM_LIMIT_FLOOR)
    return pltpu.CompilerParams(dimension_semantics=semantics, vmem_limit_bytes=limit)


def _mod_kernel(c_ref, w_ref, b_ref, o_ref):
    c = c_ref[...]
    c_act = (c * jax.nn.sigmoid(c)).astype(BF16)
    o_ref[0] = jnp.dot(c_act, w_ref[0].astype(BF16), preferred_element_type=F32) + b_ref[0]


def _mod_call(c_pad, mod_w, mod_b):
    depth, d, n = mod_w.shape
    rows = c_pad.shape[0]
    tn = MOD_COL_TILE
    return pl.pallas_call(
        _mod_kernel,
        grid=(depth, n // tn),
        in_specs=[
            pl.BlockSpec((rows, d), lambda i, j: (0, 0)),
            pl.BlockSpec((1, d, tn), lambda i, j: (i, 0, j)),
            pl.BlockSpec((1, 1, tn), lambda i, j: (i, 0, j)),
        ],
        out_specs=pl.BlockSpec((1, rows, tn), lambda i, j: (i, 0, j)),
        out_shape=jax.ShapeDtypeStruct((depth, rows, n), F32),
        compiler_params=_params(("arbitrary", "arbitrary"), 2 * d * tn * 4 + d * tn * 2),
        name="adaln_mod",
    )(c_pad, mod_w, mod_b.reshape(depth, 1, n))


def _modulate_kernel(x_ref, sc_ref, sh_ref, o_ref):
    o_ref[...] = (x_ref[...] * (1.0 + sc_ref[0]) + sh_ref[0]).astype(o_ref.dtype)


def _modulate_call(x2, sc, sh, rows_per_batch):
    m, d = x2.shape
    tm = ELEMENTWISE_ROW_TILE
    tpb = rows_per_batch // tm
    vec = pl.BlockSpec((1, 1, d), lambda i: (i // tpb, 0, 0))
    return pl.pallas_call(
        _modulate_kernel,
        grid=(m // tm,),
        in_specs=[pl.BlockSpec((tm, d), lambda i: (i, 0)), vec, vec],
        out_specs=pl.BlockSpec((tm, d), lambda i: (i, 0)),
        out_shape=jax.ShapeDtypeStruct((m, d), BF16),
        compiler_params=_params(("arbitrary",), 2 * tm * d * 6),
        name="modulate_in",
    )(x2, sc, sh)


_CONTRACT_LAST = (((1,), (1,)), ((), ()))


def _mm_kernel(*refs, act, w_transposed, side_cast):
    if side_cast:
        x_ref, w_ref, b_ref, side_ref, o_ref, side_o_ref, wb_ref = refs
        side_o_ref[...] = side_ref[...].astype(side_o_ref.dtype)
    else:
        x_ref, w_ref, b_ref, o_ref, wb_ref = refs

    @pl.when(pl.program_id(1) == 0)
    def _():
        wb_ref[...] = w_ref[...].astype(BF16)

    if w_transposed:
        acc = lax.dot_general(x_ref[...], wb_ref[...], _CONTRACT_LAST, preferred_element_type=F32)
    else:
        acc = jnp.dot(x_ref[...], wb_ref[...], preferred_element_type=F32)
    acc = acc + b_ref[...]
    if act == "relu2":
        acc = jnp.square(jnp.maximum(acc, 0.0))
    o_ref[...] = acc.astype(o_ref.dtype)


def _mm_call(x, w, layer, b, *, n_cols, out_dtype, act=None, w_transposed=False, side=None, name):
    m, k = x.shape
    tm, tn = MM_ROW_TILE, MM_COL_TILE
    osz = jnp.dtype(out_dtype).itemsize
    vmem = 2 * k * tn * 4 + k * tn * 2 + 2 * tm * k * 2 + 2 * tm * tn * osz + tm * tn * 4
    if w_transposed:
        w_spec = pl.BlockSpec((None, tn, k), lambda j, i: (layer, j, 0))
        w_tile = (tn, k)
    else:
        w_spec = pl.BlockSpec((None, k, tn), lambda j, i: (layer, 0, j))
        w_tile = (k, tn)
    n_i = m // tm
    grid = (n_cols // tn, n_i)
    in_specs = [pl.BlockSpec((tm, k), lambda j, i: (i, 0)), w_spec, pl.BlockSpec((1, tn), lambda j, i: (0, j))]
    args = [x, w, b.reshape(1, -1)]
    out_specs = [pl.BlockSpec((tm, tn), lambda j, i: (i, j))]
    out_shape = [jax.ShapeDtypeStruct((m, n_cols), out_dtype)]
    if side is not None:
        w2, layer2 = side
        _, k2, n2 = w2.shape
        n_blk = 1 << ((grid[0] * grid[1]).bit_length() - 1)
        rows = k2 // n_blk
        assert rows * n_blk == k2 and rows % 16 == 0
        in_specs.append(pl.BlockSpec((None, rows, n2), lambda j, i: (layer2, jnp.minimum(j * n_i + i, n_blk - 1), 0)))
        args.append(w2)
        out_specs.append(pl.BlockSpec((rows, n2), lambda j, i: (jnp.minimum(j * n_i + i, n_blk - 1), 0)))
        out_shape.append(jax.ShapeDtypeStruct((k2, n2), BF16))
        vmem += 2 * rows * n2 * 6
    outs = pl.pallas_call(
        functools.partial(_mm_kernel, act=act, w_transposed=w_transposed, side_cast=side is not None),
        grid=grid,
        in_specs=in_specs,
        out_specs=out_specs,
        out_shape=out_shape,
        scratch_shapes=[pltpu.VMEM(w_tile, BF16)],
        compiler_params=_params(("arbitrary", "arbitrary"), vmem),
        name=name,
    )(*args)
    return (outs[0], outs[1]) if side is not None else outs[0]


def _head_norm_gate(hs, og, nw, heads):
    dv = hs.shape[-1] // heads
    outs = []
    for h in range(heads):
        t = hs[:, h * dv:(h + 1) * dv]
        mu = jnp.mean(t, axis=-1, keepdims=True)
        var = jnp.mean(jnp.square(t - mu), axis=-1, keepdims=True)
        outs.append((t - mu) * lax.rsqrt(var + HEAD_NORM_EPS))
    hn = jnp.concatenate(outs, axis=-1) * nw
    return jax.nn.sigmoid(og) * hn


def _mm_ln_kernel(*refs, mlstm_prologue, emit_h):
    refs = list(refs)
    a_ref = refs.pop(0)
    if mlstm_prologue:
        og_ref = refs.pop(0)
        nw_ref = refs.pop(0)
    w_ref, b_ref, res_ref, gate_ref, lng_ref, lnb_ref = refs[:6]
    refs = refs[6:]
    if emit_h:
        scn_ref, shn_ref = refs[:2]
        refs = refs[2:]
    xo_ref = refs.pop(0)
    if emit_h:
        ho_ref = refs.pop(0)

    if mlstm_prologue:
        a = _head_norm_gate(a_ref[...], og_ref[...], nw_ref[...], MLSTM_HEADS).astype(BF16)
    else:
        a = a_ref[...]
    y = jnp.dot(a, w_ref[...], preferred_element_type=F32) + b_ref[...]
    z = DEEPNORM_ALPHA * res_ref[...] + (1.0 + gate_ref[0]) * y
    mu = jnp.mean(z, axis=-1, keepdims=True)
    var = jnp.mean(jnp.square(z - mu), axis=-1, keepdims=True)
    xn = (z - mu) * lax.rsqrt(var + LN_EPS) * lng_ref[...] + lnb_ref[...]
    xo_ref[...] = xn
    if emit_h:
        ho_ref[...] = (xn * (1.0 + scn_ref[0]) + shn_ref[0]).astype(ho_ref.dtype)


def _mm_ln_call(a, w_bf, b, res, gate, lng, lnb, nxt, *, rows_per_batch, tm, name, ogate=None, normw=None):
    m = a.shape[0]
    k, d = w_bf.shape
    tpb = rows_per_batch // tm
    mlstm_prologue = ogate is not None
    emit_h = nxt is not None
    vec_b = pl.BlockSpec((1, 1, d), lambda i: (i // tpb, 0, 0))
    vec = pl.BlockSpec((1, d), lambda i: (0, 0))
    row_tile = pl.BlockSpec((tm, d), lambda i: (i, 0))

    in_specs = [pl.BlockSpec((tm, k), lambda i: (i, 0))]
    args = [a]
    asz = a.dtype.itemsize
    if mlstm_prologue:
        ocb = ogate[1]
        in_specs += [pl.BlockSpec((tm, k), lambda i: (i, ocb)), pl.BlockSpec((1, k), lambda i: (0, 0))]
        args += [ogate[0], normw.reshape(1, -1)]
    in_specs += [pl.BlockSpec((k, d), lambda i: (0, 0), pipeline_mode=pl.Buffered(1)),
                 vec, row_tile, vec_b, vec, vec]
    args += [w_bf, b.reshape(1, d), res, gate, lng.reshape(1, d), lnb.reshape(1, d)]
    out_specs = [row_tile]
    out_shape = [jax.ShapeDtypeStruct((m, d), F32)]
    if emit_h:
        in_specs += [vec_b, vec_b]
        args += [nxt[0], nxt[1]]
        out_specs.append(row_tile)
        out_shape.append(jax.ShapeDtypeStruct((m, d), BF16))
    vmem = (k * d * 2 + 2 * tm * k * asz * (2 if mlstm_prologue else 1)
            + 2 * tm * d * (4 + 4 + 2) + 2 * tm * d * 4)
    outs = pl.pallas_call(
        functools.partial(_mm_ln_kernel, mlstm_prologue=mlstm_prologue, emit_h=emit_h),
        grid=(m // tm,),
        in_specs=in_specs,
        out_specs=out_specs,
        out_shape=out_shape,
        compiler_params=_params(("arbitrary",), vmem),
        name=name,
    )(*args)
    return (outs[0], outs[1]) if emit_h else (outs[0], None)


def _rope_kernel(pos_ref, freq_ref, cos_ref, sin_ref):
    ang = pos_ref[0].astype(F32) * freq_ref[...]
    lane = lax.broadcasted_iota(jnp.int32, ang.shape, 1)
    first_half = (lane % ATTN_HEAD_DIM) < ATTN_HEAD_DIM // 2
    cos_ref[0] = jnp.cos(ang)
    s = jnp.sin(ang)
    sin_ref[0] = jnp.where(first_half, -s, s)


def _rope_call(positions):
    bsz, s = positions.shape
    half = ATTN_HEAD_DIM // 2
    inv_freq = 1.0 / (ROPE_THETA ** (np.arange(0, ATTN_HEAD_DIM, 2, dtype=np.float32) / ATTN_HEAD_DIM))
    freq = jnp.asarray(np.tile(inv_freq.astype(np.float32), V7X_LANES // half).reshape(1, V7X_LANES))
    tbl = jax.ShapeDtypeStruct((bsz, s, V7X_LANES), F32)
    return pl.pallas_call(
        _rope_kernel,
        grid=(bsz,),
        in_specs=[pl.BlockSpec((1, s, 1), lambda b: (b, 0, 0)), pl.BlockSpec((1, V7X_LANES), lambda b: (0, 0))],
        out_specs=[pl.BlockSpec((1, s, V7X_LANES), lambda b: (b, 0, 0))] * 2,
        out_shape=[tbl, tbl],
        compiler_params=_params(("arbitrary",), 6 * s * V7X_LANES * 4),
        name="rope_tables",
    )(positions.reshape(bsz, s, 1), freq)


def _rope(t, cos, sin_signed, first_half):
    half = ATTN_HEAD_DIM // 2
    rot = jnp.where(first_half, pltpu.roll(t, V7X_LANES - half, 1), pltpu.roll(t, half, 1))
    return t * cos + rot * sin_signed


ATTN_VT_ROWS = ATTN_HEAD_DIM + 16


def _attn_kernel(sink_ref, q_ref, k_ref, v_ref, cos_ref, sin_ref, o_ref, kp_ref, vt_ref, bias_ref, *, seq):
    blk = ATTN_BLOCK
    hd = ATTN_HEAD_DIM
    nb = seq // blk
    hp = pl.program_id(1)
    lane = lax.broadcasted_iota(jnp.int32, (seq, V7X_LANES), 1)
    low = lane < hd
    first_half = (lane % hd) < hd // 2
    cos = cos_ref[0]
    sin = sin_ref[0]

    scale = hd ** -0.5 * LOG2_E
    lane_q = lax.broadcasted_iota(jnp.int32, (blk, V7X_LANES), 1)
    first_half_q = (lane_q % hd) < hd // 2

    kr = _rope(k_ref[0], cos, sin, first_half)
    lo = jnp.where(low, kr, 0.0)
    hi = jnp.where(low, 0.0, kr)
    zeros_pad = jnp.zeros((blk, V7X_LANES), BF16)
    for idx, val in enumerate((lo, pltpu.roll(lo, hd, 1), pltpu.roll(hi, hd, 1), hi)):
        kp_ref[idx, 0:blk, :] = zeros_pad
        kp_ref[idx, blk:blk + seq, :] = val.astype(BF16)
        kp_ref[idx, blk + seq:, :] = zeros_pad

    vt = jnp.transpose(v_ref[0])
    ones_row = lax.broadcasted_iota(jnp.int32, (ATTN_VT_ROWS - hd, seq), 0) == 0
    zeros_vt = jnp.zeros((ATTN_VT_ROWS, blk), BF16)
    for j in range(2):
        vt_ref[j, :, 0:blk] = zeros_vt
        vt_ref[j, 0:hd, blk:blk + seq] = vt[j * hd:(j + 1) * hd].astype(BF16)
        vt_ref[j, hd:, blk:blk + seq] = jnp.where(ones_row, 1.0, 0.0).astype(BF16)
        vt_ref[j, :, blk + seq:] = zeros_vt

    ci = lax.broadcasted_iota(jnp.int32, (blk, 2 * blk), 0)
    ri = lax.broadcasted_iota(jnp.int32, (blk, 2 * blk), 1) % blk
    neg = jnp.full((blk, 2 * blk), -jnp.inf, F32)
    bias_ref[0] = jnp.where(ci >= ri, 0.0, -jnp.inf)
    bias_ref[1] = neg
    bias_ref[2] = jnp.where(ci <= ri, 0.0, -jnp.inf)
    bias_ref[3] = neg
    lane2 = lax.broadcasted_iota(jnp.int32, (1, 2 * blk), 1)

    chains = [(j, half) for j in range(2) for half in range(2)]

    def block_scores(i):
        r0 = i * blk
        cos_q = cos_ref[0, pl.ds(r0, blk), :]
        sin_q = sin_ref[0, pl.ds(r0, blk), :]
        q_tiles = []
        for t in range(ATTN_Q_TILES):
            qt = q_ref[0, pl.ds(r0, blk), t * V7X_LANES:(t + 1) * V7X_LANES]
            q_tiles.append((_rope(qt, cos_q, sin_q, first_half_q) * scale).astype(BF16))
        scores = []
        for j, half in chains:
            qcat = jnp.concatenate([q_tiles[2 * j], q_tiles[2 * j + 1]], axis=0)
            kk = kp_ref[2 * j + half, pl.ds(r0, 3 * blk), :]
            scores.append(lax.dot_general(kk, qcat, _CONTRACT_LAST, preferred_element_type=F32))
        return scores

    def block_probs(i, scores):
        bias_prev = bias_ref[1 if i == 0 else 0]
        bias_next = bias_ref[3 if i == nb - 1 else 2]
        probs = []
        for (j, half), st in zip(chains, scores):
            head = hp * ATTN_Q_HEADS_PER_PAIR + ATTN_GROUP * j + half
            snk = jnp.where(lane2 < blk, sink_ref[head], sink_ref[head + 2]) * LOG2_E
            parts = (st[0:blk] + bias_prev, st[blk:2 * blk], st[2 * blk:] + bias_next)
            m = snk
            for part in parts:
                m = jnp.maximum(m, jnp.max(part, axis=0, keepdims=True))
            p = jnp.concatenate([jnp.exp2(part - m) for part in parts], axis=0)
            probs.append((p.astype(BF16), jnp.exp2(snk - m)))
        return probs

    def block_output(i, probs):
        r0 = i * blk
        outs = []
        for (j, half), (p, sink_p) in zip(chains, probs):
            vtw = vt_ref[j, :, pl.ds(r0, 3 * blk)]
            ot = jnp.dot(vtw, p, preferred_element_type=F32)
            outs.append(ot[0:hd] * (1.0 / (ot[hd:hd + 1] + sink_p)))
        for t in range(ATTN_Q_TILES):
            j, tt = divmod(t, 2)
            cols = slice(tt * blk, (tt + 1) * blk)
            tile_t = jnp.concatenate([outs[2 * j][:, cols], outs[2 * j + 1][:, cols]], axis=0)
            o_ref[0, pl.ds(r0, blk), t * V7X_LANES:(t + 1) * V7X_LANES] = jnp.transpose(tile_t).astype(o_ref.dtype)

    scores = block_scores(0)
    for i in range(nb):
        probs = block_probs(i, scores)
        if i + 1 < nb:
            scores = block_scores(i + 1)
        block_output(i, probs)


def _attn_call(proj, cos, sin, sink, bsz, seq):
    n_pairs = ATTN_KV_HEADS // 2
    qw = ATTN_Q_TILES * V7X_LANES
    k_off = (proj.shape[-1] - 2 * ATTN_KV_HEADS * ATTN_HEAD_DIM) // V7X_LANES
    v_off = k_off + n_pairs
    assert seq // ATTN_BLOCK >= 2 and ATTN_BLOCK == V7X_LANES and 2 * ATTN_HEAD_DIM == V7X_LANES
    assert WINDOW == ATTN_BLOCK
    pad_seq = seq + 2 * ATTN_BLOCK
    tbl = pl.BlockSpec((1, seq, V7X_LANES), lambda b, p: (b, 0, 0))
    vmem = (2 * seq * (qw * 4 + 4 * V7X_LANES * 4 + qw * 2) + 4 * pad_seq * V7X_LANES * 2
            + 2 * ATTN_VT_ROWS * pad_seq * 2 + 3 * 3 * ATTN_BLOCK * 2 * ATTN_BLOCK * 4 + 6 * seq * V7X_LANES * 4)
    return pl.pallas_call(
        functools.partial(_attn_kernel, seq=seq),
        grid=(bsz, n_pairs),
        in_specs=[
            pl.BlockSpec(memory_space=pltpu.SMEM),
            pl.BlockSpec((1, seq, qw), lambda b, p: (b, 0, p)),
            pl.BlockSpec((1, seq, V7X_LANES), lambda b, p: (b, 0, k_off + p)),
            pl.BlockSpec((1, seq, V7X_LANES), lambda b, p: (b, 0, v_off + p)),
            tbl, tbl,
        ],
        out_specs=pl.BlockSpec((1, seq, qw), lambda b, p: (b, 0, p)),
        out_shape=jax.ShapeDtypeStruct((bsz, seq, n_pairs * qw), BF16),
        scratch_shapes=[
            pltpu.VMEM((4, pad_seq, V7X_LANES), BF16),
            pltpu.VMEM((2, ATTN_VT_ROWS, pad_seq), BF16),
            pltpu.VMEM((4, ATTN_BLOCK, 2 * ATTN_BLOCK), F32),
        ],
        compiler_params=_params(("arbitrary", "arbitrary"), vmem),
        name="swa_sink_attention",
    )(sink, proj, proj, proj, cos, sin)


def _gates_kernel(h_ref, w_ref, b_ref, o_ref):
    g_t = lax.dot_general(w_ref[...].astype(BF16), h_ref[...], _CONTRACT_LAST, preferred_element_type=F32)
    o_ref[0] = g_t + b_ref[...]


def _gates_call(h, w_t, layer, b_col, bsz, seq):
    m, k = h.shape
    tm = GATES_ROW_TILE
    tpb = seq // tm
    gate_blk = (w_t.shape[1] - MLSTM_GATES) // MLSTM_GATES
    return pl.pallas_call(
        _gates_kernel,
        grid=(m // tm,),
        in_specs=[
            pl.BlockSpec((tm, k), lambda i: (i, 0)),
            pl.BlockSpec((None, MLSTM_GATES, k), lambda i: (layer, gate_blk, 0)),
            pl.BlockSpec((MLSTM_GATES, 1), lambda i: (0, 0)),
        ],
        out_specs=pl.BlockSpec((1, MLSTM_GATES, tm), lambda i: (i // tpb, 0, i % tpb)),
        out_shape=jax.ShapeDtypeStruct((bsz, MLSTM_GATES, seq), F32),
        compiler_params=_params(("arbitrary",), 2 * tm * k * 2 + 4 * MLSTM_GATES * (k + tm) * 4),
        name="mlstm_gates",
    )(h, w_t, b_col)


def _log_sigmoid(x):
    return jnp.minimum(x, 0.0) - jnp.log1p(jnp.exp(-jnp.abs(x)))


def _mlstm_kernel(q_ref, k_ref, v_ref, g_ref, o_ref, qb_ref, kt_ref, vx_ref, c_ref, *, seq, chunk, dk, dv):
    nh = MLSTM_HEADS
    head = pl.program_id(1)
    nc = seq // chunk
    qb_ref[...] = (q_ref[0] * (dk ** -0.5)).astype(BF16)
    kt_ref[...] = jnp.transpose(k_ref[0])
    vx_ref[:, :dv] = v_ref[0].astype(BF16)
    ones_lane = lax.broadcasted_iota(jnp.int32, (seq, V7X_LANES), 1) == 0
    vx_ref[:, dv:] = jnp.where(ones_lane, 1.0, 0.0).astype(BF16)

    ti = lax.broadcasted_iota(jnp.int32, (chunk, chunk), 0)
    ui = lax.broadcasted_iota(jnp.int32, (chunk, chunk), 1)
    eye = ti == ui
    for direction in range(2):
        seen = (ui <= ti) if direction == 0 else (ui >= ti)
        c_ref[...] = jnp.zeros_like(c_ref)
        m_prev = jnp.full((1, 1), -1e30, F32)
        order = range(nc) if direction == 0 else range(nc - 1, -1, -1)
        for c in order:
            rows = slice(c * chunk, (c + 1) * chunk)
            gate_row = 2 * direction * nh + head
            li = g_ref[0, pl.ds(gate_row, 1), rows]
            lf = _log_sigmoid(g_ref[0, pl.ds(gate_row + nh, 1), rows])
            g_col = jnp.sum(jnp.where(seen, lf, 0.0), axis=1, keepdims=True)
            g_row = jnp.sum(jnp.where(eye, g_col, 0.0), axis=0, keepdims=True)
            g_tot = jnp.sum(lf, axis=1, keepdims=True)
            dm = jnp.where(seen, g_col - g_row + li, -jnp.inf)
            a = g_col + m_prev
            m_t = jnp.maximum(a, jnp.max(dm, axis=1, keepdims=True))
            p = jnp.exp(dm - m_t)
            ea = jnp.exp(a - m_t)

            qc = qb_ref[rows, :]
            ktc = kt_ref[:, rows]
            vxc = vx_ref[rows, :]
            sqk = jnp.dot(qc, ktc.astype(BF16), preferred_element_type=F32)
            sc = (sqk * p).astype(BF16)
            tot = (ea * jnp.dot(qc, c_ref[...].astype(BF16), preferred_element_type=F32)
                   + jnp.dot(sc, vxc, preferred_element_type=F32))
            den = tot[:, dv:dv + 1]
            hh = tot[:, :dv] * (1.0 / jnp.maximum(jnp.abs(den), jnp.exp(-m_t)))
            if direction == 0:
                o_ref[0, rows, :] = hh
            else:
                o_ref[0, rows, :] += hh

            w_log = g_tot - g_row + li
            m_new = jnp.maximum(g_tot + m_prev, jnp.max(w_log, axis=1, keepdims=True))
            decay = jnp.exp(g_tot + m_prev - m_new)
            w = jnp.exp(w_log - m_new)
            c_ref[...] = decay * c_ref[...] + jnp.dot((ktc * w).astype(BF16), vxc, preferred_element_type=F32)
            m_prev = m_new


def _mlstm_call(proj, gates_t, bsz, seq, dk, dv):
    nh = MLSTM_HEADS
    chunk = MLSTM_KERNEL_CHUNK
    k_blk = nh
    v_blk = (2 * nh * dk) // dv
    dvx = dv + V7X_LANES
    vmem = (2 * seq * (2 * dk + 2 * dv) * 4 + 2 * MLSTM_GATES * seq * 4
            + seq * dk * 2 + dk * seq * 4 + seq * dvx * 2 + dk * dvx * 4 + 8 * chunk * dvx * 4)
    return pl.pallas_call(
        functools.partial(_mlstm_kernel, seq=seq, chunk=chunk, dk=dk, dv=dv),
        grid=(bsz, nh),
        in_specs=[
            pl.BlockSpec((1, seq, dk), lambda b, h: (b, 0, h)),
            pl.BlockSpec((1, seq, dk), lambda b, h: (b, 0, k_blk + h)),
            pl.BlockSpec((1, seq, dv), lambda b, h: (b, 0, v_blk + h)),
            pl.BlockSpec((1, MLSTM_GATES, seq), lambda b, h: (b, 0, 0)),
        ],
        out_specs=pl.BlockSpec((1, seq, dv), lambda b, h: (b, 0, h)),
        out_shape=jax.ShapeDtypeStruct((bsz, seq, nh * dv), F32),
        scratch_shapes=[
            pltpu.VMEM((seq, dk), BF16),
            pltpu.VMEM((dk, seq), F32),
            pltpu.VMEM((seq, dvx), BF16),
            pltpu.VMEM((dk, dvx), F32),
        ],
        compiler_params=_params(("arbitrary", "arbitrary"), vmem),
        name="bidir_mlstm",
    )(proj, proj, proj, gates_t)


def kernel(x, c, positions, attn_w_qkv, attn_b_qkv, attn_sink, attn_w_o, attn_b_o, mlstm_w_in, mlstm_b_in,
           mlstm_norm_w, mlstm_w_o, mlstm_b_o, mod_w, mod_b, mlp_w1, mlp_b1, mlp_w2, mlp_b2,
           ln_mix_g, ln_mix_b, ln_mlp_g, ln_mlp_b):
    bsz, seq, d = x.shape
    depth = mod_w.shape[0]
    assert depth == DEPTH
    m = bsz * seq
    d_ff = mlp_w1.shape[-1]
    ml_main = mlstm_w_in.shape[-1] - MLSTM_GATES
    ml_dv = mlstm_w_o.shape[1] // MLSTM_HEADS
    ml_dk = (ml_main - 2 * MLSTM_HEADS * ml_dv) // (2 * MLSTM_HEADS)

    c_pad = jnp.pad(c, ((0, V7X_SUBLANES - bsz), (0, 0)))
    mod = _mod_call(c_pad, mod_w, mod_b)[:, :bsz]
    mod = mod.reshape(depth, bsz, 6, 1, d)
    sh_m, sc_m, g_m, sh_f, sc_f, g_f = (mod[:, :, j] for j in range(6))

    cos, sin = _rope_call(positions)
    x2 = x.reshape(m, d)
    h = _modulate_call(x2, sc_m[0], sh_m[0], seq)

    mlstm_w_in_t = jnp.swapaxes(mlstm_w_in, 1, 2)

    for i in range(depth):
        j = i // 2
        if i % 2 == 0:
            proj, w_o_bf = _mm_call(h, attn_w_qkv, j, attn_b_qkv[j], n_cols=attn_w_qkv.shape[-1], out_dtype=F32,
                                    side=(attn_w_o, j), name="attn_qkv_proj")
            a = _attn_call(proj.reshape(bsz, seq, -1), cos, sin, attn_sink[j], bsz, seq).reshape(m, d)
            x2, h = _mm_ln_call(a, w_o_bf, attn_b_o[j], x2, g_m[i], ln_mix_g[i], ln_mix_b[i],
                                (sc_f[i], sh_f[i]), rows_per_batch=seq, tm=LN_PROJ_ROW_TILE, name="attn_out_ln")
        else:
            proj, w_o_bf = _mm_call(h, mlstm_w_in_t, j, mlstm_b_in[j], n_cols=ml_main, out_dtype=F32,
                                    w_transposed=True, side=(mlstm_w_o, j), name="mlstm_in_proj")
            gates_t = _gates_call(h, mlstm_w_in_t, j, mlstm_b_in[j][ml_main:].reshape(MLSTM_GATES, 1), bsz, seq)
            hs = _mlstm_call(proj.reshape(bsz, seq, ml_main), gates_t, bsz, seq, ml_dk, ml_dv).reshape(m, d)
            x2, h = _mm_ln_call(hs, w_o_bf, mlstm_b_o[j], x2, g_m[i], ln_mix_g[i], ln_mix_b[i],
                                (sc_f[i], sh_f[i]), rows_per_batch=seq, tm=LN_PROJ_ROW_TILE, name="mlstm_out_ln",
                                ogate=(proj, (ml_main - d) // d), normw=mlstm_norm_w[j])
        u, w2_bf = _mm_call(h, mlp_w1, i, mlp_b1[i], n_cols=d_ff, out_dtype=BF16, act="relu2", side=(mlp_w2, i),
                            name="mlp_up")
        nxt = (sc_m[i + 1], sh_m[i + 1]) if i + 1 < depth else None
        x2, h = _mm_ln_call(u, w2_bf, mlp_b2[i], x2, g_f[i], ln_mlp_g[i], ln_mlp_b[i], nxt,
                            rows_per_batch=seq, tm=MLP_DOWN_ROW_TILE, name="mlp_down_ln")
    return x2.reshape(bsz, seq, d)
```

```python
import functools

import numpy as np
import jax
import jax.numpy as jnp
from jax import lax
from jax.experimental import pallas as pl
from jax.experimental.pallas import tpu as pltpu

F32 = jnp.float32
BF16 = jnp.bfloat16

DEPTH = 2
ATTN_HEAD_DIM = 64
ATTN_KV_HEADS = 8
WINDOW = 128
ATTN_BLOCK = 128
ROPE_THETA = 10000.0
MLSTM_HEADS = 4
MLSTM_GATES = 4 * MLSTM_HEADS
DEEPNORM_ALPHA = (2.0 * DEPTH) ** 0.25
LN_EPS = 1e-5
HEAD_NORM_EPS = 1e-6
MLSTM_KERNEL_CHUNK = 256
LOG2_E = 1.4426950408889634

V7X_LANES = 128
V7X_SUBLANES = 8
V7X_VMEM_BYTES = 64 * 1024 * 1024
VMEM_LIMIT_CAP = 60000 * 1024

MOD_COL_TILE = 1024
ELEMENTWISE_ROW_TILE = 2048
MM_ROW_TILE = 1024
MM_COL_TILE = 1024
LN_PROJ_ROW_TILE = 512
MLP_DOWN_ROW_TILE = 256
GATES_ROW_TILE = 1024

ATTN_GROUP = 4
ATTN_Q_TILES = 2 * ATTN_GROUP * ATTN_HEAD_DIM // V7X_LANES
ATTN_Q_HEADS_PER_PAIR = 2 * ATTN_GROUP
VMEM_LIMIT_FLOOR = V7X_VMEM_BYTES - 28 * 1024 * 1024


def _params(semantics, vmem_bytes):
    limit = max(min(int(vmem_bytes * 1.25) + (4 << 20), VMEM_LIMIT_CAP), VMEM_LIMIT_FLOOR)
    return pltpu.CompilerParams(dimension_semantics=semantics, vmem_limit_bytes=limit)


def _mod_kernel(c_ref, w_ref, b_ref, o_ref):
    c = c_ref[...]
    c_act = (c * jax.nn.sigmoid(c)).astype(BF16)
    o_ref[0] = jnp.dot(c_act, w_ref[0].astype(BF16), preferred_element_type=F32) + b_ref[0]


def _mod_call(c_pad, mod_w, mod_b):
    depth, d, n = mod_w.shape
    rows = c_pad.shape[0]
    tn = MOD_COL_TILE
    return pl.pallas_call(
        _mod_kernel,
        grid=(depth, n // tn),
        in_specs=[
            pl.BlockSpec((rows, d), lambda i, j: (0, 0)),
            pl.BlockSpec((1, d, tn), lambda i, j: (i, 0, j)),
            pl.BlockSpec((1, 1, tn), lambda i, j: (i, 0, j)),
        ],
        out_specs=pl.BlockSpec((1, rows, tn), lambda i, j: (i, 0, j)),
        out_shape=jax.ShapeDtypeStruct((depth, rows, n), F32),
        compiler_params=_params(("arbitrary", "arbitrary"), 2 * d * tn * 4 + d * tn * 2),
        name="adaln_mod",
    )(c_pad, mod_w, mod_b.reshape(depth, 1, n))


def _modulate_kernel(x_ref, sc_ref, sh_ref, o_ref):
    o_ref[...] = (x_ref[...] * (1.0 + sc_ref[0]) + sh_ref[0]).astype(o_ref.dtype)


def _modulate_call(x2, sc, sh, rows_per_batch):
    m, d = x2.shape
    tm = ELEMENTWISE_ROW_TILE
    tpb = rows_per_batch // tm
    vec = pl.BlockSpec((1, 1, d), lambda i: (i // tpb, 0, 0))
    return pl.pallas_call(
        _modulate_kernel,
        grid=(m // tm,),
        in_specs=[pl.BlockSpec((tm, d), lambda i: (i, 0)), vec, vec],
        out_specs=pl.BlockSpec((tm, d), lambda i: (i, 0)),
        out_shape=jax.ShapeDtypeStruct((m, d), BF16),
        compiler_params=_params(("arbitrary",), 2 * tm * d * 6),
        name="modulate_in",
    )(x2, sc, sh)


_CONTRACT_LAST = (((1,), (1,)), ((), ()))


def _mm_kernel(*refs, act, w_transposed, side_cast, row_splits):
    if side_cast:
        x_ref, w_ref, b_ref, side_ref, o_ref, side_o_ref, wb_ref = refs
        side_o_ref[...] = side_ref[...].astype(side_o_ref.dtype)
    else:
        x_ref, w_ref, b_ref, o_ref, wb_ref = refs

    @pl.when(pl.program_id(1) == 0)
    def _():
        wb_ref[...] = w_ref[...].astype(BF16)

    for r in range(row_splits):
        rows = slice(r * MM_ROW_TILE, (r + 1) * MM_ROW_TILE)
        if w_transposed:
            acc = lax.dot_general(x_ref[rows, :], wb_ref[...], _CONTRACT_LAST, preferred_element_type=F32)
        else:
            acc = jnp.dot(x_ref[rows, :], wb_ref[...], preferred_element_type=F32)
        acc = acc + b_ref[...]
        if act == "relu2":
            acc = jnp.square(jnp.maximum(acc, 0.0))
        o_ref[rows, :] = acc.astype(o_ref.dtype)


def _mm_call(x, w, layer, b, *, n_cols, out_dtype, act=None, w_transposed=False, side=None, row_splits=1, name):
    m, k = x.shape
    tm, tn = MM_ROW_TILE * row_splits, MM_COL_TILE
    osz = jnp.dtype(out_dtype).itemsize
    vmem = 2 * k * tn * 4 + k * tn * 2 + 2 * tm * k * 2 + 2 * tm * tn * osz + MM_ROW_TILE * tn * 4
    if w_transposed:
        w_spec = pl.BlockSpec((None, tn, k), lambda j, i: (layer, j, 0))
        w_tile = (tn, k)
    else:
        w_spec = pl.BlockSpec((None, k, tn), lambda j, i: (layer, 0, j))
        w_tile = (k, tn)
    n_i = m // tm
    grid = (n_cols // tn, n_i)
    in_specs = [pl.BlockSpec((tm, k), lambda j, i: (i, 0)), w_spec, pl.BlockSpec((1, tn), lambda j, i: (0, j))]
    args = [x, w, b.reshape(1, -1)]
    out_specs = [pl.BlockSpec((tm, tn), lambda j, i: (i, j))]
    out_shape = [jax.ShapeDtypeStruct((m, n_cols), out_dtype)]
    if side is not None:
        w2, layer2 = side
        _, k2, n2 = w2.shape
        n_blk = 1 << ((grid[0] * grid[1]).bit_length() - 1)
        rows = k2 // n_blk
        assert rows * n_blk == k2 and rows % 16 == 0
        in_specs.append(pl.BlockSpec((None, rows, n2), lambda j, i: (layer2, jnp.minimum(j * n_i + i, n_blk - 1), 0)))
        args.append(w2)
        out_specs.append(pl.BlockSpec((rows, n2), lambda j, i: (jnp.minimum(j * n_i + i, n_blk - 1), 0)))
        out_shape.append(jax.ShapeDtypeStruct((k2, n2), BF16))
        vmem += 2 * rows * n2 * 6
    outs = pl.pallas_call(
        functools.partial(_mm_kernel, act=act, w_transposed=w_transposed, side_cast=side is not None,
                          row_splits=row_splits),
        grid=grid,
        in_specs=in_specs,
        out_specs=out_specs,
        out_shape=out_shape,
        scratch_shapes=[pltpu.VMEM(w_tile, BF16)],
        compiler_params=_params(("arbitrary", "arbitrary"), vmem),
        name=name,
    )(*args)
    return (outs[0], outs[1]) if side is not None else outs[0]


def _head_norm_gate(hs, og, nw, heads):
    dv = hs.shape[-1] // heads
    outs = []
    for h in range(heads):
        t = hs[:, h * dv:(h + 1) * dv]
        mu = jnp.mean(t, axis=-1, keepdims=True)
        var = jnp.mean(jnp.square(t - mu), axis=-1, keepdims=True)
        outs.append((t - mu) * lax.rsqrt(var + HEAD_NORM_EPS))
    hn = jnp.concatenate(outs, axis=-1) * nw
    return jax.nn.sigmoid(og) * hn


def _mm_ln_kernel(*refs, mlstm_prologue, emit_h):
    refs = list(refs)
    a_ref = refs.pop(0)
    if mlstm_prologue:
        og_ref = refs.pop(0)
        nw_ref = refs.pop(0)
    w_ref, b_ref, res_ref, gate_ref, lng_ref, lnb_ref = refs[:6]
    refs = refs[6:]
    if emit_h:
        scn_ref, shn_ref = refs[:2]
        refs = refs[2:]
    xo_ref = refs.pop(0)
    if emit_h:
        ho_ref = refs.pop(0)

    if mlstm_prologue:
        a = _head_norm_gate(a_ref[...], og_ref[...], nw_ref[...], MLSTM_HEADS).astype(BF16)
    else:
        a = a_ref[...]
    y = jnp.dot(a, w_ref[...], preferred_element_type=F32) + b_ref[...]
    z = DEEPNORM_ALPHA * res_ref[...] + (1.0 + gate_ref[0]) * y
    mu = jnp.mean(z, axis=-1, keepdims=True)
    var = jnp.mean(jnp.square(z - mu), axis=-1, keepdims=True)
    xn = (z - mu) * lax.rsqrt(var + LN_EPS) * lng_ref[...] + lnb_ref[...]
    xo_ref[...] = xn
    if emit_h:
        ho_ref[...] = (xn * (1.0 + scn_ref[0]) + shn_ref[0]).astype(ho_ref.dtype)


def _mm_ln_call(a, w_bf, b, res, gate, lng, lnb, nxt, *, rows_per_batch, tm, name, ogate=None, normw=None):
    m = a.shape[0]
    k, d = w_bf.shape
    tpb = rows_per_batch // tm
    mlstm_prologue = ogate is not None
    emit_h = nxt is not None
    vec_b = pl.BlockSpec((1, 1, d), lambda i: (i // tpb, 0, 0))
    vec = pl.BlockSpec((1, d), lambda i: (0, 0))
    row_tile = pl.BlockSpec((tm, d), lambda i: (i, 0))

    in_specs = [pl.BlockSpec((tm, k), lambda i: (i, 0))]
    args = [a]
    asz = a.dtype.itemsize
    if mlstm_prologue:
        ocb = ogate[1]
        in_specs += [pl.BlockSpec((tm, k), lambda i: (i, ocb)), pl.BlockSpec((1, k), lambda i: (0, 0))]
        args += [ogate[0], normw.reshape(1, -1)]
    in_specs += [pl.BlockSpec((k, d), lambda i: (0, 0), pipeline_mode=pl.Buffered(1)),
                 vec, row_tile, vec_b, vec, vec]
    args += [w_bf, b.reshape(1, d), res, gate, lng.reshape(1, d), lnb.reshape(1, d)]
    out_specs = [row_tile]
    out_shape = [jax.ShapeDtypeStruct((m, d), F32)]
    if emit_h:
        in_specs += [vec_b, vec_b]
        args += [nxt[0], nxt[1]]
        out_specs.append(row_tile)
        out_shape.append(jax.ShapeDtypeStruct((m, d), BF16))
    vmem = (k * d * 2 + 2 * tm * k * asz * (2 if mlstm_prologue else 1)
            + 2 * tm * d * (4 + 4 + 2) + 2 * tm * d * 4)
    outs = pl.pallas_call(
        functools.partial(_mm_ln_kernel, mlstm_prologue=mlstm_prologue, emit_h=emit_h),
        grid=(m // tm,),
        in_specs=in_specs,
        out_specs=out_specs,
        out_shape=out_shape,
        compiler_params=_params(("arbitrary",), vmem),
        name=name,
    )(*args)
    return (outs[0], outs[1]) if emit_h else (outs[0], None)


def _rope_kernel(pos_ref, freq_ref, cos_ref, sin_ref):
    ang = pos_ref[0].astype(F32) * freq_ref[...]
    lane = lax.broadcasted_iota(jnp.int32, ang.shape, 1)
    first_half = (lane % ATTN_HEAD_DIM) < ATTN_HEAD_DIM // 2
    cos_ref[0] = jnp.cos(ang)
    s = jnp.sin(ang)
    sin_ref[0] = jnp.where(first_half, -s, s)


def _rope_call(positions):
    bsz, s = positions.shape
    half = ATTN_HEAD_DIM // 2
    inv_freq = 1.0 / (ROPE_THETA ** (np.arange(0, ATTN_HEAD_DIM, 2, dtype=np.float32) / ATTN_HEAD_DIM))
    freq = jnp.asarray(np.tile(inv_freq.astype(np.float32), V7X_LANES // half).reshape(1, V7X_LANES))
    tbl = jax.ShapeDtypeStruct((bsz, s, V7X_LANES), F32)
    return pl.pallas_call(
        _rope_kernel,
        grid=(bsz,),
        in_specs=[pl.BlockSpec((1, s, 1), lambda b: (b, 0, 0)), pl.BlockSpec((1, V7X_LANES), lambda b: (0, 0))],
        out_specs=[pl.BlockSpec((1, s, V7X_LANES), lambda b: (b, 0, 0))] * 2,
        out_shape=[tbl, tbl],
        compiler_params=_params(("arbitrary",), 6 * s * V7X_LANES * 4),
        name="rope_tables",
    )(positions.reshape(bsz, s, 1), freq)


def _rope(t, cos, sin_signed, first_half):
    half = ATTN_HEAD_DIM // 2
    rot = jnp.where(first_half, pltpu.roll(t, V7X_LANES - half, 1), pltpu.roll(t, half, 1))
    return t * cos + rot * sin_signed


ATTN_VT_ROWS = ATTN_HEAD_DIM + 16


def _attn_kernel(sink_ref, q_ref, k_ref, v_ref, cos_ref, sin_ref, o_ref, kp_ref, vt_ref, bias_ref, *, seq):
    blk = ATTN_BLOCK
    hd = ATTN_HEAD_DIM
    nb = seq // blk
    hp = pl.program_id(1)
    lane = lax.broadcasted_iota(jnp.int32, (seq, V7X_LANES), 1)
    low = lane < hd
    first_half = (lane % hd) < hd // 2
    cos = cos_ref[0]
    sin = sin_ref[0]

    scale = hd ** -0.5 * LOG2_E
    lane_q = lax.broadcasted_iota(jnp.int32, (blk, V7X_LANES), 1)
    first_half_q = (lane_q % hd) < hd // 2

    kr = _rope(k_ref[0], cos, sin, first_half)
    lo = jnp.where(low, kr, 0.0)
    hi = jnp.where(low, 0.0, kr)
    zeros_pad = jnp.zeros((blk, V7X_LANES), BF16)
    for idx, val in enumerate((lo, pltpu.roll(lo, hd, 1), pltpu.roll(hi, hd, 1), hi)):
        kp_ref[idx, 0:blk, :] = zeros_pad
        kp_ref[idx, blk:blk + seq, :] = val.astype(BF16)
        kp_ref[idx, blk + seq:, :] = zeros_pad

    vt = jnp.transpose(v_ref[0])
    ones_row = lax.broadcasted_iota(jnp.int32, (ATTN_VT_ROWS - hd, seq), 0) == 0
    zeros_vt = jnp.zeros((ATTN_VT_ROWS, blk), BF16)
    for j in range(2):
        vt_ref[j, :, 0:blk] = zeros_vt
        vt_ref[j, 0:hd, blk:blk + seq] = vt[j * hd:(j + 1) * hd].astype(BF16)
        vt_ref[j, hd:, blk:blk + seq] = jnp.where(ones_row, 1.0, 0.0).astype(BF16)
        vt_ref[j, :, blk + seq:] = zeros_vt

    ci = lax.broadcasted_iota(jnp.int32, (blk, 2 * blk), 0)
    ri = lax.broadcasted_iota(jnp.int32, (blk, 2 * blk), 1) % blk
    neg = jnp.full((blk, 2 * blk), -jnp.inf, F32)
    bias_ref[0] = jnp.where(ci >= ri, 0.0, -jnp.inf)
    bias_ref[1] = neg
    bias_ref[2] = jnp.where(ci <= ri, 0.0, -jnp.inf)
    bias_ref[3] = neg
    lane2 = lax.broadcasted_iota(jnp.int32, (1, 2 * blk), 1)

    chains = [(j, half) for j in range(2) for half in range(2)]

    def block_scores(i):
        r0 = i * blk
        cos_q = cos_ref[0, pl.ds(r0, blk), :]
        sin_q = sin_ref[0, pl.ds(r0, blk), :]
        q_tiles = []
        for t in range(ATTN_Q_TILES):
            qt = q_ref[0, pl.ds(r0, blk), t * V7X_LANES:(t + 1) * V7X_LANES]
            q_tiles.append((_rope(qt, cos_q, sin_q, first_half_q) * scale).astype(BF16))
        scores = []
        for j, half in chains:
            qcat = jnp.concatenate([q_tiles[2 * j], q_tiles[2 * j + 1]], axis=0)
            kk = kp_ref[2 * j + half, pl.ds(r0, 3 * blk), :]
            scores.append(lax.dot_general(kk, qcat, _CONTRACT_LAST, preferred_element_type=F32))
        return scores

    def block_probs(i, scores):
        bias_prev = bias_ref[1 if i == 0 else 0]
        bias_next = bias_ref[3 if i == nb - 1 else 2]
        probs = []
        for (j, half), st in zip(chains, scores):
            head = hp * ATTN_Q_HEADS_PER_PAIR + ATTN_GROUP * j + half
            snk = jnp.where(lane2 < blk, sink_ref[head], sink_ref[head + 2]) * LOG2_E
            parts = (st[0:blk] + bias_prev, st[blk:2 * blk], st[2 * blk:] + bias_next)
            m = snk
            for part in parts:
                m = jnp.maximum(m, jnp.max(part, axis=0, keepdims=True))
            p = jnp.concatenate([jnp.exp2(part - m) for part in parts], axis=0)
            probs.append((p.astype(BF16), jnp.exp2(snk - m)))
        return probs

    def block_output(i, probs):
        r0 = i * blk
        outs = []
        for (j, half), (p, sink_p) in zip(chains, probs):
            vtw = vt_ref[j, :, pl.ds(r0, 3 * blk)]
            ot = jnp.dot(vtw, p, preferred_element_type=F32)
            outs.append(ot[0:hd] * (1.0 / (ot[hd:hd + 1] + sink_p)))
        for t in range(ATTN_Q_TILES):
            j, tt = divmod(t, 2)
            cols = slice(tt * blk, (tt + 1) * blk)
            tile_t = jnp.concatenate([outs[2 * j][:, cols], outs[2 * j + 1][:, cols]], axis=0)
            o_ref[0, pl.ds(r0, blk), t * V7X_LANES:(t + 1) * V7X_LANES] = jnp.transpose(tile_t).astype(o_ref.dtype)

    scores = block_scores(0)
    for i in range(nb):
        probs = block_probs(i, scores)
        if i + 1 < nb:
            scores = block_scores(i + 1)
        block_output(i, probs)


def _attn_call(proj, cos, sin, sink, bsz, seq):
    n_pairs = ATTN_KV_HEADS // 2
    qw = ATTN_Q_TILES * V7X_LANES
    k_off = (proj.shape[-1] - 2 * ATTN_KV_HEADS * ATTN_HEAD_DIM) // V7X_LANES
    v_off = k_off + n_pairs
    assert seq // ATTN_BLOCK >= 2 and ATTN_BLOCK == V7X_LANES and 2 * ATTN_HEAD_DIM == V7X_LANES
    assert WINDOW == ATTN_BLOCK
    pad_seq = seq + 2 * ATTN_BLOCK
    tbl = pl.BlockSpec((1, seq, V7X_LANES), lambda b, p: (b, 0, 0))
    vmem = (2 * seq * (qw * 4 + 4 * V7X_LANES * 4 + qw * 2) + 4 * pad_seq * V7X_LANES * 2
            + 2 * ATTN_VT_ROWS * pad_seq * 2 + 3 * 3 * ATTN_BLOCK * 2 * ATTN_BLOCK * 4 + 6 * seq * V7X_LANES * 4)
    return pl.pallas_call(
        functools.partial(_attn_kernel, seq=seq),
        grid=(bsz, n_pairs),
        in_specs=[
            pl.BlockSpec(memory_space=pltpu.SMEM),
            pl.BlockSpec((1, seq, qw), lambda b, p: (b, 0, p)),
            pl.BlockSpec((1, seq, V7X_LANES), lambda b, p: (b, 0, k_off + p)),
            pl.BlockSpec((1, seq, V7X_LANES), lambda b, p: (b, 0, v_off + p)),
            tbl, tbl,
        ],
        out_specs=pl.BlockSpec((1, seq, qw), lambda b, p: (b, 0, p)),
        out_shape=jax.ShapeDtypeStruct((bsz, seq, n_pairs * qw), BF16),
        scratch_shapes=[
            pltpu.VMEM((4, pad_seq, V7X_LANES), BF16),
            pltpu.VMEM((2, ATTN_VT_ROWS, pad_seq), BF16),
            pltpu.VMEM((4, ATTN_BLOCK, 2 * ATTN_BLOCK), F32),
        ],
        compiler_params=_params(("arbitrary", "arbitrary"), vmem),
        name="swa_sink_attention",
    )(sink, proj, proj, proj, cos, sin)


def _gates_kernel(h_ref, w_ref, b_ref, o_ref):
    g_t = lax.dot_general(w_ref[...].astype(BF16), h_ref[...], _CONTRACT_LAST, preferred_element_type=F32)
    o_ref[0] = g_t + b_ref[...]


def _gates_call(h, w_t, layer, b_col, bsz, seq):
    m, k = h.shape
    tm = GATES_ROW_TILE
    tpb = seq // tm
    gate_blk = (w_t.shape[1] - MLSTM_GATES) // MLSTM_GATES
    return pl.pallas_call(
        _gates_kernel,
        grid=(m // tm,),
        in_specs=[
            pl.BlockSpec((tm, k), lambda i: (i, 0)),
            pl.BlockSpec((None, MLSTM_GATES, k), lambda i: (layer, gate_blk, 0)),
            pl.BlockSpec((MLSTM_GATES, 1), lambda i: (0, 0)),
        ],
        out_specs=pl.BlockSpec((1, MLSTM_GATES, tm), lambda i: (i // tpb, 0, i % tpb)),
        out_shape=jax.ShapeDtypeStruct((bsz, MLSTM_GATES, seq), F32),
        compiler_params=_params(("arbitrary",), 2 * tm * k * 2 + 4 * MLSTM_GATES * (k + tm) * 4),
        name="mlstm_gates",
    )(h, w_t, b_col)


def _log_sigmoid(x):
    return jnp.minimum(x, 0.0) - jnp.log1p(jnp.exp(-jnp.abs(x)))


def _mlstm_kernel(q_ref, k_ref, v_ref, g_ref, o_ref, qb_ref, kt_ref, vx_ref, c_ref, *, seq, chunk, dk, dv):
    nh = MLSTM_HEADS
    head = pl.program_id(1)
    nc = seq // chunk
    qb_ref[...] = (q_ref[0] * (dk ** -0.5)).astype(BF16)
    kt_ref[...] = jnp.transpose(k_ref[0])
    vx_ref[:, :dv] = v_ref[0].astype(BF16)
    ones_lane = lax.broadcasted_iota(jnp.int32, (seq, V7X_LANES), 1) == 0
    vx_ref[:, dv:] = jnp.where(ones_lane, 1.0, 0.0).astype(BF16)

    ti = lax.broadcasted_iota(jnp.int32, (chunk, chunk), 0)
    ui = lax.broadcasted_iota(jnp.int32, (chunk, chunk), 1)
    eye = ti == ui
    for direction in range(2):
        seen = (ui <= ti) if direction == 0 else (ui >= ti)
        c_ref[...] = jnp.zeros_like(c_ref)
        m_prev = jnp.full((1, 1), -1e30, F32)
        order = range(nc) if direction == 0 else range(nc - 1, -1, -1)
        for c in order:
            rows = slice(c * chunk, (c + 1) * chunk)
            gate_row = 2 * direction * nh + head
            li = g_ref[0, pl.ds(gate_row, 1), rows]
            lf = _log_sigmoid(g_ref[0, pl.ds(gate_row + nh, 1), rows])
            g_col = jnp.sum(jnp.where(seen, lf, 0.0), axis=1, keepdims=True)
            g_row = jnp.sum(jnp.where(eye, g_col, 0.0), axis=0, keepdims=True)
            g_tot = jnp.sum(lf, axis=1, keepdims=True)
            dm = jnp.where(seen, g_col - g_row + li, -jnp.inf)
            a = g_col + m_prev
            m_t = jnp.maximum(a, jnp.max(dm, axis=1, keepdims=True))
            p = jnp.exp(dm - m_t)
            ea = jnp.exp(a - m_t)

            qc = qb_ref[rows, :]
            ktc = kt_ref[:, rows]
            vxc = vx_ref[rows, :]
            sqk = jnp.dot(qc, ktc.astype(BF16), preferred_element_type=F32)
            sc = (sqk * p).astype(BF16)
            tot = (ea * jnp.dot(qc, c_ref[...].astype(BF16), preferred_element_type=F32)
                   + jnp.dot(sc, vxc, preferred_element_type=F32))
            den = tot[:, dv:dv + 1]
            hh = tot[:, :dv] * (1.0 / jnp.maximum(jnp.abs(den), jnp.exp(-m_t)))
            if direction == 0:
                o_ref[0, rows, :] = hh
            else:
                o_ref[0, rows, :] += hh

            w_log = g_tot - g_row + li
            m_new = jnp.maximum(g_tot + m_prev, jnp.max(w_log, axis=1, keepdims=True))
            decay = jnp.exp(g_tot + m_prev - m_new)
            w = jnp.exp(w_log - m_new)
            c_ref[...] = decay * c_ref[...] + jnp.dot((ktc * w).astype(BF16), vxc, preferred_element_type=F32)
            m_prev = m_new


def _mlstm_call(proj, gates_t, bsz, seq, dk, dv):
    nh = MLSTM_HEADS
    chunk = MLSTM_KERNEL_CHUNK
    k_blk = nh
    v_blk = (2 * nh * dk) // dv
    dvx = dv + V7X_LANES
    vmem = (2 * seq * (2 * dk + 2 * dv) * 4 + 2 * MLSTM_GATES * seq * 4
            + seq * dk * 2 + dk * seq * 4 + seq * dvx * 2 + dk * dvx * 4 + 8 * chunk * dvx * 4)
    return pl.pallas_call(
        functools.partial(_mlstm_kernel, seq=seq, chunk=chunk, dk=dk, dv=dv),
        grid=(bsz, nh),
        in_specs=[
            pl.BlockSpec((1, seq, dk), lambda b, h: (b, 0, h)),
            pl.BlockSpec((1, seq, dk), lambda b, h: (b, 0, k_blk + h)),
            pl.BlockSpec((1, seq, dv), lambda b, h: (b, 0, v_blk + h)),
            pl.BlockSpec((1, MLSTM_GATES, seq), lambda b, h: (b, 0, 0)),
        ],
        out_specs=pl.BlockSpec((1, seq, dv), lambda b, h: (b, 0, h)),
        out_shape=jax.ShapeDtypeStruct((bsz, seq, nh * dv), F32),
        scratch_shapes=[
            pltpu.VMEM((seq, dk), BF16),
            pltpu.VMEM((dk, seq), F32),
            pltpu.VMEM((seq, dvx), BF16),
            pltpu.VMEM((dk, dvx), F32),
        ],
        compiler_params=_params(("arbitrary", "arbitrary"), vmem),
        name="bidir_mlstm",
    )(proj, proj, proj, gates_t)


def kernel(x, c, positions, attn_w_qkv, attn_b_qkv, attn_sink, attn_w_o, attn_b_o, mlstm_w_in, mlstm_b_in,
           mlstm_norm_w, mlstm_w_o, mlstm_b_o, mod_w, mod_b, mlp_w1, mlp_b1, mlp_w2, mlp_b2,
           ln_mix_g, ln_mix_b, ln_mlp_g, ln_mlp_b):
    bsz, seq, d = x.shape
    depth = mod_w.shape[0]
    assert depth == DEPTH
    m = bsz * seq
    d_ff = mlp_w1.shape[-1]
    ml_main = mlstm_w_in.shape[-1] - MLSTM_GATES
    ml_dv = mlstm_w_o.shape[1] // MLSTM_HEADS
    ml_dk = (ml_main - 2 * MLSTM_HEADS * ml_dv) // (2 * MLSTM_HEADS)

    c_pad = jnp.pad(c, ((0, V7X_SUBLANES - bsz), (0, 0)))
    mod = _mod_call(c_pad, mod_w, mod_b)[:, :bsz]
    mod = mod.reshape(depth, bsz, 6, 1, d)
    sh_m, sc_m, g_m, sh_f, sc_f, g_f = (mod[:, :, j] for j in range(6))

    cos, sin = _rope_call(positions)
    x2 = x.reshape(m, d)
    h = _modulate_call(x2, sc_m[0], sh_m[0], seq)

    mlstm_w_in_t = jnp.swapaxes(mlstm_w_in, 1, 2)

    for i in range(depth):
        j = i // 2
        if i % 2 == 0:
            proj, w_o_bf = _mm_call(h, attn_w_qkv, j, attn_b_qkv[j], n_cols=attn_w_qkv.shape[-1], out_dtype=F32,
                                    side=(attn_w_o, j), name="attn_qkv_proj")
            a = _attn_call(proj.reshape(bsz, seq, -1), cos, sin, attn_sink[j], bsz, seq).reshape(m, d)
            x2, h = _mm_ln_call(a, w_o_bf, attn_b_o[j], x2, g_m[i], ln_mix_g[i], ln_mix_b[i],
                                (sc_f[i], sh_f[i]), rows_per_batch=seq, tm=LN_PROJ_ROW_TILE, name="attn_out_ln")
        else:
            proj, w_o_bf = _mm_call(h, mlstm_w_in_t, j, mlstm_b_in[j], n_cols=ml_main, out_dtype=F32,
                                    w_transposed=True, side=(mlstm_w_o, j), name="mlstm_in_proj")
            gates_t = _gates_call(h, mlstm_w_in_t, j, mlstm_b_in[j][ml_main:].reshape(MLSTM_GATES, 1), bsz, seq)
            hs = _mlstm_call(proj.reshape(bsz, seq, ml_main), gates_t, bsz, seq, ml_dk, ml_dv).reshape(m, d)
            x2, h = _mm_ln_call(hs, w_o_bf, mlstm_b_o[j], x2, g_m[i], ln_mix_g[i], ln_mix_b[i],
                                (sc_f[i], sh_f[i]), rows_per_batch=seq, tm=LN_PROJ_ROW_TILE, name="mlstm_out_ln",
                                ogate=(proj, (ml_main - d) // d), normw=mlstm_norm_w[j])
        u, w2_bf = _mm_call(h, mlp_w1, i, mlp_b1[i], n_cols=d_ff, out_dtype=BF16, act="relu2", side=(mlp_w2, i),
                            row_splits=2, name="mlp_up")
        nxt = (sc_m[i + 1], sh_m[i + 1]) if i + 1 < depth else None
        x2, h = _mm_ln_call(u, w2_bf, mlp_b2[i], x2, g_f[i], ln_mlp_g[i], ln_mlp_b[i], nxt,
                            rows_per_batch=seq, tm=MLP_DOWN_ROW_TILE, name="mlp_down_ln")
    return x2.reshape(bsz, seq, d)
```

```python
import functools

import numpy as np
import jax
import jax.numpy as jnp
from jax import lax
from jax.experimental import pallas as pl
from jax.experimental.pallas import tpu as pltpu

F32 = jnp.float32
BF16 = jnp.bfloat16

DEPTH = 2
ATTN_HEAD_DIM = 64
ATTN_KV_HEADS = 8
WINDOW = 128
ATTN_BLOCK = 128
ROPE_THETA = 10000.0
MLSTM_HEADS = 4
MLSTM_GATES = 4 * MLSTM_HEADS
DEEPNORM_ALPHA = (2.0 * DEPTH) ** 0.25
LN_EPS = 1e-5
HEAD_NORM_EPS = 1e-6
MLSTM_KERNEL_CHUNK = 256
LOG2_E = 1.4426950408889634

V7X_LANES = 128
V7X_SUBLANES = 8
V7X_VMEM_BYTES = 64 * 1024 * 1024
VMEM_LIMIT_CAP = 62 * 1024 * 1024

MOD_COL_TILE = 1024
ELEMENTWISE_ROW_TILE = 2048
MM_ROW_TILE = 1024
MM_COL_TILE = 1024
LN_PROJ_ROW_TILE = 512
MLP_DOWN_ROW_TILE = 256
GATES_ROW_TILE = 1024

ATTN_GROUP = 4
ATTN_Q_TILES = 2 * ATTN_GROUP * ATTN_HEAD_DIM // V7X_LANES
ATTN_Q_HEADS_PER_PAIR = 2 * ATTN_GROUP
VMEM_LIMIT_FLOOR = V7X_VMEM_BYTES - 28 * 1024 * 1024


def _params(semantics, vmem_bytes):
    limit = max(min(int(vmem_bytes * 1.25) + (4 << 20), VMEM_LIMIT_CAP), VMEM_LIMIT_FLOOR)
    return pltpu.CompilerParams(dimension_semantics=semantics, vmem_limit_bytes=limit)


def _mod_kernel(c_ref, w_ref, b_ref, o_ref):
    c = c_ref[...]
    c_act = (c * jax.nn.sigmoid(c)).astype(BF16)
    o_ref[0] = jnp.dot(c_act, w_ref[0].astype(BF16), preferred_element_type=F32) + b_ref[0]


def _mod_call(c_pad, mod_w, mod_b):
    depth, d, n = mod_w.shape
    rows = c_pad.shape[0]
    tn = MOD_COL_TILE
    return pl.pallas_call(
        _mod_kernel,
        grid=(depth, n // tn),
        in_specs=[
            pl.BlockSpec((rows, d), lambda i, j: (0, 0)),
            pl.BlockSpec((1, d, tn), lambda i, j: (i, 0, j)),
            pl.BlockSpec((1, 1, tn), lambda i, j: (i, 0, j)),
        ],
        out_specs=pl.BlockSpec((1, rows, tn), lambda i, j: (i, 0, j)),
        out_shape=jax.ShapeDtypeStruct((depth, rows, n), F32),
        compiler_params=_params(("arbitrary", "arbitrary"), 2 * d * tn * 4 + d * tn * 2),
        name="adaln_mod",
    )(c_pad, mod_w, mod_b.reshape(depth, 1, n))


def _modulate_kernel(x_ref, sc_ref, sh_ref, o_ref):
    o_ref[...] = (x_ref[...] * (1.0 + sc_ref[0]) + sh_ref[0]).astype(o_ref.dtype)


def _modulate_call(x2, sc, sh, rows_per_batch):
    m, d = x2.shape
    tm = ELEMENTWISE_ROW_TILE
    tpb = rows_per_batch // tm
    vec = pl.BlockSpec((1, 1, d), lambda i: (i // tpb, 0, 0))
    return pl.pallas_call(
        _modulate_kernel,
        grid=(m // tm,),
        in_specs=[pl.BlockSpec((tm, d), lambda i: (i, 0)), vec, vec],
        out_specs=pl.BlockSpec((tm, d), lambda i: (i, 0)),
        out_shape=jax.ShapeDtypeStruct((m, d), BF16),
        compiler_params=_params(("arbitrary",), 2 * tm * d * 6),
        name="modulate_in",
    )(x2, sc, sh)


_CONTRACT_LAST = (((1,), (1,)), ((), ()))


def _mm_kernel(*refs, act, w_transposed, side_cast, row_splits):
    if side_cast:
        x_ref, w_ref, b_ref, side_ref, o_ref, side_o_ref, wb_ref = refs
        side_o_ref[...] = side_ref[...].astype(side_o_ref.dtype)
    else:
        x_ref, w_ref, b_ref, o_ref, wb_ref = refs

    @pl.when(pl.program_id(1) == 0)
    def _():
        wb_ref[...] = w_ref[...].astype(BF16)

    for r in range(row_splits):
        rows = slice(r * MM_ROW_TILE, (r + 1) * MM_ROW_TILE)
        if w_transposed:
            acc = lax.dot_general(x_ref[rows, :], wb_ref[...], _CONTRACT_LAST, preferred_element_type=F32)
        else:
            acc = jnp.dot(x_ref[rows, :], wb_ref[...], preferred_element_type=F32)
        acc = acc + b_ref[...]
        if act == "relu2":
            acc = jnp.square(jnp.maximum(acc, 0.0))
        o_ref[rows, :] = acc.astype(o_ref.dtype)


def _mm_call(x, w, layer, b, *, n_cols, out_dtype, act=None, w_transposed=False, side=None, row_splits=1, name):
    m, k = x.shape
    tm, tn = MM_ROW_TILE * row_splits, MM_COL_TILE
    osz = jnp.dtype(out_dtype).itemsize
    vmem = 2 * k * tn * 4 + k * tn * 2 + 2 * tm * k * 2 + 2 * tm * tn * osz + MM_ROW_TILE * tn * 4
    if w_transposed:
        w_spec = pl.BlockSpec((None, tn, k), lambda j, i: (layer, j, 0))
        w_tile = (tn, k)
    else:
        w_spec = pl.BlockSpec((None, k, tn), lambda j, i: (layer, 0, j))
        w_tile = (k, tn)
    n_i = m // tm
    grid = (n_cols // tn, n_i)
    in_specs = [pl.BlockSpec((tm, k), lambda j, i: (i, 0)), w_spec, pl.BlockSpec((1, tn), lambda j, i: (0, j))]
    args = [x, w, b.reshape(1, -1)]
    out_specs = [pl.BlockSpec((tm, tn), lambda j, i: (i, j))]
    out_shape = [jax.ShapeDtypeStruct((m, n_cols), out_dtype)]
    if side is not None:
        w2, layer2 = side
        _, k2, n2 = w2.shape
        n_blk = 1 << ((grid[0] * grid[1]).bit_length() - 1)
        rows = k2 // n_blk
        assert rows * n_blk == k2 and rows % 16 == 0
        in_specs.append(pl.BlockSpec((None, rows, n2), lambda j, i: (layer2, jnp.minimum(j * n_i + i, n_blk - 1), 0)))
        args.append(w2)
        out_specs.append(pl.BlockSpec((rows, n2), lambda j, i: (jnp.minimum(j * n_i + i, n_blk - 1), 0)))
        out_shape.append(jax.ShapeDtypeStruct((k2, n2), BF16))
        vmem += 2 * rows * n2 * 6
    outs = pl.pallas_call(
        functools.partial(_mm_kernel, act=act, w_transposed=w_transposed, side_cast=side is not None,
                          row_splits=row_splits),
        grid=grid,
        in_specs=in_specs,
        out_specs=out_specs,
        out_shape=out_shape,
        scratch_shapes=[pltpu.VMEM(w_tile, BF16)],
        compiler_params=_params(("arbitrary", "arbitrary"), vmem),
        name=name,
    )(*args)
    return (outs[0], outs[1]) if side is not None else outs[0]


def _head_norm_gate(hs, og, nw, heads):
    dv = hs.shape[-1] // heads
    outs = []
    for h in range(heads):
        t = hs[:, h * dv:(h + 1) * dv]
        mu = jnp.mean(t, axis=-1, keepdims=True)
        var = jnp.mean(jnp.square(t - mu), axis=-1, keepdims=True)
        outs.append((t - mu) * lax.rsqrt(var + HEAD_NORM_EPS))
    hn = jnp.concatenate(outs, axis=-1) * nw
    return jax.nn.sigmoid(og) * hn


def _mm_ln_kernel(*refs, mlstm_prologue, emit_h):
    refs = list(refs)
    a_ref = refs.pop(0)
    if mlstm_prologue:
        og_ref = refs.pop(0)
        nw_ref = refs.pop(0)
    w_ref, b_ref, res_ref, gate_ref, lng_ref, lnb_ref = refs[:6]
    refs = refs[6:]
    if emit_h:
        scn_ref, shn_ref = refs[:2]
        refs = refs[2:]
    xo_ref = refs.pop(0)
    if emit_h:
        ho_ref = refs.pop(0)

    if mlstm_prologue:
        a = _head_norm_gate(a_ref[...], og_ref[...], nw_ref[...], MLSTM_HEADS).astype(BF16)
    else:
        a = a_ref[...]
    y = jnp.dot(a, w_ref[...], preferred_element_type=F32) + b_ref[...]
    z = DEEPNORM_ALPHA * res_ref[...] + (1.0 + gate_ref[0]) * y
    mu = jnp.mean(z, axis=-1, keepdims=True)
    var = jnp.mean(jnp.square(z - mu), axis=-1, keepdims=True)
    xn = (z - mu) * lax.rsqrt(var + LN_EPS) * lng_ref[...] + lnb_ref[...]
    xo_ref[...] = xn
    if emit_h:
        ho_ref[...] = (xn * (1.0 + scn_ref[0]) + shn_ref[0]).astype(ho_ref.dtype)


def _mm_ln_call(a, w_bf, b, res, gate, lng, lnb, nxt, *, rows_per_batch, tm, name, ogate=None, normw=None):
    m = a.shape[0]
    k, d = w_bf.shape
    tpb = rows_per_batch // tm
    mlstm_prologue = ogate is not None
    emit_h = nxt is not None
    vec_b = pl.BlockSpec((1, 1, d), lambda i: (i // tpb, 0, 0))
    vec = pl.BlockSpec((1, d), lambda i: (0, 0))
    row_tile = pl.BlockSpec((tm, d), lambda i: (i, 0))

    in_specs = [pl.BlockSpec((tm, k), lambda i: (i, 0))]
    args = [a]
    asz = a.dtype.itemsize
    if mlstm_prologue:
        ocb = ogate[1]
        in_specs += [pl.BlockSpec((tm, k), lambda i: (i, ocb)), pl.BlockSpec((1, k), lambda i: (0, 0))]
        args += [ogate[0], normw.reshape(1, -1)]
    in_specs += [pl.BlockSpec((k, d), lambda i: (0, 0), pipeline_mode=pl.Buffered(1)),
                 vec, row_tile, vec_b, vec, vec]
    args += [w_bf, b.reshape(1, d), res, gate, lng.reshape(1, d), lnb.reshape(1, d)]
    out_specs = [row_tile]
    out_shape = [jax.ShapeDtypeStruct((m, d), F32)]
    if emit_h:
        in_specs += [vec_b, vec_b]
        args += [nxt[0], nxt[1]]
        out_specs.append(row_tile)
        out_shape.append(jax.ShapeDtypeStruct((m, d), BF16))
    vmem = (k * d * 2 + 2 * tm * k * asz * (2 if mlstm_prologue else 1)
            + 2 * tm * d * (4 + 4 + 2) + 2 * tm * d * 4)
    outs = pl.pallas_call(
        functools.partial(_mm_ln_kernel, mlstm_prologue=mlstm_prologue, emit_h=emit_h),
        grid=(m // tm,),
        in_specs=in_specs,
        out_specs=out_specs,
        out_shape=out_shape,
        compiler_params=_params(("arbitrary",), vmem),
        name=name,
    )(*args)
    return (outs[0], outs[1]) if emit_h else (outs[0], None)


def _rope_kernel(pos_ref, freq_ref, cos_ref, sin_ref):
    ang = pos_ref[0].astype(F32) * freq_ref[...]
    lane = lax.broadcasted_iota(jnp.int32, ang.shape, 1)
    first_half = (lane % ATTN_HEAD_DIM) < ATTN_HEAD_DIM // 2
    cos_ref[0] = jnp.cos(ang)
    s = jnp.sin(ang)
    sin_ref[0] = jnp.where(first_half, -s, s)


def _rope_call(positions):
    bsz, s = positions.shape
    half = ATTN_HEAD_DIM // 2
    inv_freq = 1.0 / (ROPE_THETA ** (np.arange(0, ATTN_HEAD_DIM, 2, dtype=np.float32) / ATTN_HEAD_DIM))
    freq = jnp.asarray(np.tile(inv_freq.astype(np.float32), V7X_LANES // half).reshape(1, V7X_LANES))
    tbl = jax.ShapeDtypeStruct((bsz, s, V7X_LANES), F32)
    return pl.pallas_call(
        _rope_kernel,
        grid=(bsz,),
        in_specs=[pl.BlockSpec((1, s, 1), lambda b: (b, 0, 0)), pl.BlockSpec((1, V7X_LANES), lambda b: (0, 0))],
        out_specs=[pl.BlockSpec((1, s, V7X_LANES), lambda b: (b, 0, 0))] * 2,
        out_shape=[tbl, tbl],
        compiler_params=_params(("arbitrary",), 6 * s * V7X_LANES * 4),
        name="rope_tables",
    )(positions.reshape(bsz, s, 1), freq)


def _rope(t, cos, sin_signed, first_half):
    half = ATTN_HEAD_DIM // 2
    rot = jnp.where(first_half, pltpu.roll(t, V7X_LANES - half, 1), pltpu.roll(t, half, 1))
    return t * cos + rot * sin_signed


ATTN_VT_ROWS = ATTN_HEAD_DIM + 16


def _attn_kernel(sink_ref, q_ref, k_ref, v_ref, cos_ref, sin_ref, o_ref, kp_ref, vt_ref, bias_ref, *, seq):
    blk = ATTN_BLOCK
    hd = ATTN_HEAD_DIM
    nb = seq // blk
    hp = pl.program_id(1)
    lane = lax.broadcasted_iota(jnp.int32, (seq, V7X_LANES), 1)
    low = lane < hd
    first_half = (lane % hd) < hd // 2
    cos = cos_ref[0]
    sin = sin_ref[0]

    scale = hd ** -0.5 * LOG2_E
    lane_q = lax.broadcasted_iota(jnp.int32, (blk, V7X_LANES), 1)
    first_half_q = (lane_q % hd) < hd // 2

    kr = _rope(k_ref[0], cos, sin, first_half)
    lo = jnp.where(low, kr, 0.0)
    hi = jnp.where(low, 0.0, kr)
    zeros_pad = jnp.zeros((blk, V7X_LANES), BF16)
    for idx, val in enumerate((lo, pltpu.roll(lo, hd, 1), pltpu.roll(hi, hd, 1), hi)):
        kp_ref[idx, 0:blk, :] = zeros_pad
        kp_ref[idx, blk:blk + seq, :] = val.astype(BF16)
        kp_ref[idx, blk + seq:, :] = zeros_pad

    vt = jnp.transpose(v_ref[0])
    ones_row = lax.broadcasted_iota(jnp.int32, (ATTN_VT_ROWS - hd, seq), 0) == 0
    zeros_vt = jnp.zeros((ATTN_VT_ROWS, blk), BF16)
    for j in range(2):
        vt_ref[j, :, 0:blk] = zeros_vt
        vt_ref[j, 0:hd, blk:blk + seq] = vt[j * hd:(j + 1) * hd].astype(BF16)
        vt_ref[j, hd:, blk:blk + seq] = jnp.where(ones_row, 1.0, 0.0).astype(BF16)
        vt_ref[j, :, blk + seq:] = zeros_vt

    ci = lax.broadcasted_iota(jnp.int32, (blk, 2 * blk), 0)
    ri = lax.broadcasted_iota(jnp.int32, (blk, 2 * blk), 1) % blk
    neg = jnp.full((blk, 2 * blk), -jnp.inf, F32)
    bias_ref[0] = jnp.where(ci >= ri, 0.0, -jnp.inf)
    bias_ref[1] = neg
    bias_ref[2] = jnp.where(ci <= ri, 0.0, -jnp.inf)
    bias_ref[3] = neg
    lane2 = lax.broadcasted_iota(jnp.int32, (1, 2 * blk), 1)

    chains = [(j, half) for j in range(2) for half in range(2)]

    def block_scores(i):
        r0 = i * blk
        cos_q = cos_ref[0, pl.ds(r0, blk), :]
        sin_q = sin_ref[0, pl.ds(r0, blk), :]
        q_tiles = []
        for t in range(ATTN_Q_TILES):
            qt = q_ref[0, pl.ds(r0, blk), t * V7X_LANES:(t + 1) * V7X_LANES]
            q_tiles.append((_rope(qt, cos_q, sin_q, first_half_q) * scale).astype(BF16))
        scores = []
        for j, half in chains:
            qcat = jnp.concatenate([q_tiles[2 * j], q_tiles[2 * j + 1]], axis=0)
            kk = kp_ref[2 * j + half, pl.ds(r0, 3 * blk), :]
            scores.append(lax.dot_general(kk, qcat, _CONTRACT_LAST, preferred_element_type=F32))
        return scores

    def block_probs(i, scores):
        bias_prev = bias_ref[1 if i == 0 else 0]
        bias_next = bias_ref[3 if i == nb - 1 else 2]
        probs = []
        for (j, half), st in zip(chains, scores):
            head = hp * ATTN_Q_HEADS_PER_PAIR + ATTN_GROUP * j + half
            snk = jnp.where(lane2 < blk, sink_ref[head], sink_ref[head + 2]) * LOG2_E
            parts = (st[0:blk] + bias_prev, st[blk:2 * blk], st[2 * blk:] + bias_next)
            m = snk
            for part in parts:
                m = jnp.maximum(m, jnp.max(part, axis=0, keepdims=True))
            p = jnp.concatenate([jnp.exp2(part - m) for part in parts], axis=0)
            probs.append((p.astype(BF16), jnp.exp2(snk - m)))
        return probs

    def block_output(i, probs):
        r0 = i * blk
        outs = []
        for (j, half), (p, sink_p) in zip(chains, probs):
            vtw = vt_ref[j, :, pl.ds(r0, 3 * blk)]
            ot = jnp.dot(vtw, p, preferred_element_type=F32)
            outs.append(ot[0:hd] * (1.0 / (ot[hd:hd + 1] + sink_p)))
        for t in range(ATTN_Q_TILES):
            j, tt = divmod(t, 2)
            cols = slice(tt * blk, (tt + 1) * blk)
            tile_t = jnp.concatenate([outs[2 * j][:, cols], outs[2 * j + 1][:, cols]], axis=0)
            o_ref[0, pl.ds(r0, blk), t * V7X_LANES:(t + 1) * V7X_LANES] = jnp.transpose(tile_t).astype(o_ref.dtype)

    scores = block_scores(0)
    for i in range(nb):
        probs = block_probs(i, scores)
        if i + 1 < nb:
            scores = block_scores(i + 1)
        block_output(i, probs)


def _attn_call(proj, cos, sin, sink, bsz, seq):
    n_pairs = ATTN_KV_HEADS // 2
    qw = ATTN_Q_TILES * V7X_LANES
    k_off = (proj.shape[-1] - 2 * ATTN_KV_HEADS * ATTN_HEAD_DIM) // V7X_LANES
    v_off = k_off + n_pairs
    assert seq // ATTN_BLOCK >= 2 and ATTN_BLOCK == V7X_LANES and 2 * ATTN_HEAD_DIM == V7X_LANES
    assert WINDOW == ATTN_BLOCK
    pad_seq = seq + 2 * ATTN_BLOCK
    tbl = pl.BlockSpec((1, seq, V7X_LANES), lambda b, p: (b, 0, 0))
    vmem = (2 * seq * (qw * 4 + 4 * V7X_LANES * 4 + qw * 2) + 4 * pad_seq * V7X_LANES * 2
            + 2 * ATTN_VT_ROWS * pad_seq * 2 + 3 * 3 * ATTN_BLOCK * 2 * ATTN_BLOCK * 4 + 6 * seq * V7X_LANES * 4)
    return pl.pallas_call(
        functools.partial(_attn_kernel, seq=seq),
        grid=(bsz, n_pairs),
        in_specs=[
            pl.BlockSpec(memory_space=pltpu.SMEM),
            pl.BlockSpec((1, seq, qw), lambda b, p: (b, 0, p)),
            pl.BlockSpec((1, seq, V7X_LANES), lambda b, p: (b, 0, k_off + p)),
            pl.BlockSpec((1, seq, V7X_LANES), lambda b, p: (b, 0, v_off + p)),
            tbl, tbl,
        ],
        out_specs=pl.BlockSpec((1, seq, qw), lambda b, p: (b, 0, p)),
        out_shape=jax.ShapeDtypeStruct((bsz, seq, n_pairs * qw), BF16),
        scratch_shapes=[
            pltpu.VMEM((4, pad_seq, V7X_LANES), BF16),
            pltpu.VMEM((2, ATTN_VT_ROWS, pad_seq), BF16),
            pltpu.VMEM((4, ATTN_BLOCK, 2 * ATTN_BLOCK), F32),
        ],
        compiler_params=_params(("arbitrary", "arbitrary"), vmem),
        name="swa_sink_attention",
    )(sink, proj, proj, proj, cos, sin)


def _gates_kernel(h_ref, w_ref, b_ref, o_ref):
    g_t = lax.dot_general(w_ref[...].astype(BF16), h_ref[...], _CONTRACT_LAST, preferred_element_type=F32)
    o_ref[0] = g_t + b_ref[...]


def _gates_call(h, w_t, layer, b_col, bsz, seq):
    m, k = h.shape
    tm = GATES_ROW_TILE
    tpb = seq // tm
    gate_blk = (w_t.shape[1] - MLSTM_GATES) // MLSTM_GATES
    return pl.pallas_call(
        _gates_kernel,
        grid=(m // tm,),
        in_specs=[
            pl.BlockSpec((tm, k), lambda i: (i, 0)),
            pl.BlockSpec((None, MLSTM_GATES, k), lambda i: (layer, gate_blk, 0)),
            pl.BlockSpec((MLSTM_GATES, 1), lambda i: (0, 0)),
        ],
        out_specs=pl.BlockSpec((1, MLSTM_GATES, tm), lambda i: (i // tpb, 0, i % tpb)),
        out_shape=jax.ShapeDtypeStruct((bsz, MLSTM_GATES, seq), F32),
        compiler_params=_params(("arbitrary",), 2 * tm * k * 2 + 4 * MLSTM_GATES * (k + tm) * 4),
        name="mlstm_gates",
    )(h, w_t, b_col)


def _log_sigmoid(x):
    return jnp.minimum(x, 0.0) - jnp.log1p(jnp.exp(-jnp.abs(x)))


def _mlstm_kernel(q_ref, k_ref, v_ref, g_ref, o_ref, qb_ref, kt_ref, vx_ref, c_ref, *, seq, chunk, dk, dv):
    nh = MLSTM_HEADS
    head = pl.program_id(1)
    nc = seq // chunk
    qb_ref[...] = (q_ref[0] * (dk ** -0.5)).astype(BF16)
    kt_ref[...] = jnp.transpose(k_ref[0])
    vx_ref[:, :dv] = v_ref[0].astype(BF16)
    ones_lane = lax.broadcasted_iota(jnp.int32, (seq, V7X_LANES), 1) == 0
    vx_ref[:, dv:] = jnp.where(ones_lane, 1.0, 0.0).astype(BF16)

    ti = lax.broadcasted_iota(jnp.int32, (chunk, chunk), 0)
    ui = lax.broadcasted_iota(jnp.int32, (chunk, chunk), 1)
    eye = ti == ui
    for direction in range(2):
        seen = (ui <= ti) if direction == 0 else (ui >= ti)
        c_ref[...] = jnp.zeros_like(c_ref)
        m_prev = jnp.full((1, 1), -1e30, F32)
        order = range(nc) if direction == 0 else range(nc - 1, -1, -1)
        for c in order:
            rows = slice(c * chunk, (c + 1) * chunk)
            gate_row = 2 * direction * nh + head
            li = g_ref[0, pl.ds(gate_row, 1), rows]
            lf = _log_sigmoid(g_ref[0, pl.ds(gate_row + nh, 1), rows])
            g_col = jnp.sum(jnp.where(seen, lf, 0.0), axis=1, keepdims=True)
            g_row = jnp.sum(jnp.where(eye, g_col, 0.0), axis=0, keepdims=True)
            g_tot = jnp.sum(lf, axis=1, keepdims=True)
            dm = jnp.where(seen, g_col - g_row + li, -jnp.inf)
            a = g_col + m_prev
            m_t = jnp.maximum(a, jnp.max(dm, axis=1, keepdims=True))
            p = jnp.exp(dm - m_t)
            ea = jnp.exp(a - m_t)

            qc = qb_ref[rows, :]
            ktc = kt_ref[:, rows]
            vxc = vx_ref[rows, :]
            sqk = jnp.dot(qc, ktc.astype(BF16), preferred_element_type=F32)
            sc = (sqk * p).astype(BF16)
            tot = (ea * jnp.dot(qc, c_ref[...].astype(BF16), preferred_element_type=F32)
                   + jnp.dot(sc, vxc, preferred_element_type=F32))
            den = tot[:, dv:dv + 1]
            hh = tot[:, :dv] * (1.0 / jnp.maximum(jnp.abs(den), jnp.exp(-m_t)))
            if direction == 0:
                o_ref[0, rows, :] = hh
            else:
                o_ref[0, rows, :] += hh

            w_log = g_tot - g_row + li
            m_new = jnp.maximum(g_tot + m_prev, jnp.max(w_log, axis=1, keepdims=True))
            decay = jnp.exp(g_tot + m_prev - m_new)
            w = jnp.exp(w_log - m_new)
            c_ref[...] = decay * c_ref[...] + jnp.dot((ktc * w).astype(BF16), vxc, preferred_element_type=F32)
            m_prev = m_new


def _mlstm_call(proj, gates_t, bsz, seq, dk, dv):
    nh = MLSTM_HEADS
    chunk = MLSTM_KERNEL_CHUNK
    k_blk = nh
    v_blk = (2 * nh * dk) // dv
    dvx = dv + V7X_LANES
    vmem = (2 * seq * (2 * dk + 2 * dv) * 4 + 2 * MLSTM_GATES * seq * 4
            + seq * dk * 2 + dk * seq * 4 + seq * dvx * 2 + dk * dvx * 4 + 8 * chunk * dvx * 4)
    return pl.pallas_call(
        functools.partial(_mlstm_kernel, seq=seq, chunk=chunk, dk=dk, dv=dv),
        grid=(bsz, nh),
        in_specs=[
            pl.BlockSpec((1, seq, dk), lambda b, h: (b, 0, h)),
            pl.BlockSpec((1, seq, dk), lambda b, h: (b, 0, k_blk + h)),
            pl.BlockSpec((1, seq, dv), lambda b, h: (b, 0, v_blk + h)),
            pl.BlockSpec((1, MLSTM_GATES, seq), lambda b, h: (b, 0, 0)),
        ],
        out_specs=pl.BlockSpec((1, seq, dv), lambda b, h: (b, 0, h)),
        out_shape=jax.ShapeDtypeStruct((bsz, seq, nh * dv), F32),
        scratch_shapes=[
            pltpu.VMEM((seq, dk), BF16),
            pltpu.VMEM((dk, seq), F32),
            pltpu.VMEM((seq, dvx), BF16),
            pltpu.VMEM((dk, dvx), F32),
        ],
        compiler_params=_params(("arbitrary", "arbitrary"), vmem),
        name="bidir_mlstm",
    )(proj, proj, proj, gates_t)


def kernel(x, c, positions, attn_w_qkv, attn_b_qkv, attn_sink, attn_w_o, attn_b_o, mlstm_w_in, mlstm_b_in,
           mlstm_norm_w, mlstm_w_o, mlstm_b_o, mod_w, mod_b, mlp_w1, mlp_b1, mlp_w2, mlp_b2,
           ln_mix_g, ln_mix_b, ln_mlp_g, ln_mlp_b):
    bsz, seq, d = x.shape
    depth = mod_w.shape[0]
    assert depth == DEPTH
    m = bsz * seq
    d_ff = mlp_w1.shape[-1]
    ml_main = mlstm_w_in.shape[-1] - MLSTM_GATES
    ml_dv = mlstm_w_o.shape[1] // MLSTM_HEADS
    ml_dk = (ml_main - 2 * MLSTM_HEADS * ml_dv) // (2 * MLSTM_HEADS)

    c_pad = jnp.pad(c, ((0, V7X_SUBLANES - bsz), (0, 0)))
    mod = _mod_call(c_pad, mod_w, mod_b)[:, :bsz]
    mod = mod.reshape(depth, bsz, 6, 1, d)
    sh_m, sc_m, g_m, sh_f, sc_f, g_f = (mod[:, :, j] for j in range(6))

    cos, sin = _rope_call(positions)
    x2 = x.reshape(m, d)
    h = _modulate_call(x2, sc_m[0], sh_m[0], seq)

    mlstm_w_in_t = jnp.swapaxes(mlstm_w_in, 1, 2)

    for i in range(depth):
        j = i // 2
        if i % 2 == 0:
            proj, w_o_bf = _mm_call(h, attn_w_qkv, j, attn_b_qkv[j], n_cols=attn_w_qkv.shape[-1], out_dtype=F32,
                                    side=(attn_w_o, j), name="attn_qkv_proj")
            a = _attn_call(proj.reshape(bsz, seq, -1), cos, sin, attn_sink[j], bsz, seq).reshape(m, d)
            x2, h = _mm_ln_call(a, w_o_bf, attn_b_o[j], x2, g_m[i], ln_mix_g[i], ln_mix_b[i],
                                (sc_f[i], sh_f[i]), rows_per_batch=seq, tm=LN_PROJ_ROW_TILE, name="attn_out_ln")
        else:
            proj, w_o_bf = _mm_call(h, mlstm_w_in_t, j, mlstm_b_in[j], n_cols=ml_main, out_dtype=F32,
                                    w_transposed=True, side=(mlstm_w_o, j), row_splits=2, name="mlstm_in_proj")
            gates_t = _gates_call(h, mlstm_w_in_t, j, mlstm_b_in[j][ml_main:].reshape(MLSTM_GATES, 1), bsz, seq)
            hs = _mlstm_call(proj.reshape(bsz, seq, ml_main), gates_t, bsz, seq, ml_dk, ml_dv).reshape(m, d)
            x2, h = _mm_ln_call(hs, w_o_bf, mlstm_b_o[j], x2, g_m[i], ln_mix_g[i], ln_mix_b[i],
                                (sc_f[i], sh_f[i]), rows_per_batch=seq, tm=LN_PROJ_ROW_TILE, name="mlstm_out_ln",
                                ogate=(proj, (ml_main - d) // d), normw=mlstm_norm_w[j])
        u, w2_bf = _mm_call(h, mlp_w1, i, mlp_b1[i], n_cols=d_ff, out_dtype=BF16, act="relu2", side=(mlp_w2, i),
                            row_splits=2, name="mlp_up")
        nxt = (sc_m[i + 1], sh_m[i + 1]) if i + 1 < depth else None
        x2, h = _mm_ln_call(u, w2_bf, mlp_b2[i], x2, g_f[i], ln_mlp_g[i], ln_mlp_b[i], nxt,
                            rows_per_batch=seq, tm=MLP_DOWN_ROW_TILE, name="mlp_down_ln")
    return x2.reshape(bsz, seq, d)
```

```python
import functools

import numpy as np
import jax
import jax.numpy as jnp
from jax import lax
from jax.experimental import pallas as pl
from jax.experimental.pallas import tpu as pltpu

F32 = jnp.float32
BF16 = jnp.bfloat16

DEPTH = 2
ATTN_HEAD_DIM = 64
ATTN_KV_HEADS = 8
WINDOW = 128
ATTN_BLOCK = 128
ROPE_THETA = 10000.0
MLSTM_HEADS = 4
MLSTM_GATES = 4 * MLSTM_HEADS
DEEPNORM_ALPHA = (2.0 * DEPTH) ** 0.25
LN_EPS = 1e-5
HEAD_NORM_EPS = 1e-6
MLSTM_KERNEL_CHUNK = 256
LOG2_E = 1.4426950408889634

V7X_LANES = 128
V7X_SUBLANES = 8
V7X_VMEM_BYTES = 64 * 1024 * 1024
VMEM_LIMIT_CAP = 63 * 1024 * 1024

MOD_COL_TILE = 1024
ELEMENTWISE_ROW_TILE = 2048
MM_ROW_TILE = 1024
MM_COL_TILE = 1024
LN_PROJ_ROW_TILE = 512
MLP_DOWN_ROW_TILE = 256
GATES_ROW_TILE = 1024

ATTN_GROUP = 4
ATTN_Q_TILES = 2 * ATTN_GROUP * ATTN_HEAD_DIM // V7X_LANES
ATTN_Q_HEADS_PER_PAIR = 2 * ATTN_GROUP
VMEM_LIMIT_FLOOR = V7X_VMEM_BYTES - 28 * 1024 * 1024


def _params(semantics, vmem_bytes):
    limit = max(min(int(vmem_bytes * 1.25) + (4 << 20), VMEM_LIMIT_CAP), VMEM_LIMIT_FLOOR)
    return pltpu.CompilerParams(dimension_semantics=semantics, vmem_limit_bytes=limit)


def _mod_kernel(c_ref, w_ref, b_ref, o_ref):
    c = c_ref[...]
    c_act = (c * jax.nn.sigmoid(c)).astype(BF16)
    o_ref[0] = jnp.dot(c_act, w_ref[0].astype(BF16), preferred_element_type=F32) + b_ref[0]


def _mod_call(c_pad, mod_w, mod_b):
    depth, d, n = mod_w.shape
    rows = c_pad.shape[0]
    tn = MOD_COL_TILE
    return pl.pallas_call(
        _mod_kernel,
        grid=(depth, n // tn),
        in_specs=[
            pl.BlockSpec((rows, d), lambda i, j: (0, 0)),
            pl.BlockSpec((1, d, tn), lambda i, j: (i, 0, j)),
            pl.BlockSpec((1, 1, tn), lambda i, j: (i, 0, j)),
        ],
        out_specs=pl.BlockSpec((1, rows, tn), lambda i, j: (i, 0, j)),
        out_shape=jax.ShapeDtypeStruct((depth, rows, n), F32),
        compiler_params=_params(("arbitrary", "arbitrary"), 2 * d * tn * 4 + d * tn * 2),
        name="adaln_mod",
    )(c_pad, mod_w, mod_b.reshape(depth, 1, n))


def _modulate_kernel(x_ref, sc_ref, sh_ref, o_ref):
    o_ref[...] = (x_ref[...] * (1.0 + sc_ref[0]) + sh_ref[0]).astype(o_ref.dtype)


def _modulate_call(x2, sc, sh, rows_per_batch):
    m, d = x2.shape
    tm = ELEMENTWISE_ROW_TILE
    tpb = rows_per_batch // tm
    vec = pl.BlockSpec((1, 1, d), lambda i: (i // tpb, 0, 0))
    return pl.pallas_call(
        _modulate_kernel,
        grid=(m // tm,),
        in_specs=[pl.BlockSpec((tm, d), lambda i: (i, 0)), vec, vec],
        out_specs=pl.BlockSpec((tm, d), lambda i: (i, 0)),
        out_shape=jax.ShapeDtypeStruct((m, d), BF16),
        compiler_params=_params(("arbitrary",), 2 * tm * d * 6),
        name="modulate_in",
    )(x2, sc, sh)


_CONTRACT_LAST = (((1,), (1,)), ((), ()))


def _mm_kernel(*refs, act, w_transposed, side_cast, row_splits):
    if side_cast:
        x_ref, w_ref, b_ref, side_ref, o_ref, side_o_ref, wb_ref = refs
        side_o_ref[...] = side_ref[...].astype(side_o_ref.dtype)
    else:
        x_ref, w_ref, b_ref, o_ref, wb_ref = refs

    @pl.when(pl.program_id(1) == 0)
    def _():
        wb_ref[...] = w_ref[...].astype(BF16)

    for r in range(row_splits):
        rows = slice(r * MM_ROW_TILE, (r + 1) * MM_ROW_TILE)
        if w_transposed:
            acc = lax.dot_general(x_ref[rows, :], wb_ref[...], _CONTRACT_LAST, preferred_element_type=F32)
        else:
            acc = jnp.dot(x_ref[rows, :], wb_ref[...], preferred_element_type=F32)
        acc = acc + b_ref[...]
        if act == "relu2":
            acc = jnp.square(jnp.maximum(acc, 0.0))
        o_ref[rows, :] = acc.astype(o_ref.dtype)


def _mm_call(x, w, layer, b, *, n_cols, out_dtype, act=None, w_transposed=False, side=None, row_splits=1, name):
    m, k = x.shape
    tm, tn = MM_ROW_TILE * row_splits, MM_COL_TILE
    osz = jnp.dtype(out_dtype).itemsize
    vmem = 2 * k * tn * 4 + k * tn * 2 + 2 * tm * k * 2 + 2 * tm * tn * osz + MM_ROW_TILE * tn * 4
    if w_transposed:
        w_spec = pl.BlockSpec((None, tn, k), lambda j, i: (layer, j, 0))
        w_tile = (tn, k)
    else:
        w_spec = pl.BlockSpec((None, k, tn), lambda j, i: (layer, 0, j))
        w_tile = (k, tn)
    n_i = m // tm
    grid = (n_cols // tn, n_i)
    in_specs = [pl.BlockSpec((tm, k), lambda j, i: (i, 0)), w_spec, pl.BlockSpec((1, tn), lambda j, i: (0, j))]
    args = [x, w, b.reshape(1, -1)]
    out_specs = [pl.BlockSpec((tm, tn), lambda j, i: (i, j))]
    out_shape = [jax.ShapeDtypeStruct((m, n_cols), out_dtype)]
    if side is not None:
        w2, layer2 = side
        _, k2, n2 = w2.shape
        n_blk = 1 << ((grid[0] * grid[1]).bit_length() - 1)
        rows = k2 // n_blk
        assert rows * n_blk == k2 and rows % 16 == 0
        in_specs.append(pl.BlockSpec((None, rows, n2), lambda j, i: (layer2, jnp.minimum(j * n_i + i, n_blk - 1), 0)))
        args.append(w2)
        out_specs.append(pl.BlockSpec((rows, n2), lambda j, i: (jnp.minimum(j * n_i + i, n_blk - 1), 0)))
        out_shape.append(jax.ShapeDtypeStruct((k2, n2), BF16))
        vmem += 2 * rows * n2 * 6
    outs = pl.pallas_call(
        functools.partial(_mm_kernel, act=act, w_transposed=w_transposed, side_cast=side is not None,
                          row_splits=row_splits),
        grid=grid,
        in_specs=in_specs,
        out_specs=out_specs,
        out_shape=out_shape,
        scratch_shapes=[pltpu.VMEM(w_tile, BF16)],
        compiler_params=_params(("arbitrary", "arbitrary"), vmem),
        name=name,
    )(*args)
    return (outs[0], outs[1]) if side is not None else outs[0]


def _head_norm_gate(hs, og, nw, heads):
    dv = hs.shape[-1] // heads
    outs = []
    for h in range(heads):
        t = hs[:, h * dv:(h + 1) * dv]
        mu = jnp.mean(t, axis=-1, keepdims=True)
        var = jnp.mean(jnp.square(t - mu), axis=-1, keepdims=True)
        outs.append((t - mu) * lax.rsqrt(var + HEAD_NORM_EPS))
    hn = jnp.concatenate(outs, axis=-1) * nw
    return jax.nn.sigmoid(og) * hn


def _mm_ln_kernel(*refs, mlstm_prologue, emit_h):
    refs = list(refs)
    a_ref = refs.pop(0)
    if mlstm_prologue:
        og_ref = refs.pop(0)
        nw_ref = refs.pop(0)
    w_ref, b_ref, res_ref, gate_ref, lng_ref, lnb_ref = refs[:6]
    refs = refs[6:]
    if emit_h:
        scn_ref, shn_ref = refs[:2]
        refs = refs[2:]
    xo_ref = refs.pop(0)
    if emit_h:
        ho_ref = refs.pop(0)

    if mlstm_prologue:
        a = _head_norm_gate(a_ref[...], og_ref[...], nw_ref[...], MLSTM_HEADS).astype(BF16)
    else:
        a = a_ref[...]
    y = jnp.dot(a, w_ref[...], preferred_element_type=F32) + b_ref[...]
    z = DEEPNORM_ALPHA * res_ref[...] + (1.0 + gate_ref[0]) * y
    mu = jnp.mean(z, axis=-1, keepdims=True)
    var = jnp.mean(jnp.square(z - mu), axis=-1, keepdims=True)
    xn = (z - mu) * lax.rsqrt(var + LN_EPS) * lng_ref[...] + lnb_ref[...]
    xo_ref[...] = xn
    if emit_h:
        ho_ref[...] = (xn * (1.0 + scn_ref[0]) + shn_ref[0]).astype(ho_ref.dtype)


def _mm_ln_call(a, w_bf, b, res, gate, lng, lnb, nxt, *, rows_per_batch, tm, name, ogate=None, normw=None):
    m = a.shape[0]
    k, d = w_bf.shape
    tpb = rows_per_batch // tm
    mlstm_prologue = ogate is not None
    emit_h = nxt is not None
    vec_b = pl.BlockSpec((1, 1, d), lambda i: (i // tpb, 0, 0))
    vec = pl.BlockSpec((1, d), lambda i: (0, 0))
    row_tile = pl.BlockSpec((tm, d), lambda i: (i, 0))

    in_specs = [pl.BlockSpec((tm, k), lambda i: (i, 0))]
    args = [a]
    asz = a.dtype.itemsize
    if mlstm_prologue:
        ocb = ogate[1]
        in_specs += [pl.BlockSpec((tm, k), lambda i: (i, ocb)), pl.BlockSpec((1, k), lambda i: (0, 0))]
        args += [ogate[0], normw.reshape(1, -1)]
    in_specs += [pl.BlockSpec((k, d), lambda i: (0, 0), pipeline_mode=pl.Buffered(1)),
                 vec, row_tile, vec_b, vec, vec]
    args += [w_bf, b.reshape(1, d), res, gate, lng.reshape(1, d), lnb.reshape(1, d)]
    out_specs = [row_tile]
    out_shape = [jax.ShapeDtypeStruct((m, d), F32)]
    if emit_h:
        in_specs += [vec_b, vec_b]
        args += [nxt[0], nxt[1]]
        out_specs.append(row_tile)
        out_shape.append(jax.ShapeDtypeStruct((m, d), BF16))
    vmem = (k * d * 2 + 2 * tm * k * asz * (2 if mlstm_prologue else 1)
            + 2 * tm * d * (4 + 4 + 2) + 2 * tm * d * 4)
    outs = pl.pallas_call(
        functools.partial(_mm_ln_kernel, mlstm_prologue=mlstm_prologue, emit_h=emit_h),
        grid=(m // tm,),
        in_specs=in_specs,
        out_specs=out_specs,
        out_shape=out_shape,
        compiler_params=_params(("arbitrary",), vmem),
        name=name,
    )(*args)
    return (outs[0], outs[1]) if emit_h else (outs[0], None)


def _rope_kernel(pos_ref, freq_ref, cos_ref, sin_ref):
    ang = pos_ref[0].astype(F32) * freq_ref[...]
    lane = lax.broadcasted_iota(jnp.int32, ang.shape, 1)
    first_half = (lane % ATTN_HEAD_DIM) < ATTN_HEAD_DIM // 2
    cos_ref[0] = jnp.cos(ang)
    s = jnp.sin(ang)
    sin_ref[0] = jnp.where(first_half, -s, s)


def _rope_call(positions):
    bsz, s = positions.shape
    half = ATTN_HEAD_DIM // 2
    inv_freq = 1.0 / (ROPE_THETA ** (np.arange(0, ATTN_HEAD_DIM, 2, dtype=np.float32) / ATTN_HEAD_DIM))
    freq = jnp.asarray(np.tile(inv_freq.astype(np.float32), V7X_LANES // half).reshape(1, V7X_LANES))
    tbl = jax.ShapeDtypeStruct((bsz, s, V7X_LANES), F32)
    return pl.pallas_call(
        _rope_kernel,
        grid=(bsz,),
        in_specs=[pl.BlockSpec((1, s, 1), lambda b: (b, 0, 0)), pl.BlockSpec((1, V7X_LANES), lambda b: (0, 0))],
        out_specs=[pl.BlockSpec((1, s, V7X_LANES), lambda b: (b, 0, 0))] * 2,
        out_shape=[tbl, tbl],
        compiler_params=_params(("arbitrary",), 6 * s * V7X_LANES * 4),
        name="rope_tables",
    )(positions.reshape(bsz, s, 1), freq)


def _rope(t, cos, sin_signed, first_half):
    half = ATTN_HEAD_DIM // 2
    rot = jnp.where(first_half, pltpu.roll(t, V7X_LANES - half, 1), pltpu.roll(t, half, 1))
    return t * cos + rot * sin_signed


ATTN_VT_ROWS = ATTN_HEAD_DIM + 16


def _attn_kernel(sink_ref, q_ref, k_ref, v_ref, cos_ref, sin_ref, o_ref, kp_ref, vt_ref, bias_ref, *, seq):
    blk = ATTN_BLOCK
    hd = ATTN_HEAD_DIM
    nb = seq // blk
    hp = pl.program_id(1)
    lane = lax.broadcasted_iota(jnp.int32, (seq, V7X_LANES), 1)
    low = lane < hd
    first_half = (lane % hd) < hd // 2
    cos = cos_ref[0]
    sin = sin_ref[0]

    scale = hd ** -0.5 * LOG2_E
    lane_q = lax.broadcasted_iota(jnp.int32, (blk, V7X_LANES), 1)
    first_half_q = (lane_q % hd) < hd // 2

    kr = _rope(k_ref[0], cos, sin, first_half)
    lo = jnp.where(low, kr, 0.0)
    hi = jnp.where(low, 0.0, kr)
    zeros_pad = jnp.zeros((blk, V7X_LANES), BF16)
    for idx, val in enumerate((lo, pltpu.roll(lo, hd, 1), pltpu.roll(hi, hd, 1), hi)):
        kp_ref[idx, 0:blk, :] = zeros_pad
        kp_ref[idx, blk:blk + seq, :] = val.astype(BF16)
        kp_ref[idx, blk + seq:, :] = zeros_pad

    vt = jnp.transpose(v_ref[0])
    ones_row = lax.broadcasted_iota(jnp.int32, (ATTN_VT_ROWS - hd, seq), 0) == 0
    zeros_vt = jnp.zeros((ATTN_VT_ROWS, blk), BF16)
    for j in range(2):
        vt_ref[j, :, 0:blk] = zeros_vt
        vt_ref[j, 0:hd, blk:blk + seq] = vt[j * hd:(j + 1) * hd].astype(BF16)
        vt_ref[j, hd:, blk:blk + seq] = jnp.where(ones_row, 1.0, 0.0).astype(BF16)
        vt_ref[j, :, blk + seq:] = zeros_vt

    ci = lax.broadcasted_iota(jnp.int32, (blk, 2 * blk), 0)
    ri = lax.broadcasted_iota(jnp.int32, (blk, 2 * blk), 1) % blk
    neg = jnp.full((blk, 2 * blk), -jnp.inf, F32)
    bias_ref[0] = jnp.where(ci >= ri, 0.0, -jnp.inf)
    bias_ref[1] = neg
    bias_ref[2] = jnp.where(ci <= ri, 0.0, -jnp.inf)
    bias_ref[3] = neg
    lane2 = lax.broadcasted_iota(jnp.int32, (1, 2 * blk), 1)

    chains = [(j, half) for j in range(2) for half in range(2)]

    def block_scores(i):
        r0 = i * blk
        cos_q = cos_ref[0, pl.ds(r0, blk), :]
        sin_q = sin_ref[0, pl.ds(r0, blk), :]
        q_tiles = []
        for t in range(ATTN_Q_TILES):
            qt = q_ref[0, pl.ds(r0, blk), t * V7X_LANES:(t + 1) * V7X_LANES]
            q_tiles.append((_rope(qt, cos_q, sin_q, first_half_q) * scale).astype(BF16))
        scores = []
        for j, half in chains:
            qcat = jnp.concatenate([q_tiles[2 * j], q_tiles[2 * j + 1]], axis=0)
            kk = kp_ref[2 * j + half, pl.ds(r0, 3 * blk), :]
            scores.append(lax.dot_general(kk, qcat, _CONTRACT_LAST, preferred_element_type=F32))
        return scores

    def block_probs(i, scores):
        bias_prev = bias_ref[1 if i == 0 else 0]
        bias_next = bias_ref[3 if i == nb - 1 else 2]
        probs = []
        for (j, half), st in zip(chains, scores):
            head = hp * ATTN_Q_HEADS_PER_PAIR + ATTN_GROUP * j + half
            snk = jnp.where(lane2 < blk, sink_ref[head], sink_ref[head + 2]) * LOG2_E
            parts = (st[0:blk] + bias_prev, st[blk:2 * blk], st[2 * blk:] + bias_next)
            m = snk
            for part in parts:
                m = jnp.maximum(m, jnp.max(part, axis=0, keepdims=True))
            p = jnp.concatenate([jnp.exp2(part - m) for part in parts], axis=0)
            probs.append((p.astype(BF16), jnp.exp2(snk - m)))
        return probs

    def block_output(i, probs):
        r0 = i * blk
        outs = []
        for (j, half), (p, sink_p) in zip(chains, probs):
            vtw = vt_ref[j, :, pl.ds(r0, 3 * blk)]
            ot = jnp.dot(vtw, p, preferred_element_type=F32)
            outs.append(ot[0:hd] * (1.0 / (ot[hd:hd + 1] + sink_p)))
        for t in range(ATTN_Q_TILES):
            j, tt = divmod(t, 2)
            cols = slice(tt * blk, (tt + 1) * blk)
            tile_t = jnp.concatenate([outs[2 * j][:, cols], outs[2 * j + 1][:, cols]], axis=0)
            o_ref[0, pl.ds(r0, blk), t * V7X_LANES:(t + 1) * V7X_LANES] = jnp.transpose(tile_t).astype(o_ref.dtype)

    scores = block_scores(0)
    for i in range(nb):
        probs = block_probs(i, scores)
        if i + 1 < nb:
            scores = block_scores(i + 1)
        block_output(i, probs)


def _attn_call(proj, cos, sin, sink, bsz, seq):
    n_pairs = ATTN_KV_HEADS // 2
    qw = ATTN_Q_TILES * V7X_LANES
    k_off = (proj.shape[-1] - 2 * ATTN_KV_HEADS * ATTN_HEAD_DIM) // V7X_LANES
    v_off = k_off + n_pairs
    assert seq // ATTN_BLOCK >= 2 and ATTN_BLOCK == V7X_LANES and 2 * ATTN_HEAD_DIM == V7X_LANES
    assert WINDOW == ATTN_BLOCK
    pad_seq = seq + 2 * ATTN_BLOCK
    tbl = pl.BlockSpec((1, seq, V7X_LANES), lambda b, p: (b, 0, 0))
    vmem = (2 * seq * (qw * 4 + 4 * V7X_LANES * 4 + qw * 2) + 4 * pad_seq * V7X_LANES * 2
            + 2 * ATTN_VT_ROWS * pad_seq * 2 + 3 * 3 * ATTN_BLOCK * 2 * ATTN_BLOCK * 4 + 6 * seq * V7X_LANES * 4)
    return pl.pallas_call(
        functools.partial(_attn_kernel, seq=seq),
        grid=(bsz, n_pairs),
        in_specs=[
            pl.BlockSpec(memory_space=pltpu.SMEM),
            pl.BlockSpec((1, seq, qw), lambda b, p: (b, 0, p)),
            pl.BlockSpec((1, seq, V7X_LANES), lambda b, p: (b, 0, k_off + p)),
            pl.BlockSpec((1, seq, V7X_LANES), lambda b, p: (b, 0, v_off + p)),
            tbl, tbl,
        ],
        out_specs=pl.BlockSpec((1, seq, qw), lambda b, p: (b, 0, p)),
        out_shape=jax.ShapeDtypeStruct((bsz, seq, n_pairs * qw), BF16),
        scratch_shapes=[
            pltpu.VMEM((4, pad_seq, V7X_LANES), BF16),
            pltpu.VMEM((2, ATTN_VT_ROWS, pad_seq), BF16),
            pltpu.VMEM((4, ATTN_BLOCK, 2 * ATTN_BLOCK), F32),
        ],
        compiler_params=_params(("arbitrary", "arbitrary"), vmem),
        name="swa_sink_attention",
    )(sink, proj, proj, proj, cos, sin)


def _gates_kernel(h_ref, w_ref, b_ref, o_ref):
    g_t = lax.dot_general(w_ref[...].astype(BF16), h_ref[...], _CONTRACT_LAST, preferred_element_type=F32)
    o_ref[0] = g_t + b_ref[...]


def _gates_call(h, w_t, layer, b_col, bsz, seq):
    m, k = h.shape
    tm = GATES_ROW_TILE
    tpb = seq // tm
    gate_blk = (w_t.shape[1] - MLSTM_GATES) // MLSTM_GATES
    return pl.pallas_call(
        _gates_kernel,
        grid=(m // tm,),
        in_specs=[
            pl.BlockSpec((tm, k), lambda i: (i, 0)),
            pl.BlockSpec((None, MLSTM_GATES, k), lambda i: (layer, gate_blk, 0)),
            pl.BlockSpec((MLSTM_GATES, 1), lambda i: (0, 0)),
        ],
        out_specs=pl.BlockSpec((1, MLSTM_GATES, tm), lambda i: (i // tpb, 0, i % tpb)),
        out_shape=jax.ShapeDtypeStruct((bsz, MLSTM_GATES, seq), F32),
        compiler_params=_params(("arbitrary",), 2 * tm * k * 2 + 4 * MLSTM_GATES * (k + tm) * 4),
        name="mlstm_gates",
    )(h, w_t, b_col)


def _log_sigmoid(x):
    return jnp.minimum(x, 0.0) - jnp.log1p(jnp.exp(-jnp.abs(x)))


def _mlstm_kernel(q_ref, k_ref, v_ref, g_ref, o_ref, qb_ref, kt_ref, vx_ref, c_ref, *, seq, chunk, dk, dv):
    nh = MLSTM_HEADS
    head = pl.program_id(1)
    nc = seq // chunk
    qb_ref[...] = (q_ref[0] * (dk ** -0.5)).astype(BF16)
    kt_ref[...] = jnp.transpose(k_ref[0])
    vx_ref[:, :dv] = v_ref[0].astype(BF16)
    ones_lane = lax.broadcasted_iota(jnp.int32, (seq, V7X_LANES), 1) == 0
    vx_ref[:, dv:] = jnp.where(ones_lane, 1.0, 0.0).astype(BF16)

    ti = lax.broadcasted_iota(jnp.int32, (chunk, chunk), 0)
    ui = lax.broadcasted_iota(jnp.int32, (chunk, chunk), 1)
    eye = ti == ui
    for direction in range(2):
        seen = (ui <= ti) if direction == 0 else (ui >= ti)
        c_ref[...] = jnp.zeros_like(c_ref)
        m_prev = jnp.full((1, 1), -1e30, F32)
        order = range(nc) if direction == 0 else range(nc - 1, -1, -1)
        for c in order:
            rows = slice(c * chunk, (c + 1) * chunk)
            gate_row = 2 * direction * nh + head
            li = g_ref[0, pl.ds(gate_row, 1), rows]
            lf = _log_sigmoid(g_ref[0, pl.ds(gate_row + nh, 1), rows])
            g_col = jnp.sum(jnp.where(seen, lf, 0.0), axis=1, keepdims=True)
            g_row = jnp.sum(jnp.where(eye, g_col, 0.0), axis=0, keepdims=True)
            g_tot = jnp.sum(lf, axis=1, keepdims=True)
            dm = jnp.where(seen, g_col - g_row + li, -jnp.inf)
            a = g_col + m_prev
            m_t = jnp.maximum(a, jnp.max(dm, axis=1, keepdims=True))
            p = jnp.exp(dm - m_t)
            ea = jnp.exp(a - m_t)

            qc = qb_ref[rows, :]
            ktc = kt_ref[:, rows]
            vxc = vx_ref[rows, :]
            sqk = jnp.dot(qc, ktc.astype(BF16), preferred_element_type=F32)
            sc = (sqk * p).astype(BF16)
            tot = (ea * jnp.dot(qc, c_ref[...].astype(BF16), preferred_element_type=F32)
                   + jnp.dot(sc, vxc, preferred_element_type=F32))
            den = tot[:, dv:dv + 1]
            hh = tot[:, :dv] * (1.0 / jnp.maximum(jnp.abs(den), jnp.exp(-m_t)))
            if direction == 0:
                o_ref[0, rows, :] = hh
            else:
                o_ref[0, rows, :] += hh

            w_log = g_tot - g_row + li
            m_new = jnp.maximum(g_tot + m_prev, jnp.max(w_log, axis=1, keepdims=True))
            decay = jnp.exp(g_tot + m_prev - m_new)
            w = jnp.exp(w_log - m_new)
            c_ref[...] = decay * c_ref[...] + jnp.dot((ktc * w).astype(BF16), vxc, preferred_element_type=F32)
            m_prev = m_new


def _mlstm_call(proj, gates_t, bsz, seq, dk, dv):
    nh = MLSTM_HEADS
    chunk = MLSTM_KERNEL_CHUNK
    k_blk = nh
    v_blk = (2 * nh * dk) // dv
    dvx = dv + V7X_LANES
    vmem = (2 * seq * (2 * dk + 2 * dv) * 4 + 2 * MLSTM_GATES * seq * 4
            + seq * dk * 2 + dk * seq * 4 + seq * dvx * 2 + dk * dvx * 4 + 8 * chunk * dvx * 4)
    return pl.pallas_call(
        functools.partial(_mlstm_kernel, seq=seq, chunk=chunk, dk=dk, dv=dv),
        grid=(bsz, nh),
        in_specs=[
            pl.BlockSpec((1, seq, dk), lambda b, h: (b, 0, h)),
            pl.BlockSpec((1, seq, dk), lambda b, h: (b, 0, k_blk + h)),
            pl.BlockSpec((1, seq, dv), lambda b, h: (b, 0, v_blk + h)),
            pl.BlockSpec((1, MLSTM_GATES, seq), lambda b, h: (b, 0, 0)),
        ],
        out_specs=pl.BlockSpec((1, seq, dv), lambda b, h: (b, 0, h)),
        out_shape=jax.ShapeDtypeStruct((bsz, seq, nh * dv), F32),
        scratch_shapes=[
            pltpu.VMEM((seq, dk), BF16),
            pltpu.VMEM((dk, seq), F32),
            pltpu.VMEM((seq, dvx), BF16),
            pltpu.VMEM((dk, dvx), F32),
        ],
        compiler_params=_params(("arbitrary", "arbitrary"), vmem),
        name="bidir_mlstm",
    )(proj, proj, proj, gates_t)


def kernel(x, c, positions, attn_w_qkv, attn_b_qkv, attn_sink, attn_w_o, attn_b_o, mlstm_w_in, mlstm_b_in,
           mlstm_norm_w, mlstm_w_o, mlstm_b_o, mod_w, mod_b, mlp_w1, mlp_b1, mlp_w2, mlp_b2,
           ln_mix_g, ln_mix_b, ln_mlp_g, ln_mlp_b):
    bsz, seq, d = x.shape
    depth = mod_w.shape[0]
    assert depth == DEPTH
    m = bsz * seq
    d_ff = mlp_w1.shape[-1]
    ml_main = mlstm_w_in.shape[-1] - MLSTM_GATES
    ml_dv = mlstm_w_o.shape[1] // MLSTM_HEADS
    ml_dk = (ml_main - 2 * MLSTM_HEADS * ml_dv) // (2 * MLSTM_HEADS)

    c_pad = jnp.pad(c, ((0, V7X_SUBLANES - bsz), (0, 0)))
    mod = _mod_call(c_pad, mod_w, mod_b)[:, :bsz]
    mod = mod.reshape(depth, bsz, 6, 1, d)
    sh_m, sc_m, g_m, sh_f, sc_f, g_f = (mod[:, :, j] for j in range(6))

    cos, sin = _rope_call(positions)
    x2 = x.reshape(m, d)
    h = _modulate_call(x2, sc_m[0], sh_m[0], seq)

    mlstm_w_in_t = jnp.swapaxes(mlstm_w_in, 1, 2)

    for i in range(depth):
        j = i // 2
        if i % 2 == 0:
            proj, w_o_bf = _mm_call(h, attn_w_qkv, j, attn_b_qkv[j], n_cols=attn_w_qkv.shape[-1], out_dtype=F32,
                                    side=(attn_w_o, j), row_splits=2, name="attn_qkv_proj")
            a = _attn_call(proj.reshape(bsz, seq, -1), cos, sin, attn_sink[j], bsz, seq).reshape(m, d)
            x2, h = _mm_ln_call(a, w_o_bf, attn_b_o[j], x2, g_m[i], ln_mix_g[i], ln_mix_b[i],
                                (sc_f[i], sh_f[i]), rows_per_batch=seq, tm=LN_PROJ_ROW_TILE, name="attn_out_ln")
        else:
            proj, w_o_bf = _mm_call(h, mlstm_w_in_t, j, mlstm_b_in[j], n_cols=ml_main, out_dtype=F32,
                                    w_transposed=True, side=(mlstm_w_o, j), row_splits=2, name="mlstm_in_proj")
            gates_t = _gates_call(h, mlstm_w_in_t, j, mlstm_b_in[j][ml_main:].reshape(MLSTM_GATES, 1), bsz, seq)
            hs = _mlstm_call(proj.reshape(bsz, seq, ml_main), gates_t, bsz, seq, ml_dk, ml_dv).reshape(m, d)
            x2, h = _mm_ln_call(hs, w_o_bf, mlstm_b_o[j], x2, g_m[i], ln_mix_g[i], ln_mix_b[i],
                                (sc_f[i], sh_f[i]), rows_per_batch=seq, tm=LN_PROJ_ROW_TILE, name="mlstm_out_ln",
                                ogate=(proj, (ml_main - d) // d), normw=mlstm_norm_w[j])
        u, w2_bf = _mm_call(h, mlp_w1, i, mlp_b1[i], n_cols=d_ff, out_dtype=BF16, act="relu2", side=(mlp_w2, i),
                            row_splits=2, name="mlp_up")
        nxt = (sc_m[i + 1], sh_m[i + 1]) if i + 1 < depth else None
        x2, h = _mm_ln_call(u, w2_bf, mlp_b2[i], x2, g_f[i], ln_mlp_g[i], ln_mlp_b[i], nxt,
                            rows_per_batch=seq, tm=MLP_DOWN_ROW_TILE, name="mlp_down_ln")
    return x2.reshape(bsz, seq, d)
```

```python
import functools

import numpy as np
import jax
import jax.numpy as jnp
from jax import lax
from jax.experimental import pallas as pl
from jax.experimental.pallas import tpu as pltpu

F32 = jnp.float32
BF16 = jnp.bfloat16

DEPTH = 2
ATTN_HEAD_DIM = 64
ATTN_KV_HEADS = 8
WINDOW = 128
ATTN_BLOCK = 128
ROPE_THETA = 10000.0
MLSTM_HEADS = 4
MLSTM_GATES = 4 * MLSTM_HEADS
DEEPNORM_ALPHA = (2.0 * DEPTH) ** 0.25
LN_EPS = 1e-5
HEAD_NORM_EPS = 1e-6
MLSTM_KERNEL_CHUNK = 256
LOG2_E = 1.4426950408889634

V7X_LANES = 128
V7X_SUBLANES = 8
V7X_VMEM_BYTES = 64 * 1024 * 1024
VMEM_LIMIT_CAP = 63 * 1024 * 1024

MOD_COL_TILE = 1024
ELEMENTWISE_ROW_TILE = 2048
MM_ROW_TILE = 1024
MM_COL_TILE = 1024
LN_PROJ_ROW_TILE = 512
MLP_DOWN_ROW_TILE = 256
GATES_ROW_TILE = 1024

ATTN_GROUP = 4
ATTN_Q_TILES = 2 * ATTN_GROUP * ATTN_HEAD_DIM // V7X_LANES
ATTN_Q_HEADS_PER_PAIR = 2 * ATTN_GROUP
VMEM_LIMIT_FLOOR = V7X_VMEM_BYTES - 28 * 1024 * 1024


def _params(semantics, vmem_bytes):
    limit = max(min(int(vmem_bytes * 1.25) + (4 << 20), VMEM_LIMIT_CAP), VMEM_LIMIT_FLOOR)
    return pltpu.CompilerParams(dimension_semantics=semantics, vmem_limit_bytes=limit)


def _mod_kernel(c_ref, w_ref, b_ref, o_ref):
    c = c_ref[...]
    c_act = (c * jax.nn.sigmoid(c)).astype(BF16)
    o_ref[0] = jnp.dot(c_act, w_ref[0].astype(BF16), preferred_element_type=F32) + b_ref[0]


def _mod_call(c_pad, mod_w, mod_b):
    depth, d, n = mod_w.shape
    rows = c_pad.shape[0]
    tn = MOD_COL_TILE
    return pl.pallas_call(
        _mod_kernel,
        grid=(depth, n // tn),
        in_specs=[
            pl.BlockSpec((rows, d), lambda i, j: (0, 0)),
            pl.BlockSpec((1, d, tn), lambda i, j: (i, 0, j)),
            pl.BlockSpec((1, 1, tn), lambda i, j: (i, 0, j)),
        ],
        out_specs=pl.BlockSpec((1, rows, tn), lambda i, j: (i, 0, j)),
        out_shape=jax.ShapeDtypeStruct((depth, rows, n), F32),
        compiler_params=_params(("arbitrary", "arbitrary"), 2 * d * tn * 4 + d * tn * 2),
        name="adaln_mod",
    )(c_pad, mod_w, mod_b.reshape(depth, 1, n))


def _modulate_kernel(x_ref, sc_ref, sh_ref, o_ref):
    o_ref[...] = (x_ref[...] * (1.0 + sc_ref[0]) + sh_ref[0]).astype(o_ref.dtype)


def _modulate_call(x2, sc, sh, rows_per_batch):
    m, d = x2.shape
    tm = ELEMENTWISE_ROW_TILE
    tpb = rows_per_batch // tm
    vec = pl.BlockSpec((1, 1, d), lambda i: (i // tpb, 0, 0))
    return pl.pallas_call(
        _modulate_kernel,
        grid=(m // tm,),
        in_specs=[pl.BlockSpec((tm, d), lambda i: (i, 0)), vec, vec],
        out_specs=pl.BlockSpec((tm, d), lambda i: (i, 0)),
        out_shape=jax.ShapeDtypeStruct((m, d), BF16),
        compiler_params=_params(("arbitrary",), 2 * tm * d * 6),
        name="modulate_in",
    )(x2, sc, sh)


_CONTRACT_LAST = (((1,), (1,)), ((), ()))


def _mm_kernel(*refs, act, w_transposed, side_cast, row_splits):
    if side_cast:
        x_ref, w_ref, b_ref, side_ref, o_ref, side_o_ref, wb_ref = refs
        side_o_ref[...] = side_ref[...].astype(side_o_ref.dtype)
    else:
        x_ref, w_ref, b_ref, o_ref, wb_ref = refs

    @pl.when(pl.program_id(1) == 0)
    def _():
        wb_ref[...] = w_ref[...].astype(BF16)

    for r in range(row_splits):
        rows = slice(r * MM_ROW_TILE, (r + 1) * MM_ROW_TILE)
        if w_transposed:
            acc = lax.dot_general(x_ref[rows, :], wb_ref[...], _CONTRACT_LAST, preferred_element_type=F32)
        else:
            acc = jnp.dot(x_ref[rows, :], wb_ref[...], preferred_element_type=F32)
        acc = acc + b_ref[...]
        if act == "relu2":
            acc = jnp.square(jnp.maximum(acc, 0.0))
        o_ref[rows, :] = acc.astype(o_ref.dtype)


def _mm_call(x, w, layer, b, *, n_cols, out_dtype, act=None, w_transposed=False, side=None, row_splits=1, name):
    m, k = x.shape
    tm, tn = MM_ROW_TILE * row_splits, MM_COL_TILE
    osz = jnp.dtype(out_dtype).itemsize
    vmem = 2 * k * tn * 4 + k * tn * 2 + 2 * tm * k * 2 + 2 * tm * tn * osz + MM_ROW_TILE * tn * 4
    if w_transposed:
        w_spec = pl.BlockSpec((None, tn, k), lambda j, i: (layer, j, 0))
        w_tile = (tn, k)
    else:
        w_spec = pl.BlockSpec((None, k, tn), lambda j, i: (layer, 0, j))
        w_tile = (k, tn)
    n_i = m // tm
    grid = (n_cols // tn, n_i)
    in_specs = [pl.BlockSpec((tm, k), lambda j, i: (i, 0)), w_spec, pl.BlockSpec((1, tn), lambda j, i: (0, j))]
    args = [x, w, b.reshape(1, -1)]
    out_specs = [pl.BlockSpec((tm, tn), lambda j, i: (i, j))]
    out_shape = [jax.ShapeDtypeStruct((m, n_cols), out_dtype)]
    if side is not None:
        w2, layer2 = side
        _, k2, n2 = w2.shape
        n_blk = 1 << ((grid[0] * grid[1]).bit_length() - 1)
        rows = k2 // n_blk
        assert rows * n_blk == k2 and rows % 16 == 0
        in_specs.append(pl.BlockSpec((None, rows, n2), lambda j, i: (layer2, jnp.minimum(j * n_i + i, n_blk - 1), 0)))
        args.append(w2)
        out_specs.append(pl.BlockSpec((rows, n2), lambda j, i: (jnp.minimum(j * n_i + i, n_blk - 1), 0)))
        out_shape.append(jax.ShapeDtypeStruct((k2, n2), BF16))
        vmem += 2 * rows * n2 * 6
    outs = pl.pallas_call(
        functools.partial(_mm_kernel, act=act, w_transposed=w_transposed, side_cast=side is not None,
                          row_splits=row_splits),
        grid=grid,
        in_specs=in_specs,
        out_specs=out_specs,
        out_shape=out_shape,
        scratch_shapes=[pltpu.VMEM(w_tile, BF16)],
        compiler_params=_params(("arbitrary", "arbitrary"), vmem),
        name=name,
    )(*args)
    return (outs[0], outs[1]) if side is not None else outs[0]


def _head_norm_gate(hs, og, nw, heads):
    dv = hs.shape[-1] // heads
    outs = []
    for h in range(heads):
        t = hs[:, h * dv:(h + 1) * dv]
        mu = jnp.mean(t, axis=-1, keepdims=True)
        var = jnp.mean(jnp.square(t - mu), axis=-1, keepdims=True)
        outs.append((t - mu) * lax.rsqrt(var + HEAD_NORM_EPS))
    hn = jnp.concatenate(outs, axis=-1) * nw
    return jax.nn.sigmoid(og) * hn


def _mm_ln_kernel(*refs, mlstm_prologue, emit_h):
    refs = list(refs)
    a_ref = refs.pop(0)
    if mlstm_prologue:
        og_ref = refs.pop(0)
        nw_ref = refs.pop(0)
    w_ref, b_ref, res_ref, gate_ref, lng_ref, lnb_ref = refs[:6]
    refs = refs[6:]
    if emit_h:
        scn_ref, shn_ref = refs[:2]
        refs = refs[2:]
    xo_ref = refs.pop(0)
    if emit_h:
        ho_ref = refs.pop(0)

    if mlstm_prologue:
        a = _head_norm_gate(a_ref[...], og_ref[...], nw_ref[...], MLSTM_HEADS).astype(BF16)
    else:
        a = a_ref[...]
    y = jnp.dot(a, w_ref[...], preferred_element_type=F32) + b_ref[...]
    z = DEEPNORM_ALPHA * res_ref[...] + (1.0 + gate_ref[0]) * y
    mu = jnp.mean(z, axis=-1, keepdims=True)
    var = jnp.mean(jnp.square(z - mu), axis=-1, keepdims=True)
    xn = (z - mu) * lax.rsqrt(var + LN_EPS) * lng_ref[...] + lnb_ref[...]
    xo_ref[...] = xn
    if emit_h:
        ho_ref[...] = (xn * (1.0 + scn_ref[0]) + shn_ref[0]).astype(ho_ref.dtype)


def _mm_ln_call(a, w_bf, b, res, gate, lng, lnb, nxt, *, rows_per_batch, tm, name, ogate=None, normw=None):
    m = a.shape[0]
    k, d = w_bf.shape
    tpb = rows_per_batch // tm
    mlstm_prologue = ogate is not None
    emit_h = nxt is not None
    vec_b = pl.BlockSpec((1, 1, d), lambda i: (i // tpb, 0, 0))
    vec = pl.BlockSpec((1, d), lambda i: (0, 0))
    row_tile = pl.BlockSpec((tm, d), lambda i: (i, 0))

    in_specs = [pl.BlockSpec((tm, k), lambda i: (i, 0))]
    args = [a]
    asz = a.dtype.itemsize
    if mlstm_prologue:
        ocb = ogate[1]
        in_specs += [pl.BlockSpec((tm, k), lambda i: (i, ocb)), pl.BlockSpec((1, k), lambda i: (0, 0))]
        args += [ogate[0], normw.reshape(1, -1)]
    in_specs += [pl.BlockSpec((k, d), lambda i: (0, 0), pipeline_mode=pl.Buffered(1)),
                 vec, row_tile, vec_b, vec, vec]
    args += [w_bf, b.reshape(1, d), res, gate, lng.reshape(1, d), lnb.reshape(1, d)]
    out_specs = [row_tile]
    out_shape = [jax.ShapeDtypeStruct((m, d), F32)]
    if emit_h:
        in_specs += [vec_b, vec_b]
        args += [nxt[0], nxt[1]]
        out_specs.append(row_tile)
        out_shape.append(jax.ShapeDtypeStruct((m, d), BF16))
    vmem = (k * d * 2 + 2 * tm * k * asz * (2 if mlstm_prologue else 1)
            + 2 * tm * d * (4 + 4 + 2) + 2 * tm * d * 4)
    outs = pl.pallas_call(
        functools.partial(_mm_ln_kernel, mlstm_prologue=mlstm_prologue, emit_h=emit_h),
        grid=(m // tm,),
        in_specs=in_specs,
        out_specs=out_specs,
        out_shape=out_shape,
        compiler_params=_params(("arbitrary",), vmem),
        name=name,
    )(*args)
    return (outs[0], outs[1]) if emit_h else (outs[0], None)


def _rope_kernel(pos_ref, freq_ref, cos_ref, sin_ref):
    ang = pos_ref[0].astype(F32) * freq_ref[...]
    lane = lax.broadcasted_iota(jnp.int32, ang.shape, 1)
    first_half = (lane % ATTN_HEAD_DIM) < ATTN_HEAD_DIM // 2
    cos_ref[0] = jnp.cos(ang)
    s = jnp.sin(ang)
    sin_ref[0] = jnp.where(first_half, -s, s)


def _rope_call(positions):
    bsz, s = positions.shape
    half = ATTN_HEAD_DIM // 2
    inv_freq = 1.0 / (ROPE_THETA ** (np.arange(0, ATTN_HEAD_DIM, 2, dtype=np.float32) / ATTN_HEAD_DIM))
    freq = jnp.asarray(np.tile(inv_freq.astype(np.float32), V7X_LANES // half).reshape(1, V7X_LANES))
    tbl = jax.ShapeDtypeStruct((bsz, s, V7X_LANES), F32)
    return pl.pallas_call(
        _rope_kernel,
        grid=(bsz,),
        in_specs=[pl.BlockSpec((1, s, 1), lambda b: (b, 0, 0)), pl.BlockSpec((1, V7X_LANES), lambda b: (0, 0))],
        out_specs=[pl.BlockSpec((1, s, V7X_LANES), lambda b: (b, 0, 0))] * 2,
        out_shape=[tbl, tbl],
        compiler_params=_params(("arbitrary",), 6 * s * V7X_LANES * 4),
        name="rope_tables",
    )(positions.reshape(bsz, s, 1), freq)


def _rope(t, cos, sin_signed, first_half):
    half = ATTN_HEAD_DIM // 2
    rot = jnp.where(first_half, pltpu.roll(t, V7X_LANES - half, 1), pltpu.roll(t, half, 1))
    return t * cos + rot * sin_signed


ATTN_VT_ROWS = ATTN_HEAD_DIM + 16


def _attn_kernel(sink_ref, q_ref, k_ref, v_ref, cos_ref, sin_ref, o_ref, kp_ref, vt_ref, bias_ref, *, seq):
    blk = ATTN_BLOCK
    hd = ATTN_HEAD_DIM
    nb = seq // blk
    hp = pl.program_id(1)
    lane = lax.broadcasted_iota(jnp.int32, (seq, V7X_LANES), 1)
    low = lane < hd
    first_half = (lane % hd) < hd // 2
    cos = cos_ref[0]
    sin = sin_ref[0]

    scale = hd ** -0.5 * LOG2_E
    lane_q = lax.broadcasted_iota(jnp.int32, (blk, V7X_LANES), 1)
    first_half_q = (lane_q % hd) < hd // 2

    kr = _rope(k_ref[0], cos, sin, first_half)
    lo = jnp.where(low, kr, 0.0)
    hi = jnp.where(low, 0.0, kr)
    zeros_pad = jnp.zeros((blk, V7X_LANES), BF16)
    for idx, val in enumerate((lo, pltpu.roll(lo, hd, 1), pltpu.roll(hi, hd, 1), hi)):
        kp_ref[idx, 0:blk, :] = zeros_pad
        kp_ref[idx, blk:blk + seq, :] = val.astype(BF16)
        kp_ref[idx, blk + seq:, :] = zeros_pad

    vt = jnp.transpose(v_ref[0])
    ones_row = lax.broadcasted_iota(jnp.int32, (ATTN_VT_ROWS - hd, seq), 0) == 0
    zeros_vt = jnp.zeros((ATTN_VT_ROWS, blk), BF16)
    for j in range(2):
        vt_ref[j, :, 0:blk] = zeros_vt
        vt_ref[j, 0:hd, blk:blk + seq] = vt[j * hd:(j + 1) * hd].astype(BF16)
        vt_ref[j, hd:, blk:blk + seq] = jnp.where(ones_row, 1.0, 0.0).astype(BF16)
        vt_ref[j, :, blk + seq:] = zeros_vt

    ci = lax.broadcasted_iota(jnp.int32, (blk, 2 * blk), 0)
    ri = lax.broadcasted_iota(jnp.int32, (blk, 2 * blk), 1) % blk
    bias_ref[0] = jnp.where(ci >= ri, 0.0, -jnp.inf)
    bias_ref[1] = jnp.where(ci <= ri, 0.0, -jnp.inf)
    lane2 = lax.broadcasted_iota(jnp.int32, (1, 2 * blk), 1)

    chains = [(j, half) for j in range(2) for half in range(2)]

    def key_window(i):
        lo = blk if i == 0 else 0
        hi = 2 * blk if i == nb - 1 else 3 * blk
        return i * blk + lo, hi - lo

    def block_scores(i):
        r0 = i * blk
        cos_q = cos_ref[0, pl.ds(r0, blk), :]
        sin_q = sin_ref[0, pl.ds(r0, blk), :]
        q_tiles = []
        for t in range(ATTN_Q_TILES):
            qt = q_ref[0, pl.ds(r0, blk), t * V7X_LANES:(t + 1) * V7X_LANES]
            q_tiles.append((_rope(qt, cos_q, sin_q, first_half_q) * scale).astype(BF16))
        k0, klen = key_window(i)
        scores = []
        for j, half in chains:
            qcat = jnp.concatenate([q_tiles[2 * j], q_tiles[2 * j + 1]], axis=0)
            kk = kp_ref[2 * j + half, pl.ds(k0, klen), :]
            scores.append(lax.dot_general(kk, qcat, _CONTRACT_LAST, preferred_element_type=F32))
        return scores

    def block_probs(i, scores):
        probs = []
        for (j, half), st in zip(chains, scores):
            head = hp * ATTN_Q_HEADS_PER_PAIR + ATTN_GROUP * j + half
            snk = jnp.where(lane2 < blk, sink_ref[head], sink_ref[head + 2]) * LOG2_E
            parts = []
            row = 0
            if i > 0:
                parts.append(st[0:blk] + bias_ref[0])
                row = blk
            parts.append(st[row:row + blk])
            if i < nb - 1:
                parts.append(st[row + blk:row + 2 * blk] + bias_ref[1])
            m = snk
            for part in parts:
                m = jnp.maximum(m, jnp.max(part, axis=0, keepdims=True))
            p = jnp.concatenate([jnp.exp2(part - m) for part in parts], axis=0)
            probs.append((p.astype(BF16), jnp.exp2(snk - m)))
        return probs

    def block_output(i, probs):
        r0 = i * blk
        k0, klen = key_window(i)
        outs = []
        for (j, half), (p, sink_p) in zip(chains, probs):
            vtw = vt_ref[j, :, pl.ds(k0, klen)]
            ot = jnp.dot(vtw, p, preferred_element_type=F32)
            outs.append(ot[0:hd] * (1.0 / (ot[hd:hd + 1] + sink_p)))
        for t in range(ATTN_Q_TILES):
            j, tt = divmod(t, 2)
            cols = slice(tt * blk, (tt + 1) * blk)
            tile_t = jnp.concatenate([outs[2 * j][:, cols], outs[2 * j + 1][:, cols]], axis=0)
            o_ref[0, pl.ds(r0, blk), t * V7X_LANES:(t + 1) * V7X_LANES] = jnp.transpose(tile_t).astype(o_ref.dtype)

    scores = block_scores(0)
    for i in range(nb):
        probs = block_probs(i, scores)
        if i + 1 < nb:
            scores = block_scores(i + 1)
        block_output(i, probs)


def _attn_call(proj, cos, sin, sink, bsz, seq):
    n_pairs = ATTN_KV_HEADS // 2
    qw = ATTN_Q_TILES * V7X_LANES
    k_off = (proj.shape[-1] - 2 * ATTN_KV_HEADS * ATTN_HEAD_DIM) // V7X_LANES
    v_off = k_off + n_pairs
    assert seq // ATTN_BLOCK >= 2 and ATTN_BLOCK == V7X_LANES and 2 * ATTN_HEAD_DIM == V7X_LANES
    assert WINDOW == ATTN_BLOCK
    pad_seq = seq + 2 * ATTN_BLOCK
    tbl = pl.BlockSpec((1, seq, V7X_LANES), lambda b, p: (b, 0, 0))
    vmem = (2 * seq * (qw * 4 + 4 * V7X_LANES * 4 + qw * 2) + 4 * pad_seq * V7X_LANES * 2
            + 2 * ATTN_VT_ROWS * pad_seq * 2 + 3 * 3 * ATTN_BLOCK * 2 * ATTN_BLOCK * 4 + 6 * seq * V7X_LANES * 4)
    return pl.pallas_call(
        functools.partial(_attn_kernel, seq=seq),
        grid=(bsz, n_pairs),
        in_specs=[
            pl.BlockSpec(memory_space=pltpu.SMEM),
            pl.BlockSpec((1, seq, qw), lambda b, p: (b, 0, p)),
            pl.BlockSpec((1, seq, V7X_LANES), lambda b, p: (b, 0, k_off + p)),
            pl.BlockSpec((1, seq, V7X_LANES), lambda b, p: (b, 0, v_off + p)),
            tbl, tbl,
        ],
        out_specs=pl.BlockSpec((1, seq, qw), lambda b, p: (b, 0, p)),
        out_shape=jax.ShapeDtypeStruct((bsz, seq, n_pairs * qw), BF16),
        scratch_shapes=[
            pltpu.VMEM((4, pad_seq, V7X_LANES), BF16),
            pltpu.VMEM((2, ATTN_VT_ROWS, pad_seq), BF16),
            pltpu.VMEM((2, ATTN_BLOCK, 2 * ATTN_BLOCK), F32),
        ],
        compiler_params=_params(("arbitrary", "arbitrary"), vmem),
        name="swa_sink_attention",
    )(sink, proj, proj, proj, cos, sin)


def _gates_kernel(h_ref, w_ref, b_ref, o_ref):
    g_t = lax.dot_general(w_ref[...].astype(BF16), h_ref[...], _CONTRACT_LAST, preferred_element_type=F32)
    o_ref[0] = g_t + b_ref[...]


def _gates_call(h, w_t, layer, b_col, bsz, seq):
    m, k = h.shape
    tm = GATES_ROW_TILE
    tpb = seq // tm
    gate_blk = (w_t.shape[1] - MLSTM_GATES) // MLSTM_GATES
    return pl.pallas_call(
        _gates_kernel,
        grid=(m // tm,),
        in_specs=[
            pl.BlockSpec((tm, k), lambda i: (i, 0)),
            pl.BlockSpec((None, MLSTM_GATES, k), lambda i: (layer, gate_blk, 0)),
            pl.BlockSpec((MLSTM_GATES, 1), lambda i: (0, 0)),
        ],
        out_specs=pl.BlockSpec((1, MLSTM_GATES, tm), lambda i: (i // tpb, 0, i % tpb)),
        out_shape=jax.ShapeDtypeStruct((bsz, MLSTM_GATES, seq), F32),
        compiler_params=_params(("arbitrary",), 2 * tm * k * 2 + 4 * MLSTM_GATES * (k + tm) * 4),
        name="mlstm_gates",
    )(h, w_t, b_col)


def _log_sigmoid(x):
    return jnp.minimum(x, 0.0) - jnp.log1p(jnp.exp(-jnp.abs(x)))


def _mlstm_kernel(q_ref, k_ref, v_ref, g_ref, o_ref, qb_ref, kt_ref, vx_ref, c_ref, *, seq, chunk, dk, dv):
    nh = MLSTM_HEADS
    head = pl.program_id(1)
    nc = seq // chunk
    qb_ref[...] = (q_ref[0] * (dk ** -0.5)).astype(BF16)
    kt_ref[...] = jnp.transpose(k_ref[0])
    vx_ref[:, :dv] = v_ref[0].astype(BF16)
    ones_lane = lax.broadcasted_iota(jnp.int32, (seq, V7X_LANES), 1) == 0
    vx_ref[:, dv:] = jnp.where(ones_lane, 1.0, 0.0).astype(BF16)

    ti = lax.broadcasted_iota(jnp.int32, (chunk, chunk), 0)
    ui = lax.broadcasted_iota(jnp.int32, (chunk, chunk), 1)
    eye = ti == ui
    for direction in range(2):
        seen = (ui <= ti) if direction == 0 else (ui >= ti)
        c_ref[...] = jnp.zeros_like(c_ref)
        m_prev = jnp.full((1, 1), -1e30, F32)
        order = range(nc) if direction == 0 else range(nc - 1, -1, -1)
        for c in order:
            rows = slice(c * chunk, (c + 1) * chunk)
            gate_row = 2 * direction * nh + head
            li = g_ref[0, pl.ds(gate_row, 1), rows]
            lf = _log_sigmoid(g_ref[0, pl.ds(gate_row + nh, 1), rows])
            g_col = jnp.sum(jnp.where(seen, lf, 0.0), axis=1, keepdims=True)
            g_row = jnp.sum(jnp.where(eye, g_col, 0.0), axis=0, keepdims=True)
            g_tot = jnp.sum(lf, axis=1, keepdims=True)
            dm = jnp.where(seen, g_col - g_row + li, -jnp.inf)
            a = g_col + m_prev
            m_t = jnp.maximum(a, jnp.max(dm, axis=1, keepdims=True))
            p = jnp.exp(dm - m_t)
            ea = jnp.exp(a - m_t)

            qc = qb_ref[rows, :]
            ktc = kt_ref[:, rows]
            vxc = vx_ref[rows, :]
            sqk = jnp.dot(qc, ktc.astype(BF16), preferred_element_type=F32)
            sc = (sqk * p).astype(BF16)
            tot = (ea * jnp.dot(qc, c_ref[...].astype(BF16), preferred_element_type=F32)
                   + jnp.dot(sc, vxc, preferred_element_type=F32))
            den = tot[:, dv:dv + 1]
            hh = tot[:, :dv] * (1.0 / jnp.maximum(jnp.abs(den), jnp.exp(-m_t)))
            if direction == 0:
                o_ref[0, rows, :] = hh
            else:
                o_ref[0, rows, :] += hh

            w_log = g_tot - g_row + li
            m_new = jnp.maximum(g_tot + m_prev, jnp.max(w_log, axis=1, keepdims=True))
            decay = jnp.exp(g_tot + m_prev - m_new)
            w = jnp.exp(w_log - m_new)
            c_ref[...] = decay * c_ref[...] + jnp.dot((ktc * w).astype(BF16), vxc, preferred_element_type=F32)
            m_prev = m_new


def _mlstm_call(proj, gates_t, bsz, seq, dk, dv):
    nh = MLSTM_HEADS
    chunk = MLSTM_KERNEL_CHUNK
    k_blk = nh
    v_blk = (2 * nh * dk) // dv
    dvx = dv + V7X_LANES
    vmem = (2 * seq * (2 * dk + 2 * dv) * 4 + 2 * MLSTM_GATES * seq * 4
            + seq * dk * 2 + dk * seq * 4 + seq * dvx * 2 + dk * dvx * 4 + 8 * chunk * dvx * 4)
    return pl.pallas_call(
        functools.partial(_mlstm_kernel, seq=seq, chunk=chunk, dk=dk, dv=dv),
        grid=(bsz, nh),
        in_specs=[
            pl.BlockSpec((1, seq, dk), lambda b, h: (b, 0, h)),
            pl.BlockSpec((1, seq, dk), lambda b, h: (b, 0, k_blk + h)),
            pl.BlockSpec((1, seq, dv), lambda b, h: (b, 0, v_blk + h)),
            pl.BlockSpec((1, MLSTM_GATES, seq), lambda b, h: (b, 0, 0)),
        ],
        out_specs=pl.BlockSpec((1, seq, dv), lambda b, h: (b, 0, h)),
        out_shape=jax.ShapeDtypeStruct((bsz, seq, nh * dv), F32),
        scratch_shapes=[
            pltpu.VMEM((seq, dk), BF16),
            pltpu.VMEM((dk, seq), F32),
            pltpu.VMEM((seq, dvx), BF16),
            pltpu.VMEM((dk, dvx), F32),
        ],
        compiler_params=_params(("arbitrary", "arbitrary"), vmem),
        name="bidir_mlstm",
    )(proj, proj, proj, gates_t)


def kernel(x, c, positions, attn_w_qkv, attn_b_qkv, attn_sink, attn_w_o, attn_b_o, mlstm_w_in, mlstm_b_in,
           mlstm_norm_w, mlstm_w_o, mlstm_b_o, mod_w, mod_b, mlp_w1, mlp_b1, mlp_w2, mlp_b2,
           ln_mix_g, ln_mix_b, ln_mlp_g, ln_mlp_b):
    bsz, seq, d = x.shape
    depth = mod_w.shape[0]
    assert depth == DEPTH
    m = bsz * seq
    d_ff = mlp_w1.shape[-1]
    ml_main = mlstm_w_in.shape[-1] - MLSTM_GATES
    ml_dv = mlstm_w_o.shape[1] // MLSTM_HEADS
    ml_dk = (ml_main - 2 * MLSTM_HEADS * ml_dv) // (2 * MLSTM_HEADS)

    c_pad = jnp.pad(c, ((0, V7X_SUBLANES - bsz), (0, 0)))
    mod = _mod_call(c_pad, mod_w, mod_b)[:, :bsz]
    mod = mod.reshape(depth, bsz, 6, 1, d)
    sh_m, sc_m, g_m, sh_f, sc_f, g_f = (mod[:, :, j] for j in range(6))

    cos, sin = _rope_call(positions)
    x2 = x.reshape(m, d)
    h = _modulate_call(x2, sc_m[0], sh_m[0], seq)

    mlstm_w_in_t = jnp.swapaxes(mlstm_w_in, 1, 2)

    for i in range(depth):
        j = i // 2
        if i % 2 == 0:
            proj, w_o_bf = _mm_call(h, attn_w_qkv, j, attn_b_qkv[j], n_cols=attn_w_qkv.shape[-1], out_dtype=F32,
                                    side=(attn_w_o, j), row_splits=2, name="attn_qkv_proj")
            a = _attn_call(proj.reshape(bsz, seq, -1), cos, sin, attn_sink[j], bsz, seq).reshape(m, d)
            x2, h = _mm_ln_call(a, w_o_bf, attn_b_o[j], x2, g_m[i], ln_mix_g[i], ln_mix_b[i],
                                (sc_f[i], sh_f[i]), rows_per_batch=seq, tm=LN_PROJ_ROW_TILE, name="attn_out_ln")
        else:
            proj, w_o_bf = _mm_call(h, mlstm_w_in_t, j, mlstm_b_in[j], n_cols=ml_main, out_dtype=F32,
                                    w_transposed=True, side=(mlstm_w_o, j), row_splits=2, name="mlstm_in_proj")
            gates_t = _gates_call(h, mlstm_w_in_t, j, mlstm_b_in[j][ml_main:].reshape(MLSTM_GATES, 1), bsz, seq)
            hs = _mlstm_call(proj.reshape(bsz, seq, ml_main), gates_t, bsz, seq, ml_dk, ml_dv).reshape(m, d)
            x2, h = _mm_ln_call(hs, w_o_bf, mlstm_b_o[j], x2, g_m[i], ln_mix_g[i], ln_mix_b[i],
                                (sc_f[i], sh_f[i]), rows_per_batch=seq, tm=LN_PROJ_ROW_TILE, name="mlstm_out_ln",
                                ogate=(proj, (ml_main - d) // d), normw=mlstm_norm_w[j])
        u, w2_bf = _mm_call(h, mlp_w1, i, mlp_b1[i], n_cols=d_ff, out_dtype=BF16, act="relu2", side=(mlp_w2, i),
                            row_splits=2, name="mlp_up")
        nxt = (sc_m[i + 1], sh_m[i + 1]) if i + 1 < depth else None
        x2, h = _mm_ln_call(u, w2_bf, mlp_b2[i], x2, g_f[i], ln_mlp_g[i], ln_mlp_b[i], nxt,
                            rows_per_batch=seq, tm=MLP_DOWN_ROW_TILE, name="mlp_down_ln")
    return x2.reshape(bsz, seq, d)
```
